```python
import functools
import jax, jax.numpy as jnp
from jax import lax
import numpy as np

D_MODEL = 1024
BATCH = 2
SEQ = 16384
DEPTH = 4
DEC_BATCH = 16
DEC_SEQ = 32
PAST_LEN = 2048

CHUNK = 64
Q_BLOCK = 128
MIX_WIDTH = D_MODEL
SB_WIDTH = MIX_WIDTH // 2
RET_WIDTH = MIX_WIDTH - SB_WIDTH
SB_HEAD_DIM = 64
SB_HEADS = SB_WIDTH // SB_HEAD_DIM
RET_HEAD_DIM = 128
RET_HEADS = RET_WIDTH // RET_HEAD_DIM
IN_SPLITS = (SB_WIDTH, 2 * SB_WIDTH, 3 * SB_WIDTH, 3 * SB_WIDTH + RET_WIDTH,
             3 * SB_WIDTH + 2 * RET_WIDTH, 3 * SB_WIDTH + 3 * RET_WIDTH)
IN_WIDTH = 3 * SB_WIDTH + 4 * RET_WIDTH
D_FF = ((8 * D_MODEL + 3 * 256 - 1) // (3 * 256)) * 256
N_MOD = 6
ROPE_BASE = 10000.0
EPS = 1e-6

kernel_name = 'stickbreak_retention_hybrid_stream_step'


def rms_norm(x, g):
    xf = x.astype(jnp.float32)
    y = xf * lax.rsqrt(jnp.mean(xf * xf, axis=-1, keepdims=True) + EPS)
    return (y * g.astype(jnp.float32)).astype(x.dtype)


def rotary(x, pos):
    half = x.shape[-1] // 2
    inv = ROPE_BASE ** (-jnp.arange(half, dtype=jnp.float32) / half)
    ang = pos.astype(jnp.float32)[:, None] * inv[None, :]
    cos, sin = jnp.cos(ang)[:, None, :], jnp.sin(ang)[:, None, :]
    xf = x.astype(jnp.float32)
    x1, x2 = xf[..., :half], xf[..., half:]
    return jnp.concatenate([x1 * cos - x2 * sin, x1 * sin + x2 * cos], axis=-1).astype(x.dtype)


def retention_log_decay():
    return jnp.log1p(-jnp.exp2(-5.0 - jnp.arange(RET_HEADS, dtype=jnp.float32)))


def modulation(c, w_ada, b_ada):
    mod = jax.nn.silu(c) @ w_ada + b_ada
    return tuple(m[:, None, :] for m in jnp.split(mod, N_MOD, axis=-1))


def project_heads(h, w_in, pos):
    B, T, _ = h.shape
    q_sb, k_sb, v_sb, q_r, k_r, v_r, gate = jnp.split(h @ w_in, IN_SPLITS, axis=-1)
    q_sb = q_sb.reshape(B, T, SB_HEADS, SB_HEAD_DIM)
    k_sb = k_sb.reshape(B, T, SB_HEADS, SB_HEAD_DIM)
    v_sb = v_sb.reshape(B, T, SB_HEADS, SB_HEAD_DIM)
    q_r = rotary(q_r.reshape(B, T, RET_HEADS, RET_HEAD_DIM), pos)
    k_r = rotary(k_r.reshape(B, T, RET_HEADS, RET_HEAD_DIM), pos) * (RET_HEAD_DIM ** -0.5)
    v_r = v_r.reshape(B, T, RET_HEADS, RET_HEAD_DIM)
    return q_sb, k_sb, v_sb, q_r, k_r, v_r, gate


def stick_breaking_weights(z, mask, cum):
    log_1m = jnp.where(mask, jax.nn.log_sigmoid(-z), 0.0)
    between = lax.cumsum(log_1m, axis=3, reverse=True) - log_1m + cum[..., None]
    w = jnp.where(mask, jnp.exp(jax.nn.log_sigmoid(z) + between), 0.0)
    return w, jnp.sum(log_1m, axis=3)


def stick_breaking_prompt(q, k, v):
    B, S, H, d = q.shape
    scale = d ** -0.5
    local = jnp.arange(Q_BLOCK)
    diag_mask = local[None, :] < local[:, None]

    def query_block(i):
        qi = lax.dynamic_slice_in_dim(q, i * Q_BLOCK, Q_BLOCK, axis=1)

        def cond(carry):
            return carry[0] >= 0

        def body(carry):
            j, acc, cum = carry
            kj = lax.dynamic_slice_in_dim(k, j * Q_BLOCK, Q_BLOCK, axis=1)
            vj = lax.dynamic_slice_in_dim(v, j * Q_BLOCK, Q_BLOCK, axis=1)
            z = jnp.einsum('bqhd,bkhd->bhqk', qi, kj, preferred_element_type=jnp.float32) * scale
            mask = jnp.logical_or(j < i, diag_mask)
            w, block_sum = stick_breaking_weights(z, mask, cum)
            acc = acc + jnp.einsum('bhqk,bkhd->bqhd', w, vj.astype(jnp.float32))
            return j - 1, acc, cum + block_sum

        init = (i, jnp.zeros((B, Q_BLOCK, H, d), jnp.float32),
                jnp.zeros((B, H, Q_BLOCK), jnp.float32))
        _, acc, _ = lax.while_loop(cond, body, init)
        return acc.astype(q.dtype)

    blocks = lax.map(query_block, jnp.arange(S // Q_BLOCK, dtype=jnp.int32))
    return blocks.transpose(1, 0, 2, 3, 4).reshape(B, S, H, d)


def stick_breaking_dense(q, k, v, q_pos, k_pos):
    z = jnp.einsum('bqhd,bkhd->bhqk', q, k, preferred_element_type=jnp.float32) * (q.shape[-1] ** -0.5)
    mask = k_pos[None, :] < q_pos[:, None]
    w, _ = stick_breaking_weights(z, mask, jnp.zeros(z.shape[:3], jnp.float32))
    return jnp.einsum('bhqk,bkhd->bqhd', w, v.astype(jnp.float32)).astype(q.dtype)


def retention_chunk(q, k, v, state, log_gamma):
    L = q.shape[1]
    q, k, v = (a.astype(jnp.float32) for a in (q, k, v))
    idx = jnp.arange(L, dtype=jnp.float32)
    diff = idx[:, None] - idx[None, :]
    decay = jnp.where(diff >= 0, jnp.exp(jnp.maximum(diff, 0.0)[None] * log_gamma[:, None, None]), 0.0)
    scores = jnp.einsum('blhd,bmhd->bhlm', q, k) * decay[None]
    inner = jnp.einsum('bhlm,bmhe->blhe', scores, v)
    q_decay = jnp.exp((idx + 1.0)[:, None] * log_gamma[None, :])
    cross = jnp.einsum('blhd,bhde->blhe', q, state) * q_decay[None, :, :, None]
    k_decay = jnp.exp((L - 1.0 - idx)[:, None] * log_gamma[None, :])
    new_state = (jnp.exp(L * log_gamma)[None, :, None, None] * state
                 + jnp.einsum('blhd,blhe->bhde', k * k_decay[None, :, :, None], v))
    return inner + cross, new_state


def retention_prompt(q, k, v):
    B, S, H, dk = q.shape
    dv = v.shape[-1]
    n_chunks = S // CHUNK
    log_gamma = retention_log_decay()

    def chunks(a):
        return a.reshape(B, n_chunks, CHUNK, H, a.shape[-1]).swapaxes(0, 1)

    def body(state, xs):
        o, state = retention_chunk(xs[0], xs[1], xs[2], state, log_gamma)
        return state, o

    state, o = lax.scan(body, jnp.zeros((B, H, dk, dv), jnp.float32), (chunks(q), chunks(k), chunks(v)))
    return o.swapaxes(0, 1).reshape(B, S, H, dv), state


def core_prompt(q_sb, k_sb, v_sb, q_r, k_r, v_r):
    o_sb = stick_breaking_prompt(q_sb, k_sb, v_sb)
    o_r, st = retention_prompt(q_r, k_r, v_r)
    return o_sb, o_r.astype(q_sb.dtype), (k_sb, v_sb, st.astype(q_sb.dtype))


def core_sample(cache_k, cache_v, state, q_sb, k_sb, v_sb, q_r, k_r, v_r):
    T = q_sb.shape[1]
    past = cache_k.shape[1]
    k_all = jnp.concatenate([cache_k.astype(k_sb.dtype), k_sb], axis=1)
    v_all = jnp.concatenate([cache_v.astype(v_sb.dtype), v_sb], axis=1)
    q_pos = past + jnp.arange(T)
    k_pos = jnp.arange(past + T)
    o_sb = stick_breaking_dense(q_sb, k_all, v_all, q_pos, k_pos)
    o_r, st = retention_chunk(q_r, k_r, v_r, state.astype(jnp.float32), retention_log_decay())
    return o_sb, o_r.astype(q_sb.dtype), (k_sb, v_sb, st.astype(q_sb.dtype))


def merge_heads(o_sb, o_ret, gate, g_sb, g_ret, w_out):
    B, T = o_sb.shape[:2]
    o_sb = rms_norm(o_sb.reshape(B, T, SB_WIDTH), g_sb)
    o_ret = rms_norm(o_ret, g_ret).reshape(B, T, RET_WIDTH) * jax.nn.silu(gate)
    return jnp.concatenate([o_sb, o_ret], axis=-1) @ w_out


def swiglu(h, w_ff_in, w_ff_out):
    g, u = jnp.split(h @ w_ff_in, 2, axis=-1)
    return (jax.nn.silu(g) * u) @ w_ff_out


def trunk_layer(x, c, pos, core, g_mix, g_ffn, w_ada, b_ada, w_in, g_sb, g_ret, w_out, w_ff_in, w_ff_out):
    sh1, sc1, gt1, sh2, sc2, gt2 = modulation(c, w_ada, b_ada)
    h = rms_norm(x, g_mix) * (1 + sc1) + sh1
    q_sb, k_sb, v_sb, q_r, k_r, v_r, gate = project_heads(h, w_in, pos)
    o_sb, o_ret, new_state = core(q_sb, k_sb, v_sb, q_r, k_r, v_r)
    x = x + gt1 * merge_heads(o_sb, o_ret, gate, g_sb, g_ret, w_out)
    h = rms_norm(x, g_ffn) * (1 + sc2) + sh2
    x = x + gt2 * swiglu(h, w_ff_in, w_ff_out)
    return x, new_state


def setup_inputs(seed: int = 0) -> dict:
    key = jax.random.key(seed)
    ks = jax.random.split(key, 18)

    def nrm(k, shape, s=1.0):
        return s * jax.random.normal(k, shape, jnp.float32)

    return {
        'x_prompt': nrm(ks[0], (BATCH, SEQ, D_MODEL)),
        'x_sample': nrm(ks[1], (DEC_BATCH, DEC_SEQ, D_MODEL)),
        'cache_sb_k': nrm(ks[2], (DEPTH, DEC_BATCH, PAST_LEN, SB_HEADS, SB_HEAD_DIM)),
        'cache_sb_v': nrm(ks[3], (DEPTH, DEC_BATCH, PAST_LEN, SB_HEADS, SB_HEAD_DIM)),
        'state_ret': nrm(ks[4], (DEPTH, DEC_BATCH, RET_HEADS, RET_HEAD_DIM, RET_HEAD_DIM), 0.5),
        'c_prompt': nrm(ks[5], (BATCH, D_MODEL)),
        'c_sample': nrm(ks[6], (DEC_BATCH, D_MODEL)),
        'g_norm_mix': 1.0 + nrm(ks[7], (DEPTH, D_MODEL), 0.02),
        'g_norm_ffn': 1.0 + nrm(ks[8], (DEPTH, D_MODEL), 0.02),
        'w_ada': nrm(ks[9], (DEPTH, D_MODEL, N_MOD * D_MODEL), 0.5 * D_MODEL ** -0.5),
        'b_ada': nrm(ks[10], (DEPTH, N_MOD * D_MODEL), 0.02),
        'w_in': nrm(ks[11], (DEPTH, D_MODEL, IN_WIDTH), D_MODEL ** -0.5),
        'g_sb_out': 1.0 + nrm(ks[12], (DEPTH, SB_WIDTH), 0.02),
        'g_ret_out': 1.0 + nrm(ks[13], (DEPTH, RET_HEADS, RET_HEAD_DIM), 0.02),
        'w_out': nrm(ks[14], (DEPTH, MIX_WIDTH, D_MODEL), MIX_WIDTH ** -0.5),
        'w_ff_in': nrm(ks[15], (DEPTH, D_MODEL, 2 * D_FF), D_MODEL ** -0.5),
        'w_ff_out': nrm(ks[16], (DEPTH, D_FF, D_MODEL), D_FF ** -0.5),
        'g_final': 1.0 + nrm(ks[17], (D_MODEL,), 0.02),
    }


def reference(x_prompt, x_sample, cache_sb_k, cache_sb_v, state_ret, c_prompt, c_sample,
              g_norm_mix, g_norm_ffn, w_ada, b_ada, w_in, g_sb_out, g_ret_out, w_out,
              w_ff_in, w_ff_out, g_final):
    pos_p = jnp.arange(x_prompt.shape[1])
    pos_s = cache_sb_k.shape[2] + jnp.arange(x_sample.shape[1])
    xp, xs = x_prompt, x_sample
    kp, vp, rp, ksm, vsm, rsm = [], [], [], [], [], []
    for l in range(DEPTH):
        weights = (g_norm_mix[l], g_norm_ffn[l], w_ada[l], b_ada[l], w_in[l], g_sb_out[l],
                   g_ret_out[l], w_out[l], w_ff_in[l], w_ff_out[l])
        xp, (k_new, v_new, r_new) = trunk_layer(xp, c_prompt, pos_p, core_prompt, *weights)
        kp.append(k_new)
        vp.append(v_new)
        rp.append(r_new)
        core_s = functools.partial(core_sample, cache_sb_k[l], cache_sb_v[l], state_ret[l])
        xs, (k_new, v_new, r_new) = trunk_layer(xs, c_sample, pos_s, core_s, *weights)
        ksm.append(k_new)
        vsm.append(v_new)
        rsm.append(r_new)
    y_prompt = rms_norm(xp, g_final)
    y_sample = rms_norm(xs, g_final)
    return (y_prompt, y_sample, jnp.stack(kp), jnp.stack(vp), jnp.stack(rp),
            jnp.stack(ksm), jnp.stack(vsm), jnp.stack(rsm))
```

```python
import functools
import math

import jax
import jax.numpy as jnp
from jax import lax
from jax.experimental import pallas as pl
from jax.experimental.pallas import tpu as pltpu

F32 = jnp.float32
BF16 = jnp.bfloat16

D_MODEL = 1024
SB_WIDTH = 512
RET_WIDTH = 512
SB_HEAD_DIM = 64
SB_HEADS = SB_WIDTH // SB_HEAD_DIM
RET_HEAD_DIM = 128
RET_HEADS = RET_WIDTH // RET_HEAD_DIM
IN_WIDTH = 3 * SB_WIDTH + 4 * RET_WIDTH
D_FF = 2816
N_MOD = 6
ROPE_BASE = 10000.0
EPS = 1e-6
LOG2E = 1.4426950408889634
LN2 = 0.6931471805599453

LANES = 128
MXU_DIM = 256
VMEM_LIMIT_BYTES = 56 * 1024 * 1024

TOKEN_TILE = 512
SB_TILE = MXU_DIM
RET_CHUNK = MXU_DIM
FF_CHUNK = MXU_DIM
SOFTPLUS_CLAMP = 30.0


def _dot(a, b):
    return jnp.dot(a, b, preferred_element_type=F32)


def _dot_nt(a, b):
    return lax.dot_general(a, b, (((1,), (1,)), ((), ())), preferred_element_type=F32)


def _dot_tn(a, b):
    return lax.dot_general(a, b, (((0,), (0,)), ((), ())), preferred_element_type=F32)


def _split_bf16(x):
    hi = x.astype(BF16)
    lo = (x - hi.astype(F32)).astype(BF16)
    return hi, lo


def _rms(x, g):
    return x * lax.rsqrt(jnp.mean(x * x, axis=-1, keepdims=True) + EPS) * g


def _softplus2(z2):
    e = jnp.exp2(jnp.minimum(z2, SOFTPLUS_CLAMP))
    return jnp.maximum(z2, jnp.log(1.0 + e) * LOG2E)


def _params(semantics):
    return pltpu.CompilerParams(dimension_semantics=semantics, vmem_limit_bytes=VMEM_LIMIT_BYTES)


def _resident(shape):
    zeros = (0,) * len(shape)
    return pl.BlockSpec(shape, lambda *_: zeros, pipeline_mode=pl.Buffered(1))


def _mod_kernel(c_ref, w_ref, b_ref, o_ref):
    c = c_ref[...]
    a = c * jax.nn.sigmoid(c)
    a_hi, a_lo = _split_bf16(a)
    w_hi, w_lo = _split_bf16(w_ref[0])
    o_ref[0] = _dot(a_hi, w_hi) + _dot(a_lo, w_hi) + _dot(a_hi, w_lo) + b_ref[0]


def _modulation(c, w_ada, b_ada):
    depth = w_ada.shape[0]
    rows = c.shape[0]
    return pl.pallas_call(
        _mod_kernel,
        grid=(depth, N_MOD),
        in_specs=[
            pl.BlockSpec((rows, D_MODEL), lambda l, j: (0, 0)),
            pl.BlockSpec((1, D_MODEL, D_MODEL), lambda l, j: (l, 0, j)),
            pl.BlockSpec((1, 1, D_MODEL), lambda l, j: (l, 0, j)),
        ],
        out_specs=pl.BlockSpec((1, rows, D_MODEL), lambda l, j: (l, 0, j)),
        out_shape=jax.ShapeDtypeStruct((depth, rows, N_MOD * D_MODEL), F32),
        compiler_params=_params(("arbitrary", "arbitrary")),
        name="modulation",
    )(c, w_ada, b_ada.reshape(depth, 1, N_MOD * D_MODEL))


def _rope_kernel(inv_ref, cos_ref, sin_ref, *, rows, pos0):
    row = lax.broadcasted_iota(jnp.int32, (rows, LANES), 0) + (pl.program_id(0) * rows + pos0)
    lane = lax.broadcasted_iota(jnp.int32, (rows, LANES), 1)
    ang = row.astype(F32) * inv_ref[...]
    sin = jnp.sin(ang)
    cos_ref[...] = jnp.cos(ang)
    sin_ref[...] = jnp.where(lane < RET_HEAD_DIM // 2, -sin, sin)


def _rope_tables(n, pos0):
    half = RET_HEAD_DIM // 2
    inv = ROPE_BASE ** (-jnp.arange(half, dtype=F32) / half)
    inv2 = jnp.concatenate([inv, inv]).reshape(1, LANES)
    rows = min(n, 2048)
    assert n % rows == 0
    return pl.pallas_call(
        functools.partial(_rope_kernel, rows=rows, pos0=pos0),
        grid=(n // rows,),
        in_specs=[pl.BlockSpec((1, LANES), lambda i: (0, 0))],
        out_specs=[pl.BlockSpec((rows, LANES), lambda i: (i, 0))] * 2,
        out_shape=[jax.ShapeDtypeStruct((n, LANES), F32)] * 2,
        compiler_params=_params(("arbitrary",)),
        name="rope_tables",
    )(inv2)


def _proj_kernel(x_ref, sc_ref, sh_ref, g_ref, cos_ref, sin_ref, w_ref, *rest, tm, kb, prompt):
    if prompt:
        wvt_ref, q_ref, k_ref, kb_ref, v_ref, vt_ref, qr_ref, kr_ref, vr_ref, gate_ref = rest
    else:
        q_ref, k_ref, v_ref, qr_ref, kr_ref, vr_ref, gate_ref = rest
    h = _rms(x_ref[...], g_ref[...]) * (1.0 + sc_ref[0]) + sh_ref[0]
    hb = h.astype(BF16)

    def seg(a, b):
        return _dot(hb, w_ref[:, a:b])

    q_ref[...] = (seg(0, SB_WIDTH) * (SB_HEAD_DIM ** -0.5 * LOG2E)).astype(BF16)
    k = seg(SB_WIDTH, 2 * SB_WIDTH)
    k_ref[...] = k
    v_ref[...] = seg(2 * SB_WIDTH, 3 * SB_WIDTH)
    if prompt:
        kb_ref[...] = k.astype(BF16)
        vt = _dot_nt(wvt_ref[...], hb).astype(BF16)
        for hp in range(SB_WIDTH // LANES):
            for j in range(tm // kb):
                vt_ref[0, hp, j] = vt[hp * LANES:(hp + 1) * LANES, j * kb:(j + 1) * kb]
    cos = cos_ref[...]
    sin = sin_ref[...]
    base = 3 * SB_WIDTH
    qr = seg(base, base + RET_WIDTH)
    kr = seg(base + RET_WIDTH, base + 2 * RET_WIDTH)
    for hh in range(RET_HEADS):
        sl = slice(hh * RET_HEAD_DIM, (hh + 1) * RET_HEAD_DIM)
        qh = qr[:, sl]
        kh = kr[:, sl]
        qr_ref[:, sl] = (qh * cos + pltpu.roll(qh, RET_HEAD_DIM // 2, 1) * sin).astype(BF16)
        kr_ref[:, sl] = ((kh * cos + pltpu.roll(kh, RET_HEAD_DIM // 2, 1) * sin)
                         * RET_HEAD_DIM ** -0.5).astype(BF16)
    vr_ref[...] = seg(base + 2 * RET_WIDTH, base + 3 * RET_WIDTH).astype(BF16)
    gate_ref[...] = seg(base + 3 * RET_WIDTH, base + 4 * RET_WIDTH)


def _proj(x, sc, sh, g, cos, sin, w_in_b, wvt_b, *, batch, prompt):
    n = x.shape[0]
    tm = min(TOKEN_TILE, n)
    tiles = n // tm
    per_seq = max(tiles // batch, 1)
    pos_tiles = cos.shape[0] // tm
    kb = SB_TILE
    mrows = sc.shape[1]

    def row(i):
        return (i, 0)

    def mod(i):
        return (i // per_seq if mrows == 1 else i, 0, 0)

    def pos(i):
        return (i % pos_tiles, 0)

    in_specs = [
        pl.BlockSpec((tm, D_MODEL), row),
        pl.BlockSpec((1, mrows, D_MODEL), mod),
        pl.BlockSpec((1, mrows, D_MODEL), mod),
        _resident((1, D_MODEL)),
        pl.BlockSpec((tm, LANES), pos),
        pl.BlockSpec((tm, LANES), pos),
        _resident((D_MODEL, IN_WIDTH)),
    ]
    args = [x, sc, sh, g, cos, sin, w_in_b]
    wide_b = jax.ShapeDtypeStruct((n, SB_WIDTH), BF16)
    wide_f = jax.ShapeDtypeStruct((n, SB_WIDTH), F32)
    wide = pl.BlockSpec((tm, SB_WIDTH), row)
    if prompt:
        in_specs.append(_resident((SB_WIDTH, D_MODEL)))
        args.append(wvt_b)
        nkb = n // batch // kb
        hps = SB_WIDTH // LANES
        out_shape = [wide_b, wide_f, wide_b, wide_f,
                     jax.ShapeDtypeStruct((batch, hps, nkb, LANES, kb), BF16),
                     wide_b, wide_b, wide_b, wide_f]
        out_specs = [wide, wide, wide, wide,
                     pl.BlockSpec((1, hps, tm // kb, LANES, kb),
                                  lambda i: (i // per_seq, 0, i % per_seq, 0, 0)),
                     wide, wide, wide, wide]
    else:
        out_shape = [wide_b, wide_f, wide_f, wide_b, wide_b, wide_b, wide_f]
        out_specs = [wide] * 7
    return pl.pallas_call(
        functools.partial(_proj_kernel, tm=tm, kb=kb, prompt=prompt),
        grid=(tiles,),
        in_specs=in_specs,
        out_specs=out_specs,
        out_shape=out_shape,
        compiler_params=_params(("arbitrary",)),
        name="proj_prompt" if prompt else "proj_sample",
    )(*args)


def _sb_prompt_kernel(q_ref, k_ref, vt_ref, o_ref, acc_ref, cum_ref, *, t):
    i = pl.program_id(2)
    hd = SB_HEAD_DIM
    q = q_ref[0]
    lane = lax.broadcasted_iota(jnp.int32, (t, LANES), 1)
    qm = (jnp.where(lane < hd, q, jnp.zeros_like(q)), jnp.where(lane >= hd, q, jnp.zeros_like(q)))
    kk = lax.broadcasted_iota(jnp.int32, (t, t), 0)
    qq = lax.broadcasted_iota(jnp.int32, (t, t), 1)
    tri = (qq >= kk).astype(BF16)
    valid = kk < qq
    acc_ref[...] = jnp.zeros_like(acc_ref)
    cum_ref[...] = jnp.zeros_like(cum_ref)

    def block(j, masked):
        kblk = k_ref[0, pl.ds(pl.multiple_of(j * t, t), t), :]
        vblk = vt_ref[0, 0, j]
        for h in range(2):
            zt = _dot_nt(kblk, qm[h])
            s = _softplus2(zt)
            if masked:
                s = jnp.where(valid, s, 0.0)
            s_hi, s_lo = _split_bf16(s)
            r = _dot(tri, s_hi) + _dot(tri, s_lo)
            cum = cum_ref[h:h + 1, :]
            w = jnp.exp2(zt - r + cum)
            if masked:
                w = jnp.where(valid, w, 0.0)
            acc_ref[h * hd:(h + 1) * hd, :] += _dot(vblk[h * hd:(h + 1) * hd, :], w.astype(BF16))
            cum_ref[h:h + 1, :] = cum - r[0:1, :]

    block(i, True)

    def body(n, carry):
        block(i - 1 - n, False)
        return carry

    lax.fori_loop(0, i, body, 0)
    o_ref[0] = acc_ref[...].T


def _sb_prompt(q, kb16, vt, *, batch):
    n = q.shape[0]
    s = n // batch
    t = SB_TILE
    nq = s // t
    hps = SB_WIDTH // LANES
    q3 = q.reshape(batch, s, SB_WIDTH)
    k3 = kb16.reshape(batch, s, SB_WIDTH)
    out = pl.pallas_call(
        functools.partial(_sb_prompt_kernel, t=t),
        grid=(batch, hps, nq),
        in_specs=[
            pl.BlockSpec((1, t, LANES), lambda b, hp, i: (b, i, hp)),
            pl.BlockSpec((1, s, LANES), lambda b, hp, i: (b, 0, hp)),
            pl.BlockSpec((1, 1, nq, LANES, t), lambda b, hp, i: (b, hp, 0, 0, 0)),
        ],
        out_specs=pl.BlockSpec((1, t, LANES), lambda b, hp, i: (b, i, hp)),
        out_shape=jax.ShapeDtypeStruct((batch, s, SB_WIDTH), F32),
        scratch_shapes=[pltpu.VMEM((LANES, t), F32), pltpu.VMEM((8, t), F32)],
        compiler_params=_params(("arbitrary", "arbitrary", "arbitrary")),
        name="sb_prompt",
    )(q3, k3, vt)
    return out.reshape(n, SB_WIDTH)


def _sb_sample_kernel(q_ref, kn_ref, vn_ref, ck_ref, cv_ref, o_ref, *, tq, past, kb):
    hd = SB_HEAD_DIM
    lane = lax.broadcasted_iota(jnp.int32, (tq, LANES), 1)
    qi = lax.broadcasted_iota(jnp.int32, (tq, tq), 0)
    ki = lax.broadcasted_iota(jnp.int32, (tq, tq), 1)
    valid = ki < qi
    tri_new = (qi >= ki).astype(BF16)
    a = lax.broadcasted_iota(jnp.int32, (kb, kb), 0)
    b = lax.broadcasted_iota(jnp.int32, (kb, kb), 1)
    tri = (a >= b).astype(BF16)

    def cumsum_right(s, m):
        s_hi, s_lo = _split_bf16(s)
        return _dot(s_hi, m) + _dot(s_lo, m)

    for hp in range(SB_WIDTH // LANES):
        sl = slice(hp * LANES, (hp + 1) * LANES)
        q2 = q_ref[:, sl]
        kn = kn_ref[:, sl].astype(BF16)
        vn = vn_ref[:, sl].astype(BF16)
        outs = []
        for h in range(2):
            keep = (lane < hd) if h == 0 else (lane >= hd)
            qm = jnp.where(keep, q2, jnp.zeros_like(q2))
            z = _dot_nt(qm, kn)
            s = jnp.where(valid, _softplus2(z), 0.0)
            r = cumsum_right(s, tri_new)
            w = jnp.where(valid, jnp.exp2(z - r), 0.0)
            acc = _dot(w.astype(BF16), vn)
            cum = -r[:, 0:1]

            def body(n, carry):
                acc, cum = carry
                start = pl.multiple_of(past - (n + 1) * kb, kb)
                kblk = ck_ref[0, 0, pl.ds(start, kb), sl].astype(BF16)
                vblk = cv_ref[0, 0, pl.ds(start, kb), sl].astype(BF16)
                z = _dot_nt(qm, kblk)
                r = cumsum_right(_softplus2(z), tri)
                w = jnp.exp2(z - r + cum)
                return acc + _dot(w.astype(BF16), vblk), cum - r[:, 0:1]

            acc, cum = lax.fori_loop(0, past // kb, body, (acc, cum))
            outs.append(acc)
        o_ref[:, sl] = jnp.where(lane < hd, outs[0], outs[1])


def _sb_sample(q, k_new, v_new, cache_k, cache_v, layer, *, batch):
    n = q.shape[0]
    tq = n // batch
    past = cache_k.shape[2]
    kb = min(MXU_DIM, past)
    assert past % kb == 0
    row = pl.BlockSpec((tq, SB_WIDTH), lambda b: (b, 0))
    cache = pl.BlockSpec((1, 1, past, SB_WIDTH), lambda b: (layer, b, 0, 0))
    return pl.pallas_call(
        functools.partial(_sb_sample_kernel, tq=tq, past=past, kb=kb),
        grid=(batch,),
        in_specs=[row, row, row, cache, cache],
        out_specs=row,
        out_shape=jax.ShapeDtypeStruct((n, SB_WIDTH), F32),
        compiler_params=_params(("arbitrary",)),
        name="sb_sample",
    )(q, k_new, v_new, cache_k, cache_v)


def _ret_log_gamma(h):
    return math.log1p(-2.0 ** (-5 - h))


def _ret_kernel(q_ref, k_ref, v_ref, s0_ref, o_ref, sout_ref, state, dec, qdec, kdec, *, c, nchunks):
    seq = pl.program_id(0)
    ci = pl.program_id(1)

    @pl.when(jnp.logical_and(seq == 0, ci == 0))
    def _():
        li = lax.broadcasted_iota(jnp.int32, (c, c), 0)
        mi = lax.broadcasted_iota(jnp.int32, (c, c), 1)
        diff = (li - mi).astype(F32)
        pos = lax.broadcasted_iota(jnp.int32, (c, RET_HEAD_DIM), 0).astype(F32)
        for h in range(RET_HEADS):
            lg = _ret_log_gamma(h)
            dec[h] = jnp.where(diff >= 0.0, jnp.exp(jnp.maximum(diff, 0.0) * lg), 0.0)
            qdec[h] = jnp.exp((pos + 1.0) * lg)
            kdec[h] = jnp.exp((c - 1.0 - pos) * lg)

    @pl.when(ci == 0)
    def _():
        state[...] = s0_ref[0]

    for h in range(RET_HEADS):
        sl = slice(h * RET_HEAD_DIM, (h + 1) * RET_HEAD_DIM)
        qh = q_ref[:, sl]
        kh = k_ref[:, sl]
        vh = v_ref[:, sl]
        st = state[h]
        scores = _dot_nt(qh, kh) * dec[h]
        inner = _dot(scores.astype(BF16), vh)
        cross = _dot(qh, st.astype(BF16)) * qdec[h]
        o_ref[:, sl] = inner + cross
        kd = (kh.astype(F32) * kdec[h]).astype(BF16)
        state[h] = math.exp(c * _ret_log_gamma(h)) * st + _dot_tn(kd, vh)

    @pl.when(ci == nchunks - 1)
    def _():
        sout_ref[0] = state[...]


def _retention(q, k, v, state0, *, batch):
    n = q.shape[0]
    t = n // batch
    c = min(RET_CHUNK, t)
    nchunks = t // c
    row = pl.BlockSpec((c, RET_WIDTH), lambda b, i: (b * nchunks + i, 0))
    st = pl.BlockSpec((1, RET_HEADS, RET_HEAD_DIM, RET_HEAD_DIM), lambda b, i: (b, 0, 0, 0))
    return pl.pallas_call(
        functools.partial(_ret_kernel, c=c, nchunks=nchunks),
        grid=(batch, nchunks),
        in_specs=[row, row, row, st],
        out_specs=[row, st],
        out_shape=[jax.ShapeDtypeStruct((n, RET_WIDTH), F32),
                   jax.ShapeDtypeStruct((batch, RET_HEADS, RET_HEAD_DIM, RET_HEAD_DIM), F32)],
        scratch_shapes=[pltpu.VMEM((RET_HEADS, RET_HEAD_DIM, RET_HEAD_DIM), F32),
                        pltpu.VMEM((RET_HEADS, c, c), F32),
                        pltpu.VMEM((RET_HEADS, c, RET_HEAD_DIM), F32),
                        pltpu.VMEM((RET_HEADS, c, RET_HEAD_DIM), F32)],
        compiler_params=_params(("arbitrary", "arbitrary")),
        name="retention",
    )(q, k, v, state0)


def _merge_ffn_kernel(x_ref, osb_ref, oret_ref, gate_ref, gt1_ref, sc2_ref, sh2_ref, gt2_ref,
                      gsb_ref, gret_ref, gffn_ref, gfin_ref, wout_ref, wfi_ref, wfo_ref,
                      *outs, final):
    xo_ref = outs[0]
    a = _rms(osb_ref[...], gsb_ref[...]).astype(BF16)
    mix = _dot(a, wout_ref[0:SB_WIDTH, :])
    gate = gate_ref[...]
    gate = gate * jax.nn.sigmoid(gate)
    for h in range(RET_HEADS):
        sl = slice(h * RET_HEAD_DIM, (h + 1) * RET_HEAD_DIM)
        r = (_rms(oret_ref[:, sl], gret_ref[:, sl]) * gate[:, sl]).astype(BF16)
        mix += _dot(r, wout_ref[SB_WIDTH + h * RET_HEAD_DIM:SB_WIDTH + (h + 1) * RET_HEAD_DIM, :])
    x1 = x_ref[...] + gt1_ref[0] * mix
    h2 = (_rms(x1, gffn_ref[...]) * (1.0 + sc2_ref[0]) + sh2_ref[0]).astype(BF16)
    ff = jnp.zeros_like(x1)
    for j in range(D_FF // FF_CHUNK):
        cols = slice(j * FF_CHUNK, (j + 1) * FF_CHUNK)
        ucols = slice(D_FF + j * FF_CHUNK, D_FF + (j + 1) * FF_CHUNK)
        g = _dot(h2, wfi_ref[:, cols])
        u = _dot(h2, wfi_ref[:, ucols])
        act = (g * jax.nn.sigmoid(g) * u).astype(BF16)
        ff += _dot(act, wfo_ref[cols, :])
    x2 = x1 + gt2_ref[0] * ff
    xo_ref[...] = x2
    if final:
        outs[1][...] = _rms(x2, gfin_ref[...])


def _merge_ffn(x, o_sb, o_ret, gate, gt1, sc2, sh2, gt2, g_sb, g_ret, g_ffn, g_fin,
               w_out_b, w_fi_b, w_fo_b, *, batch, final):
    n = x.shape[0]
    tm = min(TOKEN_TILE, n)
    tiles = n // tm
    per_seq = max(tiles // batch, 1)
    mrows = gt1.shape[1]

    def row(i):
        return (i, 0)

    def mod(i):
        return (i // per_seq if mrows == 1 else i, 0, 0)

    full = pl.BlockSpec((tm, D_MODEL), row)
    half = pl.BlockSpec((tm, SB_WIDTH), row)
    mspec = pl.BlockSpec((1, mrows, D_MODEL), mod)
    n_out = 2 if final else 1
    res = pl.pallas_call(
        functools.partial(_merge_ffn_kernel, final=final),
        grid=(tiles,),
        in_specs=[full, half, half, half, mspec, mspec, mspec, mspec,
                  _resident((1, SB_WIDTH)), _resident((1, RET_WIDTH)),
                  _resident((1, D_MODEL)), _resident((1, D_MODEL)),
                  _resident((D_MODEL, D_MODEL)), _resident((D_MODEL, 2 * D_FF)),
                  _resident((D_FF, D_MODEL))],
        out_specs=[full] * n_out,
        out_shape=[jax.ShapeDtypeStruct((n, D_MODEL), F32)] * n_out,
        compiler_params=_params(("arbitrary",)),
        name="merge_ffn",
    )(x, o_sb, o_ret, gate, gt1, sc2, sh2, gt2, g_sb, g_ret, g_ffn, g_fin, w_out_b, w_fi_b, w_fo_b)
    return res if final else (res[0], None)


def kernel(x_prompt, x_sample, cache_sb_k, cache_sb_v, state_ret, c_prompt, c_sample,
           g_norm_mix, g_norm_ffn, w_ada, b_ada, w_in, g_sb_out, g_ret_out, w_out,
           w_ff_in, w_ff_out, g_final):
    depth = w_in.shape[0]
    bp, sp, _ = x_prompt.shape
    bs, ts, _ = x_sample.shape
    past = cache_sb_k.shape[2]
    n_p = bp * sp
    n_s = bs * ts

    c_all = jnp.concatenate([c_prompt, c_sample], axis=0)
    pad = (-c_all.shape[0]) % 16
    mods = _modulation(jnp.pad(c_all, ((0, pad), (0, 0))), w_ada, b_ada)
    mods = mods.reshape(depth, -1, N_MOD, D_MODEL)
    mods_p = mods[:, :bp]
    mods_s = jnp.repeat(mods[:, bp:bp + bs], ts, axis=1)

    cos_p, sin_p = _rope_tables(sp, 0)
    cos_s, sin_s = _rope_tables(ts, past)
    cos_s = jnp.tile(cos_s, (bs, 1))
    sin_s = jnp.tile(sin_s, (bs, 1))

    cache_k = cache_sb_k.reshape(depth, bs, past, SB_WIDTH)
    cache_v = cache_sb_v.reshape(depth, bs, past, SB_WIDTH)
    zero_state = jnp.zeros((bp, RET_HEADS, RET_HEAD_DIM, RET_HEAD_DIM), F32)

    xp = x_prompt.reshape(n_p, D_MODEL)
    xs = x_sample.reshape(n_s, D_MODEL)
    kp, vp, rp, ks, vs, rs = [], [], [], [], [], []
    yp = ys = None
    for l in range(depth):
        final = l == depth - 1
        w_in_b = w_in[l].astype(BF16)
        wvt_b = w_in[l][:, 2 * SB_WIDTH:3 * SB_WIDTH].T.astype(BF16)
        w_out_b = w_out[l].astype(BF16)
        w_fi_b = w_ff_in[l].astype(BF16)
        w_fo_b = w_ff_out[l].astype(BF16)
        g_mix = g_norm_mix[l].reshape(1, D_MODEL)
        g_ffn = g_norm_ffn[l].reshape(1, D_MODEL)
        g_sb = g_sb_out[l].reshape(1, SB_WIDTH)
        g_ret = g_ret_out[l].reshape(1, RET_WIDTH)
        g_fin = g_final.reshape(1, D_MODEL)

        m = [mods_p[l, :, j].reshape(bp, 1, D_MODEL) for j in range(N_MOD)]
        q, k, k16, v, vt, qr, kr, vr, gate = _proj(
            xp, m[1], m[0], g_mix, cos_p, sin_p, w_in_b, wvt_b, batch=bp, prompt=True)
        o_sb = _sb_prompt(q, k16, vt, batch=bp)
        o_ret, st = _retention(qr, kr, vr, zero_state, batch=bp)
        xp, yp = _merge_ffn(xp, o_sb, o_ret, gate, m[2], m[4], m[3], m[5], g_sb, g_ret, g_ffn,
                            g_fin, w_out_b, w_fi_b, w_fo_b, batch=bp, final=final)
        kp.append(k)
        vp.append(v)
        rp.append(st)

        m = [mods_s[l, :, j].reshape(1, n_s, D_MODEL) for j in range(N_MOD)]
        q, k, v, qr, kr, vr, gate = _proj(
            xs, m[1], m[0], g_mix, cos_s, sin_s, w_in_b, None, batch=bs, prompt=False)
        o_sb = _sb_sample(q, k, v, cache_k, cache_v, l, batch=bs)
        o_ret, st = _retention(qr, kr, vr, state_ret[l], batch=bs)
        xs, ys = _merge_ffn(xs, o_sb, o_ret, gate, m[2], m[4], m[3], m[5], g_sb, g_ret, g_ffn,
                            g_fin, w_out_b, w_fi_b, w_fo_b, batch=bs, final=final)
        ks.append(k)
        vs.append(v)
        rs.append(st)

    def heads(a, b, t):
        return jnp.stack(a).reshape(depth, b, t, SB_HEADS, SB_HEAD_DIM)

    return (yp.reshape(bp, sp, D_MODEL), ys.reshape(bs, ts, D_MODEL),
            heads(kp, bp, sp), heads(vp, bp, sp), jnp.stack(rp),
            heads(ks, bs, ts), heads(vs, bs, ts), jnp.stack(rs))
```

```python
import functools
import math

import jax
import jax.numpy as jnp
from jax import lax
from jax.experimental import pallas as pl
from jax.experimental.pallas import tpu as pltpu

F32 = jnp.float32
BF16 = jnp.bfloat16

D_MODEL = 1024
SB_WIDTH = 512
RET_WIDTH = 512
SB_HEAD_DIM = 64
SB_HEADS = SB_WIDTH // SB_HEAD_DIM
RET_HEAD_DIM = 128
RET_HEADS = RET_WIDTH // RET_HEAD_DIM
IN_WIDTH = 3 * SB_WIDTH + 4 * RET_WIDTH
D_FF = 2816
N_MOD = 6
ROPE_BASE = 10000.0
EPS = 1e-6
LOG2E = 1.4426950408889634
LN2 = 0.6931471805599453

LANES = 128
MXU_DIM = 256
VMEM_LIMIT_BYTES = 56 * 1024 * 1024

TOKEN_TILE = 512
SB_TILE = MXU_DIM
RET_CHUNK = MXU_DIM
FF_CHUNK = MXU_DIM
SOFTPLUS_CLAMP = 30.0
SHIFT_OUT = 1e30


def _dot(a, b):
    return jnp.dot(a, b, preferred_element_type=F32)


def _dot_nt(a, b):
    return lax.dot_general(a, b, (((1,), (1,)), ((), ())), preferred_element_type=F32)


def _dot_tn(a, b):
    return lax.dot_general(a, b, (((0,), (0,)), ((), ())), preferred_element_type=F32)


def _split_bf16(x):
    hi = x.astype(BF16)
    lo = (x - hi.astype(F32)).astype(BF16)
    return hi, lo


def _rms(x, g):
    return x * lax.rsqrt(jnp.mean(x * x, axis=-1, keepdims=True) + EPS) * g


def _softplus2(z2):
    e = jnp.exp2(jnp.minimum(z2, SOFTPLUS_CLAMP))
    return jnp.maximum(z2, jnp.log(1.0 + e) * LOG2E)


def _params(semantics, flags=None):
    return pltpu.CompilerParams(dimension_semantics=semantics, vmem_limit_bytes=VMEM_LIMIT_BYTES,
                                flags=flags)


def _resident(shape):
    zeros = (0,) * len(shape)
    return pl.BlockSpec(shape, lambda *_: zeros, pipeline_mode=pl.Buffered(1))


def _mod_kernel(c_ref, w_ref, b_ref, o_ref):
    c = c_ref[...]
    a = c * jax.nn.sigmoid(c)
    a_hi, a_lo = _split_bf16(a)
    w_hi, w_lo = _split_bf16(w_ref[0])
    o_ref[0] = _dot(a_hi, w_hi) + _dot(a_lo, w_hi) + _dot(a_hi, w_lo) + b_ref[0]


def _modulation(c, w_ada, b_ada):
    depth = w_ada.shape[0]
    rows = c.shape[0]
    return pl.pallas_call(
        _mod_kernel,
        grid=(depth, N_MOD),
        in_specs=[
            pl.BlockSpec((rows, D_MODEL), lambda l, j: (0, 0)),
            pl.BlockSpec((1, D_MODEL, D_MODEL), lambda l, j: (l, 0, j)),
            pl.BlockSpec((1, 1, D_MODEL), lambda l, j: (l, 0, j)),
        ],
        out_specs=pl.BlockSpec((1, rows, D_MODEL), lambda l, j: (l, 0, j)),
        out_shape=jax.ShapeDtypeStruct((depth, rows, N_MOD * D_MODEL), F32),
        compiler_params=_params(("arbitrary", "arbitrary")),
        name="modulation",
    )(c, w_ada, b_ada.reshape(depth, 1, N_MOD * D_MODEL))


def _rope_kernel(inv_ref, cos_ref, sin_ref, *, rows, pos0):
    row = lax.broadcasted_iota(jnp.int32, (rows, LANES), 0) + (pl.program_id(0) * rows + pos0)
    lane = lax.broadcasted_iota(jnp.int32, (rows, LANES), 1)
    ang = row.astype(F32) * inv_ref[...]
    sin = jnp.sin(ang)
    cos_ref[...] = jnp.cos(ang)
    sin_ref[...] = jnp.where(lane < RET_HEAD_DIM // 2, -sin, sin)


def _rope_tables(n, pos0):
    half = RET_HEAD_DIM // 2
    inv = ROPE_BASE ** (-jnp.arange(half, dtype=F32) / half)
    inv2 = jnp.concatenate([inv, inv]).reshape(1, LANES)
    rows = min(n, 2048)
    assert n % rows == 0
    return pl.pallas_call(
        functools.partial(_rope_kernel, rows=rows, pos0=pos0),
        grid=(n // rows,),
        in_specs=[pl.BlockSpec((1, LANES), lambda i: (0, 0))],
        out_specs=[pl.BlockSpec((rows, LANES), lambda i: (i, 0))] * 2,
        out_shape=[jax.ShapeDtypeStruct((n, LANES), F32)] * 2,
        compiler_params=_params(("arbitrary",)),
        name="rope_tables",
    )(inv2)


def _proj_kernel(x_ref, sc_ref, sh_ref, g_ref, cos_ref, sin_ref, w_ref, *rest, tm, kb, prompt):
    if prompt:
        wvt_ref, q_ref, k_ref, kb_ref, v_ref, vt_ref, qr_ref, kr_ref, vr_ref, gate_ref = rest
    else:
        q_ref, k_ref, v_ref, qr_ref, kr_ref, vr_ref, gate_ref = rest
    h = _rms(x_ref[...], g_ref[...]) * (1.0 + sc_ref[0]) + sh_ref[0]
    hb = h.astype(BF16)

    def seg(a, b):
        return _dot(hb, w_ref[:, a:b])

    q_ref[...] = (seg(0, SB_WIDTH) * (SB_HEAD_DIM ** -0.5 * LOG2E)).astype(BF16)
    k = seg(SB_WIDTH, 2 * SB_WIDTH)
    k_ref[...] = k
    v_ref[...] = seg(2 * SB_WIDTH, 3 * SB_WIDTH)
    if prompt:
        kb_ref[...] = k.astype(BF16)
        vt = _dot_nt(wvt_ref[...], hb).astype(BF16)
        for hp in range(SB_WIDTH // LANES):
            for j in range(tm // kb):
                vt_ref[0, hp, j] = vt[hp * LANES:(hp + 1) * LANES, j * kb:(j + 1) * kb]
    cos = cos_ref[...]
    sin = sin_ref[...]
    base = 3 * SB_WIDTH
    qr = seg(base, base + RET_WIDTH)
    kr = seg(base + RET_WIDTH, base + 2 * RET_WIDTH)
    for hh in range(RET_HEADS):
        sl = slice(hh * RET_HEAD_DIM, (hh + 1) * RET_HEAD_DIM)
        qh = qr[:, sl]
        kh = kr[:, sl]
        qr_ref[:, sl] = (qh * cos + pltpu.roll(qh, RET_HEAD_DIM // 2, 1) * sin).astype(BF16)
        kr_ref[:, sl] = ((kh * cos + pltpu.roll(kh, RET_HEAD_DIM // 2, 1) * sin)
                         * RET_HEAD_DIM ** -0.5).astype(BF16)
    vr_ref[...] = seg(base + 2 * RET_WIDTH, base + 3 * RET_WIDTH).astype(BF16)
    gate_ref[...] = seg(base + 3 * RET_WIDTH, base + 4 * RET_WIDTH)


def _proj(x, sc, sh, g, cos, sin, w_in_b, wvt_b, *, batch, prompt):
    n = x.shape[0]
    tm = min(TOKEN_TILE, n)
    tiles = n // tm
    per_seq = max(tiles // batch, 1)
    pos_tiles = cos.shape[0] // tm
    kb = SB_TILE
    mrows = sc.shape[1]

    def row(i):
        return (i, 0)

    def mod(i):
        return (i // per_seq if mrows == 1 else i, 0, 0)

    def pos(i):
        return (i % pos_tiles, 0)

    in_specs = [
        pl.BlockSpec((tm, D_MODEL), row),
        pl.BlockSpec((1, mrows, D_MODEL), mod),
        pl.BlockSpec((1, mrows, D_MODEL), mod),
        _resident((1, D_MODEL)),
        pl.BlockSpec((tm, LANES), pos),
        pl.BlockSpec((tm, LANES), pos),
        _resident((D_MODEL, IN_WIDTH)),
    ]
    args = [x, sc, sh, g, cos, sin, w_in_b]
    wide_b = jax.ShapeDtypeStruct((n, SB_WIDTH), BF16)
    wide_f = jax.ShapeDtypeStruct((n, SB_WIDTH), F32)
    wide = pl.BlockSpec((tm, SB_WIDTH), row)
    if prompt:
        in_specs.append(_resident((SB_WIDTH, D_MODEL)))
        args.append(wvt_b)
        nkb = n // batch // kb
        hps = SB_WIDTH // LANES
        out_shape = [wide_b, wide_f, wide_b, wide_f,
                     jax.ShapeDtypeStruct((batch, hps, nkb, LANES, kb), BF16),
                     wide_b, wide_b, wide_b, wide_f]
        out_specs = [wide, wide, wide, wide,
                     pl.BlockSpec((1, hps, tm // kb, LANES, kb),
                                  lambda i: (i // per_seq, 0, i % per_seq, 0, 0)),
                     wide, wide, wide, wide]
    else:
        out_shape = [wide_b, wide_f, wide_f, wide_b, wide_b, wide_b, wide_f]
        out_specs = [wide] * 7
    return pl.pallas_call(
        functools.partial(_proj_kernel, tm=tm, kb=kb, prompt=prompt),
        grid=(tiles,),
        in_specs=in_specs,
        out_specs=out_specs,
        out_shape=out_shape,
        compiler_params=_params(("arbitrary",)),
        name="proj_prompt" if prompt else "proj_sample",
    )(*args)


def _sb_prompt_kernel(q_ref, k_ref, vt_ref, o_ref, acc_ref, zt_ref, s_ref, w_ref, *, t, nq):
    i = pl.program_id(2)
    hd = SB_HEAD_DIM
    heads = range(2)
    q = q_ref[0]
    lane = lax.broadcasted_iota(jnp.int32, (t, LANES), 1)
    qm = (jnp.where(lane < hd, q, jnp.zeros_like(q)), jnp.where(lane >= hd, q, jnp.zeros_like(q)))
    kk = lax.broadcasted_iota(jnp.int32, (t, t), 0)
    qq = lax.broadcasted_iota(jnp.int32, (t, t), 1)
    tri = (qq >= kk).astype(BF16)
    valid = kk < qq

    def scores(j):
        kblk = k_ref[0, pl.ds(pl.multiple_of(j * t, t), t), :]
        return [_dot_nt(kblk, qm[h]) for h in heads]

    def survive(zt, s, cum, shift):
        r = _dot(tri, s)
        return jnp.exp2(zt - r + (cum + shift)), cum - r[0:1, :]

    def gather(j, ws):
        vblk = vt_ref[0, 0, j]
        return [_dot(vblk[h * hd:(h + 1) * hd, :], ws[h]) for h in heads]

    zts = scores(i)
    zero = jnp.zeros((1, t), F32)
    ss = [jnp.where(valid, _softplus2(zts[h]), 0.0).astype(BF16) for h in heads]
    ws, cums = zip(*[survive(zts[h], ss[h], zero, 0.0) for h in heads])
    parts = gather(i, [jnp.where(valid, ws[h], 0.0).astype(BF16) for h in heads])
    for h in heads:
        acc_ref[h * hd:(h + 1) * hd, :] = parts[h]

    def block_of(m):
        return jnp.clip(i - 1 - m, 0, nq - 1)

    first = scores(block_of(0))
    for h in heads:
        zt_ref[0, h] = first[h]
        zt_ref[2, h] = jnp.zeros((t, t), F32)
        s_ref[1, h] = jnp.zeros((t, t), BF16)
        w_ref[0, h] = jnp.zeros((t, t), BF16)

    def body(n, cums):
        z_new, z_cur, z_old = lax.rem(n + 1, 3), lax.rem(n, 3), lax.rem(n + 2, 3)
        cur, old = lax.rem(n, 2), lax.rem(n + 1, 2)
        nxt = scores(block_of(n + 1))
        parts = gather(block_of(n - 2), [w_ref[cur, h] for h in heads])
        shift = jnp.where(jnp.logical_and(n >= 1, n <= i), 0.0, -SHIFT_OUT)
        new_ws, new_cums = zip(*[survive(zt_ref[z_old, h], s_ref[old, h], cums[h], shift)
                                 for h in heads])
        for h in heads:
            s_ref[cur, h] = _softplus2(zt_ref[z_cur, h]).astype(BF16)
        for h in heads:
            acc_ref[h * hd:(h + 1) * hd, :] += parts[h]
            w_ref[old, h] = new_ws[h].astype(BF16)
            zt_ref[z_new, h] = nxt[h]
        return tuple(new_cums)

    lax.fori_loop(0, jnp.where(i > 0, i + 2, 0), body, tuple(cums))
    o_ref[0] = acc_ref[...].T


def _sb_prompt(q, kb16, vt, *, batch):
    n = q.shape[0]
    s = n // batch
    t = SB_TILE
    nq = s // t
    hps = SB_WIDTH // LANES
    q3 = q.reshape(batch, s, SB_WIDTH)
    k3 = kb16.reshape(batch, s, SB_WIDTH)
    out = pl.pallas_call(
        functools.partial(_sb_prompt_kernel, t=t, nq=nq),
        grid=(batch, hps, nq),
        in_specs=[
            pl.BlockSpec((1, t, LANES), lambda b, hp, i: (b, i, hp)),
            pl.BlockSpec((1, s, LANES), lambda b, hp, i: (b, 0, hp)),
            pl.BlockSpec((1, 1, nq, LANES, t), lambda b, hp, i: (b, hp, 0, 0, 0)),
        ],
        out_specs=pl.BlockSpec((1, t, LANES), lambda b, hp, i: (b, i, hp)),
        out_shape=jax.ShapeDtypeStruct((batch, s, SB_WIDTH), F32),
        scratch_shapes=[pltpu.VMEM((LANES, t), F32), pltpu.VMEM((3, 2, t, t), F32),
                        pltpu.VMEM((2, 2, t, t), BF16), pltpu.VMEM((2, 2, t, t), BF16)],
        compiler_params=_params(("arbitrary", "arbitrary", "arbitrary")),
        name="sb_prompt",
    )(q3, k3, vt)
    return out.reshape(n, SB_WIDTH)


def _sb_sample_kernel(q_ref, kn_ref, vn_ref, ck_ref, cv_ref, o_ref, *, tq, past, kb):
    hd = SB_HEAD_DIM
    lane = lax.broadcasted_iota(jnp.int32, (tq, LANES), 1)
    qi = lax.broadcasted_iota(jnp.int32, (tq, tq), 0)
    ki = lax.broadcasted_iota(jnp.int32, (tq, tq), 1)
    valid = ki < qi
    tri_new = (qi >= ki).astype(BF16)
    a = lax.broadcasted_iota(jnp.int32, (kb, kb), 0)
    b = lax.broadcasted_iota(jnp.int32, (kb, kb), 1)
    tri = (a >= b).astype(BF16)

    def cumsum_right(s, m):
        s_hi, s_lo = _split_bf16(s)
        return _dot(s_hi, m) + _dot(s_lo, m)

    for hp in range(SB_WIDTH // LANES):
        sl = slice(hp * LANES, (hp + 1) * LANES)
        q2 = q_ref[:, sl]
        kn = kn_ref[:, sl].astype(BF16)
        vn = vn_ref[:, sl].astype(BF16)
        outs = []
        for h in range(2):
            keep = (lane < hd) if h == 0 else (lane >= hd)
            qm = jnp.where(keep, q2, jnp.zeros_like(q2))
            z = _dot_nt(qm, kn)
            s = jnp.where(valid, _softplus2(z), 0.0)
            r = cumsum_right(s, tri_new)
            w = jnp.where(valid, jnp.exp2(z - r), 0.0)
            acc = _dot(w.astype(BF16), vn)
            cum = -r[:, 0:1]

            def body(n, carry):
                acc, cum = carry
                start = pl.multiple_of(past - (n + 1) * kb, kb)
                kblk = ck_ref[0, 0, pl.ds(start, kb), sl].astype(BF16)
                vblk = cv_ref[0, 0, pl.ds(start, kb), sl].astype(BF16)
                z = _dot_nt(qm, kblk)
                r = cumsum_right(_softplus2(z), tri)
                w = jnp.exp2(z - r + cum)
                return acc + _dot(w.astype(BF16), vblk), cum - r[:, 0:1]

            acc, cum = lax.fori_loop(0, past // kb, body, (acc, cum))
            outs.append(acc)
        o_ref[:, sl] = jnp.where(lane < hd, outs[0], outs[1])


def _sb_sample(q, k_new, v_new, cache_k, cache_v, layer, *, batch):
    n = q.shape[0]
    tq = n // batch
    past = cache_k.shape[2]
    kb = min(MXU_DIM, past)
    assert past % kb == 0
    row = pl.BlockSpec((tq, SB_WIDTH), lambda b: (b, 0))
    cache = pl.BlockSpec((1, 1, past, SB_WIDTH), lambda b: (layer, b, 0, 0))
    return pl.pallas_call(
        functools.partial(_sb_sample_kernel, tq=tq, past=past, kb=kb),
        grid=(batch,),
        in_specs=[row, row, row, cache, cache],
        out_specs=row,
        out_shape=jax.ShapeDtypeStruct((n, SB_WIDTH), F32),
        compiler_params=_params(("arbitrary",)),
        name="sb_sample",
    )(q, k_new, v_new, cache_k, cache_v)


def _ret_log_gamma(h):
    return math.log1p(-2.0 ** (-5 - h))


def _ret_kernel(q_ref, k_ref, v_ref, s0_ref, o_ref, sout_ref, state, dec, qdec, kdec, *, c, nchunks):
    seq = pl.program_id(0)
    ci = pl.program_id(1)

    @pl.when(jnp.logical_and(seq == 0, ci == 0))
    def _():
        li = lax.broadcasted_iota(jnp.int32, (c, c), 0)
        mi = lax.broadcasted_iota(jnp.int32, (c, c), 1)
        diff = (li - mi).astype(F32)
        pos = lax.broadcasted_iota(jnp.int32, (c, RET_HEAD_DIM), 0).astype(F32)
        for h in range(RET_HEADS):
            lg = _ret_log_gamma(h)
            dec[h] = jnp.where(diff >= 0.0, jnp.exp(jnp.maximum(diff, 0.0) * lg), 0.0)
            qdec[h] = jnp.exp((pos + 1.0) * lg)
            kdec[h] = jnp.exp((c - 1.0 - pos) * lg)

    @pl.when(ci == 0)
    def _():
        state[...] = s0_ref[0]

    for h in range(RET_HEADS):
        sl = slice(h * RET_HEAD_DIM, (h + 1) * RET_HEAD_DIM)
        qh = q_ref[:, sl]
        kh = k_ref[:, sl]
        vh = v_ref[:, sl]
        st = state[h]
        scores = _dot_nt(qh, kh) * dec[h]
        inner = _dot(scores.astype(BF16), vh)
        cross = _dot(qh, st.astype(BF16)) * qdec[h]
        o_ref[:, sl] = inner + cross
        kd = (kh.astype(F32) * kdec[h]).astype(BF16)
        state[h] = math.exp(c * _ret_log_gamma(h)) * st + _dot_tn(kd, vh)

    @pl.when(ci == nchunks - 1)
    def _():
        sout_ref[0] = state[...]


def _retention(q, k, v, state0, *, batch):
    n = q.shape[0]
    t = n // batch
    c = min(RET_CHUNK, t)
    nchunks = t // c
    row = pl.BlockSpec((c, RET_WIDTH), lambda b, i: (b * nchunks + i, 0))
    st = pl.BlockSpec((1, RET_HEADS, RET_HEAD_DIM, RET_HEAD_DIM), lambda b, i: (b, 0, 0, 0))
    return pl.pallas_call(
        functools.partial(_ret_kernel, c=c, nchunks=nchunks),
        grid=(batch, nchunks),
        in_specs=[row, row, row, st],
        out_specs=[row, st],
        out_shape=[jax.ShapeDtypeStruct((n, RET_WIDTH), F32),
                   jax.ShapeDtypeStruct((batch, RET_HEADS, RET_HEAD_DIM, RET_HEAD_DIM), F32)],
        scratch_shapes=[pltpu.VMEM((RET_HEADS, RET_HEAD_DIM, RET_HEAD_DIM), F32),
                        pltpu.VMEM((RET_HEADS, c, c), F32),
                        pltpu.VMEM((RET_HEADS, c, RET_HEAD_DIM), F32),
                        pltpu.VMEM((RET_HEADS, c, RET_HEAD_DIM), F32)],
        compiler_params=_params(("arbitrary", "arbitrary")),
        name="retention",
    )(q, k, v, state0)


def _merge_ffn_kernel(x_ref, osb_ref, oret_ref, gate_ref, gt1_ref, sc2_ref, sh2_ref, gt2_ref,
                      gsb_ref, gret_ref, gffn_ref, gfin_ref, wout_ref, wfi_ref, wfo_ref,
                      *outs, final):
    xo_ref = outs[0]
    a = _rms(osb_ref[...], gsb_ref[...]).astype(BF16)
    mix = _dot(a, wout_ref[0:SB_WIDTH, :])
    gate = gate_ref[...]
    gate = gate * jax.nn.sigmoid(gate)
    for h in range(RET_HEADS):
        sl = slice(h * RET_HEAD_DIM, (h + 1) * RET_HEAD_DIM)
        r = (_rms(oret_ref[:, sl], gret_ref[:, sl]) * gate[:, sl]).astype(BF16)
        mix += _dot(r, wout_ref[SB_WIDTH + h * RET_HEAD_DIM:SB_WIDTH + (h + 1) * RET_HEAD_DIM, :])
    x1 = x_ref[...] + gt1_ref[0] * mix
    h2 = (_rms(x1, gffn_ref[...]) * (1.0 + sc2_ref[0]) + sh2_ref[0]).astype(BF16)
    ff = jnp.zeros_like(x1)
    for j in range(D_FF // FF_CHUNK):
        cols = slice(j * FF_CHUNK, (j + 1) * FF_CHUNK)
        ucols = slice(D_FF + j * FF_CHUNK, D_FF + (j + 1) * FF_CHUNK)
        g = _dot(h2, wfi_ref[:, cols])
        u = _dot(h2, wfi_ref[:, ucols])
        act = (g * jax.nn.sigmoid(g) * u).astype(BF16)
        ff += _dot(act, wfo_ref[cols, :])
    x2 = x1 + gt2_ref[0] * ff
    xo_ref[...] = x2
    if final:
        outs[1][...] = _rms(x2, gfin_ref[...])


def _merge_ffn(x, o_sb, o_ret, gate, gt1, sc2, sh2, gt2, g_sb, g_ret, g_ffn, g_fin,
               w_out_b, w_fi_b, w_fo_b, *, batch, final):
    n = x.shape[0]
    tm = min(TOKEN_TILE, n)
    tiles = n // tm
    per_seq = max(tiles // batch, 1)
    mrows = gt1.shape[1]

    def row(i):
        return (i, 0)

    def mod(i):
        return (i // per_seq if mrows == 1 else i, 0, 0)

    full = pl.BlockSpec((tm, D_MODEL), row)
    half = pl.BlockSpec((tm, SB_WIDTH), row)
    mspec = pl.BlockSpec((1, mrows, D_MODEL), mod)
    n_out = 2 if final else 1
    res = pl.pallas_call(
        functools.partial(_merge_ffn_kernel, final=final),
        grid=(tiles,),
        in_specs=[full, half, half, half, mspec, mspec, mspec, mspec,
                  _resident((1, SB_WIDTH)), _resident((1, RET_WIDTH)),
                  _resident((1, D_MODEL)), _resident((1, D_MODEL)),
                  _resident((D_MODEL, D_MODEL)), _resident((D_MODEL, 2 * D_FF)),
                  _resident((D_FF, D_MODEL))],
        out_specs=[full] * n_out,
        out_shape=[jax.ShapeDtypeStruct((n, D_MODEL), F32)] * n_out,
        compiler_params=_params(("arbitrary",)),
        name="merge_ffn",
    )(x, o_sb, o_ret, gate, gt1, sc2, sh2, gt2, g_sb, g_ret, g_ffn, g_fin, w_out_b, w_fi_b, w_fo_b)
    return res if final else (res[0], None)


def kernel(x_prompt, x_sample, cache_sb_k, cache_sb_v, state_ret, c_prompt, c_sample,
           g_norm_mix, g_norm_ffn, w_ada, b_ada, w_in, g_sb_out, g_ret_out, w_out,
           w_ff_in, w_ff_out, g_final):
    depth = w_in.shape[0]
    bp, sp, _ = x_prompt.shape
    bs, ts, _ = x_sample.shape
    past = cache_sb_k.shape[2]
    n_p = bp * sp
    n_s = bs * ts

    c_all = jnp.concatenate([c_prompt, c_sample], axis=0)
    pad = (-c_all.shape[0]) % 16
    mods = _modulation(jnp.pad(c_all, ((0, pad), (0, 0))), w_ada, b_ada)
    mods = mods.reshape(depth, -1, N_MOD, D_MODEL)
    mods_p = mods[:, :bp]
    mods_s = jnp.repeat(mods[:, bp:bp + bs], ts, axis=1)

    cos_p, sin_p = _rope_tables(sp, 0)
    cos_s, sin_s = _rope_tables(ts, past)
    cos_s = jnp.tile(cos_s, (bs, 1))
    sin_s = jnp.tile(sin_s, (bs, 1))

    cache_k = cache_sb_k.reshape(depth, bs, past, SB_WIDTH)
    cache_v = cache_sb_v.reshape(depth, bs, past, SB_WIDTH)
    zero_state = jnp.zeros((bp, RET_HEADS, RET_HEAD_DIM, RET_HEAD_DIM), F32)

    xp = x_prompt.reshape(n_p, D_MODEL)
    xs = x_sample.reshape(n_s, D_MODEL)
    kp, vp, rp, ks, vs, rs = [], [], [], [], [], []
    yp = ys = None
    for l in range(depth):
        final = l == depth - 1
        w_in_b = w_in[l].astype(BF16)
        wvt_b = w_in[l][:, 2 * SB_WIDTH:3 * SB_WIDTH].T.astype(BF16)
        w_out_b = w_out[l].astype(BF16)
        w_fi_b = w_ff_in[l].astype(BF16)
        w_fo_b = w_ff_out[l].astype(BF16)
        g_mix = g_norm_mix[l].reshape(1, D_MODEL)
        g_ffn = g_norm_ffn[l].reshape(1, D_MODEL)
        g_sb = g_sb_out[l].reshape(1, SB_WIDTH)
        g_ret = g_ret_out[l].reshape(1, RET_WIDTH)
        g_fin = g_final.reshape(1, D_MODEL)

        m = [mods_p[l, :, j].reshape(bp, 1, D_MODEL) for j in range(N_MOD)]
        q, k, k16, v, vt, qr, kr, vr, gate = _proj(
            xp, m[1], m[0], g_mix, cos_p, sin_p, w_in_b, wvt_b, batch=bp, prompt=True)
        o_sb = _sb_prompt(q, k16, vt, batch=bp)
        o_ret, st = _retention(qr, kr, vr, zero_state, batch=bp)
        xp, yp = _merge_ffn(xp, o_sb, o_ret, gate, m[2], m[4], m[3], m[5], g_sb, g_ret, g_ffn,
                            g_fin, w_out_b, w_fi_b, w_fo_b, batch=bp, final=final)
        kp.append(k)
        vp.append(v)
        rp.append(st)

        m = [mods_s[l, :, j].reshape(1, n_s, D_MODEL) for j in range(N_MOD)]
        q, k, v, qr, kr, vr, gate = _proj(
            xs, m[1], m[0], g_mix, cos_s, sin_s, w_in_b, None, batch=bs, prompt=False)
        o_sb = _sb_sample(q, k, v, cache_k, cache_v, l, batch=bs)
        o_ret, st = _retention(qr, kr, vr, state_ret[l], batch=bs)
        xs, ys = _merge_ffn(xs, o_sb, o_ret, gate, m[2], m[4], m[3], m[5], g_sb, g_ret, g_ffn,
                            g_fin, w_out_b, w_fi_b, w_fo_b, batch=bs, final=final)
        ks.append(k)
        vs.append(v)
        rs.append(st)

    def heads(a, b, t):
        return jnp.stack(a).reshape(depth, b, t, SB_HEADS, SB_HEAD_DIM)

    return (yp.reshape(bp, sp, D_MODEL), ys.reshape(bs, ts, D_MODEL),
            heads(kp, bp, sp), heads(vp, bp, sp), jnp.stack(rp),
            heads(ks, bs, ts), heads(vs, bs, ts), jnp.stack(rs))
```

```python
import functools
import math

import jax
import jax.numpy as jnp
from jax import lax
from jax.experimental import pallas as pl
from jax.experimental.pallas import tpu as pltpu

F32 = jnp.float32
BF16 = jnp.bfloat16

D_MODEL = 1024
SB_WIDTH = 512
RET_WIDTH = 512
SB_HEAD_DIM = 64
SB_HEADS = SB_WIDTH // SB_HEAD_DIM
RET_HEAD_DIM = 128
RET_HEADS = RET_WIDTH // RET_HEAD_DIM
IN_WIDTH = 3 * SB_WIDTH + 4 * RET_WIDTH
D_FF = 2816
N_MOD = 6
ROPE_BASE = 10000.0
EPS = 1e-6
LOG2E = 1.4426950408889634
LN2 = 0.6931471805599453

LANES = 128
MXU_DIM = 256
VMEM_LIMIT_BYTES = 56 * 1024 * 1024

TOKEN_TILE = 512
SB_TILE = MXU_DIM
SB_UNROLL = 3
RET_CHUNK = MXU_DIM
FF_CHUNK = MXU_DIM
SOFTPLUS_CLAMP = 30.0
SHIFT_OUT = 1e30


def _dot(a, b):
    return jnp.dot(a, b, preferred_element_type=F32)


def _dot_nt(a, b):
    return lax.dot_general(a, b, (((1,), (1,)), ((), ())), preferred_element_type=F32)


def _dot_tn(a, b):
    return lax.dot_general(a, b, (((0,), (0,)), ((), ())), preferred_element_type=F32)


def _split_bf16(x):
    hi = x.astype(BF16)
    lo = (x - hi.astype(F32)).astype(BF16)
    return hi, lo


def _rms(x, g):
    return x * lax.rsqrt(jnp.mean(x * x, axis=-1, keepdims=True) + EPS) * g


def _softplus2(z2):
    e = jnp.exp2(jnp.minimum(z2, SOFTPLUS_CLAMP))
    return jnp.maximum(z2, jnp.log(1.0 + e) * LOG2E)


def _params(semantics, flags=None):
    return pltpu.CompilerParams(dimension_semantics=semantics, vmem_limit_bytes=VMEM_LIMIT_BYTES,
                                flags=flags)


def _resident(shape):
    zeros = (0,) * len(shape)
    return pl.BlockSpec(shape, lambda *_: zeros, pipeline_mode=pl.Buffered(1))


def _mod_kernel(c_ref, w_ref, b_ref, o_ref):
    c = c_ref[...]
    a = c * jax.nn.sigmoid(c)
    a_hi, a_lo = _split_bf16(a)
    w_hi, w_lo = _split_bf16(w_ref[0])
    o_ref[0] = _dot(a_hi, w_hi) + _dot(a_lo, w_hi) + _dot(a_hi, w_lo) + b_ref[0]


def _modulation(c, w_ada, b_ada):
    depth = w_ada.shape[0]
    rows = c.shape[0]
    return pl.pallas_call(
        _mod_kernel,
        grid=(depth, N_MOD),
        in_specs=[
            pl.BlockSpec((rows, D_MODEL), lambda l, j: (0, 0)),
            pl.BlockSpec((1, D_MODEL, D_MODEL), lambda l, j: (l, 0, j)),
            pl.BlockSpec((1, 1, D_MODEL), lambda l, j: (l, 0, j)),
        ],
        out_specs=pl.BlockSpec((1, rows, D_MODEL), lambda l, j: (l, 0, j)),
        out_shape=jax.ShapeDtypeStruct((depth, rows, N_MOD * D_MODEL), F32),
        compiler_params=_params(("arbitrary", "arbitrary")),
        name="modulation",
    )(c, w_ada, b_ada.reshape(depth, 1, N_MOD * D_MODEL))


def _rope_kernel(inv_ref, cos_ref, sin_ref, *, rows, pos0):
    row = lax.broadcasted_iota(jnp.int32, (rows, LANES), 0) + (pl.program_id(0) * rows + pos0)
    lane = lax.broadcasted_iota(jnp.int32, (rows, LANES), 1)
    ang = row.astype(F32) * inv_ref[...]
    sin = jnp.sin(ang)
    cos_ref[...] = jnp.cos(ang)
    sin_ref[...] = jnp.where(lane < RET_HEAD_DIM // 2, -sin, sin)


def _rope_tables(n, pos0):
    half = RET_HEAD_DIM // 2
    inv = ROPE_BASE ** (-jnp.arange(half, dtype=F32) / half)
    inv2 = jnp.concatenate([inv, inv]).reshape(1, LANES)
    rows = min(n, 2048)
    assert n % rows == 0
    return pl.pallas_call(
        functools.partial(_rope_kernel, rows=rows, pos0=pos0),
        grid=(n // rows,),
        in_specs=[pl.BlockSpec((1, LANES), lambda i: (0, 0))],
        out_specs=[pl.BlockSpec((rows, LANES), lambda i: (i, 0))] * 2,
        out_shape=[jax.ShapeDtypeStruct((n, LANES), F32)] * 2,
        compiler_params=_params(("arbitrary",)),
        name="rope_tables",
    )(inv2)


def _proj_kernel(x_ref, sc_ref, sh_ref, g_ref, cos_ref, sin_ref, w_ref, *rest, tm, kb, prompt):
    if prompt:
        wvt_ref, q_ref, k_ref, kb_ref, v_ref, vt_ref, qr_ref, kr_ref, vr_ref, gate_ref = rest
    else:
        q_ref, k_ref, v_ref, qr_ref, kr_ref, vr_ref, gate_ref = rest
    h = _rms(x_ref[...], g_ref[...]) * (1.0 + sc_ref[0]) + sh_ref[0]
    hb = h.astype(BF16)

    def seg(a, b):
        return _dot(hb, w_ref[:, a:b])

    q_ref[...] = (seg(0, SB_WIDTH) * (SB_HEAD_DIM ** -0.5 * LOG2E)).astype(BF16)
    k = seg(SB_WIDTH, 2 * SB_WIDTH)
    k_ref[...] = k
    v_ref[...] = seg(2 * SB_WIDTH, 3 * SB_WIDTH)
    if prompt:
        kb_ref[...] = k.astype(BF16)
        vt = _dot_nt(wvt_ref[...], hb).astype(BF16)
        for hp in range(SB_WIDTH // LANES):
            for j in range(tm // kb):
                vt_ref[0, hp, j] = vt[hp * LANES:(hp + 1) * LANES, j * kb:(j + 1) * kb]
    cos = cos_ref[...]
    sin = sin_ref[...]
    base = 3 * SB_WIDTH
    qr = seg(base, base + RET_WIDTH)
    kr = seg(base + RET_WIDTH, base + 2 * RET_WIDTH)
    for hh in range(RET_HEADS):
        sl = slice(hh * RET_HEAD_DIM, (hh + 1) * RET_HEAD_DIM)
        qh = qr[:, sl]
        kh = kr[:, sl]
        qr_ref[:, sl] = (qh * cos + pltpu.roll(qh, RET_HEAD_DIM // 2, 1) * sin).astype(BF16)
        kr_ref[:, sl] = ((kh * cos + pltpu.roll(kh, RET_HEAD_DIM // 2, 1) * sin)
                         * RET_HEAD_DIM ** -0.5).astype(BF16)
    vr_ref[...] = seg(base + 2 * RET_WIDTH, base + 3 * RET_WIDTH).astype(BF16)
    gate_ref[...] = seg(base + 3 * RET_WIDTH, base + 4 * RET_WIDTH)


def _proj(x, sc, sh, g, cos, sin, w_in_b, wvt_b, *, batch, prompt):
    n = x.shape[0]
    tm = min(TOKEN_TILE, n)
    tiles = n // tm
    per_seq = max(tiles // batch, 1)
    pos_tiles = cos.shape[0] // tm
    kb = SB_TILE
    mrows = sc.shape[1]

    def row(i):
        return (i, 0)

    def mod(i):
        return (i // per_seq if mrows == 1 else i, 0, 0)

    def pos(i):
        return (i % pos_tiles, 0)

    in_specs = [
        pl.BlockSpec((tm, D_MODEL), row),
        pl.BlockSpec((1, mrows, D_MODEL), mod),
        pl.BlockSpec((1, mrows, D_MODEL), mod),
        _resident((1, D_MODEL)),
        pl.BlockSpec((tm, LANES), pos),
        pl.BlockSpec((tm, LANES), pos),
        _resident((D_MODEL, IN_WIDTH)),
    ]
    args = [x, sc, sh, g, cos, sin, w_in_b]
    wide_b = jax.ShapeDtypeStruct((n, SB_WIDTH), BF16)
    wide_f = jax.ShapeDtypeStruct((n, SB_WIDTH), F32)
    wide = pl.BlockSpec((tm, SB_WIDTH), row)
    if prompt:
        in_specs.append(_resident((SB_WIDTH, D_MODEL)))
        args.append(wvt_b)
        nkb = n // batch // kb
        hps = SB_WIDTH // LANES
        out_shape = [wide_b, wide_f, wide_b, wide_f,
                     jax.ShapeDtypeStruct((batch, hps, nkb, LANES, kb), BF16),
                     wide_b, wide_b, wide_b, wide_f]
        out_specs = [wide, wide, wide, wide,
                     pl.BlockSpec((1, hps, tm // kb, LANES, kb),
                                  lambda i: (i // per_seq, 0, i % per_seq, 0, 0)),
                     wide, wide, wide, wide]
    else:
        out_shape = [wide_b, wide_f, wide_f, wide_b, wide_b, wide_b, wide_f]
        out_specs = [wide] * 7
    return pl.pallas_call(
        functools.partial(_proj_kernel, tm=tm, kb=kb, prompt=prompt),
        grid=(tiles,),
        in_specs=in_specs,
        out_specs=out_specs,
        out_shape=out_shape,
        compiler_params=_params(("arbitrary",)),
        name="proj_prompt" if prompt else "proj_sample",
    )(*args)


def _sb_prompt_kernel(q_ref, k_ref, vt_ref, o_ref, acc_ref, zt_ref, s_ref, w_ref, *, t, nq):
    i = pl.program_id(2)
    hd = SB_HEAD_DIM
    heads = range(2)
    q = q_ref[0]
    lane = lax.broadcasted_iota(jnp.int32, (t, LANES), 1)
    qm = (jnp.where(lane < hd, q, jnp.zeros_like(q)), jnp.where(lane >= hd, q, jnp.zeros_like(q)))
    kk = lax.broadcasted_iota(jnp.int32, (t, t), 0)
    qq = lax.broadcasted_iota(jnp.int32, (t, t), 1)
    tri = (qq >= kk).astype(BF16)
    valid = kk < qq

    def scores(j):
        kblk = k_ref[0, pl.ds(pl.multiple_of(j * t, t), t), :]
        return [_dot_nt(kblk, qm[h]) for h in heads]

    def survive(zt, s, cum, shift):
        r = _dot(tri, s)
        return jnp.exp2(zt - r + (cum + shift)), cum - r[0:1, :]

    def gather(j, ws):
        vblk = vt_ref[0, 0, j]
        return [_dot(vblk[h * hd:(h + 1) * hd, :], ws[h]) for h in heads]

    zts = scores(i)
    zero = jnp.zeros((1, t), F32)
    ss = [jnp.where(valid, _softplus2(zts[h]), 0.0).astype(BF16) for h in heads]
    ws, cums = zip(*[survive(zts[h], ss[h], zero, 0.0) for h in heads])
    parts = gather(i, [jnp.where(valid, ws[h], 0.0).astype(BF16) for h in heads])
    for h in heads:
        acc_ref[h * hd:(h + 1) * hd, :] = parts[h]

    def block_of(m):
        return jnp.clip(i - 1 - m, 0, nq - 1)

    first = scores(block_of(0))
    for h in heads:
        zt_ref[0, h] = first[h]
        zt_ref[2, h] = jnp.zeros((t, t), F32)
        s_ref[2, h] = jnp.zeros((t, t), BF16)
        w_ref[1, h] = jnp.zeros((t, t), BF16)

    def trip(n, r, cums):
        nxt_slot, cur, old, done = (r + 1) % 3, r, (r + 2) % 3, (r + 1) % 3
        shift = jnp.where(jnp.logical_and(n >= 1, n <= i), 0.0, -SHIFT_OUT)
        new_ws, new_cums = zip(*[survive(zt_ref[old, h], s_ref[old, h], cums[h], shift)
                                 for h in heads])
        nxt = scores(block_of(n + 1))
        parts = gather(block_of(n - 2), [w_ref[done, h] for h in heads])
        for h in heads:
            s_ref[cur, h] = _softplus2(zt_ref[cur, h]).astype(BF16)
        for h in heads:
            acc_ref[h * hd:(h + 1) * hd, :] += parts[h]
            w_ref[old, h] = new_ws[h].astype(BF16)
            zt_ref[nxt_slot, h] = nxt[h]
        return tuple(new_cums)

    def body(p, cums):
        for r in range(SB_UNROLL):
            cums = trip(SB_UNROLL * p + r, r % 3, cums)
        return cums

    lax.fori_loop(0, jnp.where(i > 0, (i + 1 + SB_UNROLL) // SB_UNROLL, 0), body, tuple(cums))
    o_ref[0] = acc_ref[...].T


def _sb_prompt(q, kb16, vt, *, batch):
    n = q.shape[0]
    s = n // batch
    t = SB_TILE
    nq = s // t
    hps = SB_WIDTH // LANES
    q3 = q.reshape(batch, s, SB_WIDTH)
    k3 = kb16.reshape(batch, s, SB_WIDTH)
    out = pl.pallas_call(
        functools.partial(_sb_prompt_kernel, t=t, nq=nq),
        grid=(batch, hps, nq),
        in_specs=[
            pl.BlockSpec((1, t, LANES), lambda b, hp, i: (b, i, hp)),
            pl.BlockSpec((1, s, LANES), lambda b, hp, i: (b, 0, hp)),
            pl.BlockSpec((1, 1, nq, LANES, t), lambda b, hp, i: (b, hp, 0, 0, 0)),
        ],
        out_specs=pl.BlockSpec((1, t, LANES), lambda b, hp, i: (b, i, hp)),
        out_shape=jax.ShapeDtypeStruct((batch, s, SB_WIDTH), F32),
        scratch_shapes=[pltpu.VMEM((LANES, t), F32), pltpu.VMEM((3, 2, t, t), F32),
                        pltpu.VMEM((3, 2, t, t), BF16), pltpu.VMEM((3, 2, t, t), BF16)],
        compiler_params=_params(("arbitrary", "arbitrary", "arbitrary")),
        name="sb_prompt",
    )(q3, k3, vt)
    return out.reshape(n, SB_WIDTH)


def _sb_sample_kernel(q_ref, kn_ref, vn_ref, ck_ref, cv_ref, o_ref, *, tq, past, kb):
    hd = SB_HEAD_DIM
    lane = lax.broadcasted_iota(jnp.int32, (tq, LANES), 1)
    qi = lax.broadcasted_iota(jnp.int32, (tq, tq), 0)
    ki = lax.broadcasted_iota(jnp.int32, (tq, tq), 1)
    valid = ki < qi
    tri_new = (qi >= ki).astype(BF16)
    a = lax.broadcasted_iota(jnp.int32, (kb, kb), 0)
    b = lax.broadcasted_iota(jnp.int32, (kb, kb), 1)
    tri = (a >= b).astype(BF16)

    def cumsum_right(s, m):
        s_hi, s_lo = _split_bf16(s)
        return _dot(s_hi, m) + _dot(s_lo, m)

    pairs = range(SB_WIDTH // LANES)
    chains = range(2 * len(pairs))
    slices = [slice(hp * LANES, (hp + 1) * LANES) for hp in pairs]
    qms = []
    for hp in pairs:
        q2 = q_ref[:, slices[hp]]
        qms += [jnp.where(lane < hd, q2, jnp.zeros_like(q2)),
                jnp.where(lane >= hd, q2, jnp.zeros_like(q2))]

    def sweep(kblks, vblks, m, cums, mask=None):
        zs = [_dot_nt(qms[c], kblks[c // 2]) for c in chains]
        ss = [_softplus2(z) for z in zs]
        if mask is not None:
            ss = [jnp.where(mask, s, 0.0) for s in ss]
        rs = [cumsum_right(s, m) for s in ss]
        ws = [jnp.exp2(zs[c] - rs[c] + cums[c]) for c in chains]
        if mask is not None:
            ws = [jnp.where(mask, w, 0.0) for w in ws]
        outs = [_dot(ws[c].astype(BF16), vblks[c // 2]) for c in chains]
        return outs, [cums[c] - rs[c][:, 0:1] for c in chains]

    zero = jnp.zeros((tq, 1), F32)
    accs, cums = sweep([kn_ref[:, sl].astype(BF16) for sl in slices],
                       [vn_ref[:, sl].astype(BF16) for sl in slices],
                       tri_new, [zero] * len(chains), valid)

    def body(n, carry):
        accs, cums = carry
        start = pl.multiple_of(past - (n + 1) * kb, kb)
        outs, cums = sweep([ck_ref[0, 0, pl.ds(start, kb), sl].astype(BF16) for sl in slices],
                           [cv_ref[0, 0, pl.ds(start, kb), sl].astype(BF16) for sl in slices],
                           tri, cums)
        return tuple(accs[c] + outs[c] for c in chains), tuple(cums)

    accs, _ = lax.fori_loop(0, past // kb, body, (tuple(accs), tuple(cums)))
    for hp in pairs:
        o_ref[:, slices[hp]] = jnp.where(lane < hd, accs[2 * hp], accs[2 * hp + 1])


def _sb_sample(q, k_new, v_new, cache_k, cache_v, layer, *, batch):
    n = q.shape[0]
    tq = n // batch
    past = cache_k.shape[2]
    kb = min(MXU_DIM, past)
    assert past % kb == 0
    row = pl.BlockSpec((tq, SB_WIDTH), lambda b: (b, 0))
    cache = pl.BlockSpec((1, 1, past, SB_WIDTH), lambda b: (layer, b, 0, 0))
    return pl.pallas_call(
        functools.partial(_sb_sample_kernel, tq=tq, past=past, kb=kb),
        grid=(batch,),
        in_specs=[row, row, row, cache, cache],
        out_specs=row,
        out_shape=jax.ShapeDtypeStruct((n, SB_WIDTH), F32),
        compiler_params=_params(("arbitrary",)),
        name="sb_sample",
    )(q, k_new, v_new, cache_k, cache_v)


def _ret_log_gamma(h):
    return math.log1p(-2.0 ** (-5 - h))


def _ret_kernel(q_ref, k_ref, v_ref, s0_ref, o_ref, sout_ref, state, dec, qdec, kdec, *, c, nchunks):
    seq = pl.program_id(0)
    ci = pl.program_id(1)

    @pl.when(jnp.logical_and(seq == 0, ci == 0))
    def _():
        li = lax.broadcasted_iota(jnp.int32, (c, c), 0)
        mi = lax.broadcasted_iota(jnp.int32, (c, c), 1)
        diff = (li - mi).astype(F32)
        pos = lax.broadcasted_iota(jnp.int32, (c, RET_HEAD_DIM), 0).astype(F32)
        for h in range(RET_HEADS):
            lg = _ret_log_gamma(h)
            dec[h] = jnp.where(diff >= 0.0, jnp.exp(jnp.maximum(diff, 0.0) * lg), 0.0)
            qdec[h] = jnp.exp((pos + 1.0) * lg)
            kdec[h] = jnp.exp((c - 1.0 - pos) * lg)

    @pl.when(ci == 0)
    def _():
        state[...] = s0_ref[0]

    for h in range(RET_HEADS):
        sl = slice(h * RET_HEAD_DIM, (h + 1) * RET_HEAD_DIM)
        qh = q_ref[:, sl]
        kh = k_ref[:, sl]
        vh = v_ref[:, sl]
        st = state[h]
        scores = _dot_nt(qh, kh) * dec[h]
        inner = _dot(scores.astype(BF16), vh)
        cross = _dot(qh, st.astype(BF16)) * qdec[h]
        o_ref[:, sl] = inner + cross
        kd = (kh.astype(F32) * kdec[h]).astype(BF16)
        state[h] = math.exp(c * _ret_log_gamma(h)) * st + _dot_tn(kd, vh)

    @pl.when(ci == nchunks - 1)
    def _():
        sout_ref[0] = state[...]


def _retention(q, k, v, state0, *, batch):
    n = q.shape[0]
    t = n // batch
    c = min(RET_CHUNK, t)
    nchunks = t // c
    row = pl.BlockSpec((c, RET_WIDTH), lambda b, i: (b * nchunks + i, 0))
    st = pl.BlockSpec((1, RET_HEADS, RET_HEAD_DIM, RET_HEAD_DIM), lambda b, i: (b, 0, 0, 0))
    return pl.pallas_call(
        functools.partial(_ret_kernel, c=c, nchunks=nchunks),
        grid=(batch, nchunks),
        in_specs=[row, row, row, st],
        out_specs=[row, st],
        out_shape=[jax.ShapeDtypeStruct((n, RET_WIDTH), F32),
                   jax.ShapeDtypeStruct((batch, RET_HEADS, RET_HEAD_DIM, RET_HEAD_DIM), F32)],
        scratch_shapes=[pltpu.VMEM((RET_HEADS, RET_HEAD_DIM, RET_HEAD_DIM), F32),
                        pltpu.VMEM((RET_HEADS, c, c), F32),
                        pltpu.VMEM((RET_HEADS, c, RET_HEAD_DIM), F32),
                        pltpu.VMEM((RET_HEADS, c, RET_HEAD_DIM), F32)],
        compiler_params=_params(("arbitrary", "arbitrary")),
        name="retention",
    )(q, k, v, state0)


def _merge_ffn_kernel(x_ref, osb_ref, oret_ref, gate_ref, gt1_ref, sc2_ref, sh2_ref, gt2_ref,
                      gsb_ref, gret_ref, gffn_ref, gfin_ref, wout_ref, wfi_ref, wfo_ref,
                      *outs, final):
    xo_ref = outs[0]
    a = _rms(osb_ref[...], gsb_ref[...]).astype(BF16)
    mix = _dot(a, wout_ref[0:SB_WIDTH, :])
    gate = gate_ref[...]
    gate = gate * jax.nn.sigmoid(gate)
    for h in range(RET_HEADS):
        sl = slice(h * RET_HEAD_DIM, (h + 1) * RET_HEAD_DIM)
        r = (_rms(oret_ref[:, sl], gret_ref[:, sl]) * gate[:, sl]).astype(BF16)
        mix += _dot(r, wout_ref[SB_WIDTH + h * RET_HEAD_DIM:SB_WIDTH + (h + 1) * RET_HEAD_DIM, :])
    x1 = x_ref[...] + gt1_ref[0] * mix
    h2 = (_rms(x1, gffn_ref[...]) * (1.0 + sc2_ref[0]) + sh2_ref[0]).astype(BF16)
    ff = jnp.zeros_like(x1)
    for j in range(D_FF // FF_CHUNK):
        cols = slice(j * FF_CHUNK, (j + 1) * FF_CHUNK)
        ucols = slice(D_FF + j * FF_CHUNK, D_FF + (j + 1) * FF_CHUNK)
        g = _dot(h2, wfi_ref[:, cols])
        u = _dot(h2, wfi_ref[:, ucols])
        act = (g * jax.nn.sigmoid(g) * u).astype(BF16)
        ff += _dot(act, wfo_ref[cols, :])
    x2 = x1 + gt2_ref[0] * ff
    xo_ref[...] = x2
    if final:
        outs[1][...] = _rms(x2, gfin_ref[...])


def _merge_ffn(x, o_sb, o_ret, gate, gt1, sc2, sh2, gt2, g_sb, g_ret, g_ffn, g_fin,
               w_out_b, w_fi_b, w_fo_b, *, batch, final):
    n = x.shape[0]
    tm = min(TOKEN_TILE, n)
    tiles = n // tm
    per_seq = max(tiles // batch, 1)
    mrows = gt1.shape[1]

    def row(i):
        return (i, 0)

    def mod(i):
        return (i // per_seq if mrows == 1 else i, 0, 0)

    full = pl.BlockSpec((tm, D_MODEL), row)
    half = pl.BlockSpec((tm, SB_WIDTH), row)
    mspec = pl.BlockSpec((1, mrows, D_MODEL), mod)
    n_out = 2 if final else 1
    res = pl.pallas_call(
        functools.partial(_merge_ffn_kernel, final=final),
        grid=(tiles,),
        in_specs=[full, half, half, half, mspec, mspec, mspec, mspec,
                  _resident((1, SB_WIDTH)), _resident((1, RET_WIDTH)),
                  _resident((1, D_MODEL)), _resident((1, D_MODEL)),
                  _resident((D_MODEL, D_MODEL)), _resident((D_MODEL, 2 * D_FF)),
                  _resident((D_FF, D_MODEL))],
        out_specs=[full] * n_out,
        out_shape=[jax.ShapeDtypeStruct((n, D_MODEL), F32)] * n_out,
        compiler_params=_params(("arbitrary",)),
        name="merge_ffn",
    )(x, o_sb, o_ret, gate, gt1, sc2, sh2, gt2, g_sb, g_ret, g_ffn, g_fin, w_out_b, w_fi_b, w_fo_b)
    return res if final else (res[0], None)


def kernel(x_prompt, x_sample, cache_sb_k, cache_sb_v, state_ret, c_prompt, c_sample,
           g_norm_mix, g_norm_ffn, w_ada, b_ada, w_in, g_sb_out, g_ret_out, w_out,
           w_ff_in, w_ff_out, g_final):
    depth = w_in.shape[0]
    bp, sp, _ = x_prompt.shape
    bs, ts, _ = x_sample.shape
    past = cache_sb_k.shape[2]
    n_p = bp * sp
    n_s = bs * ts

    c_all = jnp.concatenate([c_prompt, c_sample], axis=0)
    pad = (-c_all.shape[0]) % 16
    mods = _modulation(jnp.pad(c_all, ((0, pad), (0, 0))), w_ada, b_ada)
    mods = mods.reshape(depth, -1, N_MOD, D_MODEL)
    mods_p = mods[:, :bp]
    mods_s = jnp.repeat(mods[:, bp:bp + bs], ts, axis=1)

    cos_p, sin_p = _rope_tables(sp, 0)
    cos_s, sin_s = _rope_tables(ts, past)
    cos_s = jnp.tile(cos_s, (bs, 1))
    sin_s = jnp.tile(sin_s, (bs, 1))

    cache_k = cache_sb_k.reshape(depth, bs, past, SB_WIDTH)
    cache_v = cache_sb_v.reshape(depth, bs, past, SB_WIDTH)
    zero_state = jnp.zeros((bp, RET_HEADS, RET_HEAD_DIM, RET_HEAD_DIM), F32)

    xp = x_prompt.reshape(n_p, D_MODEL)
    xs = x_sample.reshape(n_s, D_MODEL)
    kp, vp, rp, ks, vs, rs = [], [], [], [], [], []
    yp = ys = None
    for l in range(depth):
        final = l == depth - 1
        w_in_b = w_in[l].astype(BF16)
        wvt_b = w_in[l][:, 2 * SB_WIDTH:3 * SB_WIDTH].T.astype(BF16)
        w_out_b = w_out[l].astype(BF16)
        w_fi_b = w_ff_in[l].astype(BF16)
        w_fo_b = w_ff_out[l].astype(BF16)
        g_mix = g_norm_mix[l].reshape(1, D_MODEL)
        g_ffn = g_norm_ffn[l].reshape(1, D_MODEL)
        g_sb = g_sb_out[l].reshape(1, SB_WIDTH)
        g_ret = g_ret_out[l].reshape(1, RET_WIDTH)
        g_fin = g_final.reshape(1, D_MODEL)

        m = [mods_p[l, :, j].reshape(bp, 1, D_MODEL) for j in range(N_MOD)]
        q, k, k16, v, vt, qr, kr, vr, gate = _proj(
            xp, m[1], m[0], g_mix, cos_p, sin_p, w_in_b, wvt_b, batch=bp, prompt=True)
        o_sb = _sb_prompt(q, k16, vt, batch=bp)
        o_ret, st = _retention(qr, kr, vr, zero_state, batch=bp)
        xp, yp = _merge_ffn(xp, o_sb, o_ret, gate, m[2], m[4], m[3], m[5], g_sb, g_ret, g_ffn,
                            g_fin, w_out_b, w_fi_b, w_fo_b, batch=bp, final=final)
        kp.append(k)
        vp.append(v)
        rp.append(st)

        m = [mods_s[l, :, j].reshape(1, n_s, D_MODEL) for j in range(N_MOD)]
        q, k, v, qr, kr, vr, gate = _proj(
            xs, m[1], m[0], g_mix, cos_s, sin_s, w_in_b, None, batch=bs, prompt=False)
        o_sb = _sb_sample(q, k, v, cache_k, cache_v, l, batch=bs)
        o_ret, st = _retention(qr, kr, vr, state_ret[l], batch=bs)
        xs, ys = _merge_ffn(xs, o_sb, o_ret, gate, m[2], m[4], m[3], m[5], g_sb, g_ret, g_ffn,
                            g_fin, w_out_b, w_fi_b, w_fo_b, batch=bs, final=final)
        ks.append(k)
        vs.append(v)
        rs.append(st)

    def heads(a, b, t):
        return jnp.stack(a).reshape(depth, b, t, SB_HEADS, SB_HEAD_DIM)

    return (yp.reshape(bp, sp, D_MODEL), ys.reshape(bs, ts, D_MODEL),
            heads(kp, bp, sp), heads(vp, bp, sp), jnp.stack(rp),
            heads(ks, bs, ts), heads(vs, bs, ts), jnp.stack(rs))
```

```python
import functools
import math

import jax
import jax.numpy as jnp
from jax import lax
from jax.experimental import pallas as pl
from jax.experimental.pallas import tpu as pltpu

F32 = jnp.float32
BF16 = jnp.bfloat16

D_MODEL = 1024
SB_WIDTH = 512
RET_WIDTH = 512
SB_HEAD_DIM = 64
SB_HEADS = SB_WIDTH // SB_HEAD_DIM
RET_HEAD_DIM = 128
RET_HEADS = RET_WIDTH // RET_HEAD_DIM
IN_WIDTH = 3 * SB_WIDTH + 4 * RET_WIDTH
D_FF = 2816
N_MOD = 6
ROPE_BASE = 10000.0
EPS = 1e-6
LOG2E = 1.4426950408889634
LN2 = 0.6931471805599453

LANES = 128
MXU_DIM = 256
VMEM_LIMIT_BYTES = 56 * 1024 * 1024

TOKEN_TILE = 512
SB_TILE = MXU_DIM
SB_UNROLL = 3
RET_CHUNK = MXU_DIM
FF_CHUNK = MXU_DIM
SOFTPLUS_CLAMP = 30.0
SHIFT_OUT = 1e30
UNDERFLOW_LOG2 = 160.0


def _dot(a, b):
    return jnp.dot(a, b, preferred_element_type=F32)


def _dot_nt(a, b):
    return lax.dot_general(a, b, (((1,), (1,)), ((), ())), preferred_element_type=F32)


def _dot_tn(a, b):
    return lax.dot_general(a, b, (((0,), (0,)), ((), ())), preferred_element_type=F32)


def _split_bf16(x):
    hi = x.astype(BF16)
    lo = (x - hi.astype(F32)).astype(BF16)
    return hi, lo


def _rms(x, g):
    return x * lax.rsqrt(jnp.mean(x * x, axis=-1, keepdims=True) + EPS) * g


def _softplus2(z2):
    e = jnp.exp2(jnp.minimum(z2, SOFTPLUS_CLAMP))
    return jnp.maximum(z2, jnp.log(1.0 + e) * LOG2E)


def _params(semantics, flags=None):
    return pltpu.CompilerParams(dimension_semantics=semantics, vmem_limit_bytes=VMEM_LIMIT_BYTES,
                                flags=flags)


def _resident(shape):
    zeros = (0,) * len(shape)
    return pl.BlockSpec(shape, lambda *_: zeros, pipeline_mode=pl.Buffered(1))


def _mod_kernel(c_ref, w_ref, b_ref, o_ref):
    c = c_ref[...]
    a = c * jax.nn.sigmoid(c)
    a_hi, a_lo = _split_bf16(a)
    w_hi, w_lo = _split_bf16(w_ref[0])
    o_ref[0] = _dot(a_hi, w_hi) + _dot(a_lo, w_hi) + _dot(a_hi, w_lo) + b_ref[0]


def _modulation(c, w_ada, b_ada):
    depth = w_ada.shape[0]
    rows = c.shape[0]
    return pl.pallas_call(
        _mod_kernel,
        grid=(depth, N_MOD),
        in_specs=[
            pl.BlockSpec((rows, D_MODEL), lambda l, j: (0, 0)),
            pl.BlockSpec((1, D_MODEL, D_MODEL), lambda l, j: (l, 0, j)),
            pl.BlockSpec((1, 1, D_MODEL), lambda l, j: (l, 0, j)),
        ],
        out_specs=pl.BlockSpec((1, rows, D_MODEL), lambda l, j: (l, 0, j)),
        out_shape=jax.ShapeDtypeStruct((depth, rows, N_MOD * D_MODEL), F32),
        compiler_params=_params(("arbitrary", "arbitrary")),
        name="modulation",
    )(c, w_ada, b_ada.reshape(depth, 1, N_MOD * D_MODEL))


def _rope_kernel(inv_ref, cos_ref, sin_ref, *, rows, pos0):
    row = lax.broadcasted_iota(jnp.int32, (rows, LANES), 0) + (pl.program_id(0) * rows + pos0)
    lane = lax.broadcasted_iota(jnp.int32, (rows, LANES), 1)
    ang = row.astype(F32) * inv_ref[...]
    sin = jnp.sin(ang)
    cos_ref[...] = jnp.cos(ang)
    sin_ref[...] = jnp.where(lane < RET_HEAD_DIM // 2, -sin, sin)


def _rope_tables(n, pos0):
    half = RET_HEAD_DIM // 2
    inv = ROPE_BASE ** (-jnp.arange(half, dtype=F32) / half)
    inv2 = jnp.concatenate([inv, inv]).reshape(1, LANES)
    rows = min(n, 2048)
    assert n % rows == 0
    return pl.pallas_call(
        functools.partial(_rope_kernel, rows=rows, pos0=pos0),
        grid=(n // rows,),
        in_specs=[pl.BlockSpec((1, LANES), lambda i: (0, 0))],
        out_specs=[pl.BlockSpec((rows, LANES), lambda i: (i, 0))] * 2,
        out_shape=[jax.ShapeDtypeStruct((n, LANES), F32)] * 2,
        compiler_params=_params(("arbitrary",)),
        name="rope_tables",
    )(inv2)


def _proj_kernel(x_ref, sc_ref, sh_ref, g_ref, cos_ref, sin_ref, w_ref, *rest, tm, kb, prompt):
    if prompt:
        wvt_ref, q_ref, k_ref, kb_ref, v_ref, vt_ref, qr_ref, kr_ref, vr_ref, gate_ref = rest
    else:
        q_ref, k_ref, v_ref, qr_ref, kr_ref, vr_ref, gate_ref = rest
    h = _rms(x_ref[...], g_ref[...]) * (1.0 + sc_ref[0]) + sh_ref[0]
    hb = h.astype(BF16)

    def seg(a, b):
        return _dot(hb, w_ref[:, a:b])

    q_ref[...] = (seg(0, SB_WIDTH) * (SB_HEAD_DIM ** -0.5 * LOG2E)).astype(BF16)
    k = seg(SB_WIDTH, 2 * SB_WIDTH)
    k_ref[...] = k
    v_ref[...] = seg(2 * SB_WIDTH, 3 * SB_WIDTH)
    if prompt:
        kb_ref[...] = k.astype(BF16)
        vt = _dot_nt(wvt_ref[...], hb).astype(BF16)
        for hp in range(SB_WIDTH // LANES):
            for j in range(tm // kb):
                vt_ref[0, hp, j] = vt[hp * LANES:(hp + 1) * LANES, j * kb:(j + 1) * kb]
    cos = cos_ref[...]
    sin = sin_ref[...]
    base = 3 * SB_WIDTH
    qr = seg(base, base + RET_WIDTH)
    kr = seg(base + RET_WIDTH, base + 2 * RET_WIDTH)
    for hh in range(RET_HEADS):
        sl = slice(hh * RET_HEAD_DIM, (hh + 1) * RET_HEAD_DIM)
        qh = qr[:, sl]
        kh = kr[:, sl]
        qr_ref[:, sl] = (qh * cos + pltpu.roll(qh, RET_HEAD_DIM // 2, 1) * sin).astype(BF16)
        kr_ref[:, sl] = ((kh * cos + pltpu.roll(kh, RET_HEAD_DIM // 2, 1) * sin)
                         * RET_HEAD_DIM ** -0.5).astype(BF16)
    vr_ref[...] = seg(base + 2 * RET_WIDTH, base + 3 * RET_WIDTH).astype(BF16)
    gate_ref[...] = seg(base + 3 * RET_WIDTH, base + 4 * RET_WIDTH)


def _proj(x, sc, sh, g, cos, sin, w_in_b, wvt_b, *, batch, prompt):
    n = x.shape[0]
    tm = min(TOKEN_TILE, n)
    tiles = n // tm
    per_seq = max(tiles // batch, 1)
    pos_tiles = cos.shape[0] // tm
    kb = SB_TILE
    mrows = sc.shape[1]

    def row(i):
        return (i, 0)

    def mod(i):
        return (i // per_seq if mrows == 1 else i, 0, 0)

    def pos(i):
        return (i % pos_tiles, 0)

    in_specs = [
        pl.BlockSpec((tm, D_MODEL), row),
        pl.BlockSpec((1, mrows, D_MODEL), mod),
        pl.BlockSpec((1, mrows, D_MODEL), mod),
        _resident((1, D_MODEL)),
        pl.BlockSpec((tm, LANES), pos),
        pl.BlockSpec((tm, LANES), pos),
        _resident((D_MODEL, IN_WIDTH)),
    ]
    args = [x, sc, sh, g, cos, sin, w_in_b]
    wide_b = jax.ShapeDtypeStruct((n, SB_WIDTH), BF16)
    wide_f = jax.ShapeDtypeStruct((n, SB_WIDTH), F32)
    wide = pl.BlockSpec((tm, SB_WIDTH), row)
    if prompt:
        in_specs.append(_resident((SB_WIDTH, D_MODEL)))
        args.append(wvt_b)
        nkb = n // batch // kb
        hps = SB_WIDTH // LANES
        out_shape = [wide_b, wide_f, wide_b, wide_f,
                     jax.ShapeDtypeStruct((batch, hps, nkb, LANES, kb), BF16),
                     wide_b, wide_b, wide_b, wide_f]
        out_specs = [wide, wide, wide, wide,
                     pl.BlockSpec((1, hps, tm // kb, LANES, kb),
                                  lambda i: (i // per_seq, 0, i % per_seq, 0, 0)),
                     wide, wide, wide, wide]
    else:
        out_shape = [wide_b, wide_f, wide_f, wide_b, wide_b, wide_b, wide_f]
        out_specs = [wide] * 7
    return pl.pallas_call(
        functools.partial(_proj_kernel, tm=tm, kb=kb, prompt=prompt),
        grid=(tiles,),
        in_specs=in_specs,
        out_specs=out_specs,
        out_shape=out_shape,
        compiler_params=_params(("arbitrary",)),
        name="proj_prompt" if prompt else "proj_sample",
    )(*args)


def _sb_prompt_kernel(q_ref, k_ref, vt_ref, o_ref, acc_ref, zt_ref, s_ref, w_ref, *, t, nq):
    i = pl.program_id(2)
    hd = SB_HEAD_DIM
    heads = range(2)
    q = q_ref[0]
    lane = lax.broadcasted_iota(jnp.int32, (t, LANES), 1)
    qm = (jnp.where(lane < hd, q, jnp.zeros_like(q)), jnp.where(lane >= hd, q, jnp.zeros_like(q)))
    kk = lax.broadcasted_iota(jnp.int32, (t, t), 0)
    qq = lax.broadcasted_iota(jnp.int32, (t, t), 1)
    tri = (qq >= kk).astype(BF16)
    valid = kk < qq

    def scores(j):
        kblk = k_ref[0, pl.ds(pl.multiple_of(j * t, t), t), :]
        return [_dot_nt(kblk, qm[h]) for h in heads]

    def survive(zt, s, cum, shift):
        r = _dot(tri, s)
        return jnp.exp2(jnp.minimum(zt - r, 0.0) + (cum + shift)), cum - r[0:1, :]

    def gather(j, ws):
        vblk = vt_ref[0, 0, j]
        return [_dot(vblk[h * hd:(h + 1) * hd, :], ws[h]) for h in heads]

    zts = scores(i)
    zero = jnp.zeros((1, t), F32)
    ss = [jnp.where(valid, _softplus2(zts[h]), 0.0).astype(BF16) for h in heads]
    ws, cums = zip(*[survive(zts[h], ss[h], zero, 0.0) for h in heads])
    parts = gather(i, [jnp.where(valid, ws[h], 0.0).astype(BF16) for h in heads])
    for h in heads:
        acc_ref[h * hd:(h + 1) * hd, :] = parts[h]

    def block_of(m):
        return jnp.clip(i - 1 - m, 0, nq - 1)

    first = scores(block_of(0))
    for h in heads:
        zt_ref[0, h] = first[h]
        zt_ref[2, h] = jnp.zeros((t, t), F32)
        s_ref[2, h] = jnp.zeros((t, t), BF16)
        w_ref[1, h] = jnp.zeros((t, t), BF16)

    def trip(n, r, cums):
        nxt_slot, cur, old, done = (r + 1) % 3, r, (r + 2) % 3, (r + 1) % 3
        shift = jnp.where(jnp.logical_and(n >= 1, n <= i), 0.0, -SHIFT_OUT)
        new_ws, new_cums = zip(*[survive(zt_ref[old, h], s_ref[old, h], cums[h], shift)
                                 for h in heads])
        nxt = scores(block_of(n + 1))
        parts = gather(block_of(n - 2), [w_ref[done, h] for h in heads])
        for h in heads:
            s_ref[cur, h] = _softplus2(zt_ref[cur, h]).astype(BF16)
        for h in heads:
            acc_ref[h * hd:(h + 1) * hd, :] += parts[h]
            w_ref[old, h] = new_ws[h].astype(BF16)
            zt_ref[nxt_slot, h] = nxt[h]
        return tuple(new_cums)

    rounds = jnp.where(i > 0, (i + 1 + SB_UNROLL) // SB_UNROLL, 0)

    def more(state):
        p, faded, _ = state
        return jnp.logical_and(p < rounds, jnp.logical_not(faded))

    def body(state):
        p, _, cums = state
        faded = jnp.max(jnp.maximum(cums[0], cums[1])) < -UNDERFLOW_LOG2
        for r in range(SB_UNROLL):
            cums = trip(SB_UNROLL * p + r, r % 3, cums)
        return p + 1, faded, cums

    lax.while_loop(more, body, (jnp.int32(0), jnp.bool_(False), tuple(cums)))
    o_ref[0] = acc_ref[...].T


def _sb_prompt(q, kb16, vt, *, batch):
    n = q.shape[0]
    s = n // batch
    t = SB_TILE
    nq = s // t
    hps = SB_WIDTH // LANES
    q3 = q.reshape(batch, s, SB_WIDTH)
    k3 = kb16.reshape(batch, s, SB_WIDTH)
    out = pl.pallas_call(
        functools.partial(_sb_prompt_kernel, t=t, nq=nq),
        grid=(batch, hps, nq),
        in_specs=[
            pl.BlockSpec((1, t, LANES), lambda b, hp, i: (b, i, hp)),
            pl.BlockSpec((1, s, LANES), lambda b, hp, i: (b, 0, hp)),
            pl.BlockSpec((1, 1, nq, LANES, t), lambda b, hp, i: (b, hp, 0, 0, 0)),
        ],
        out_specs=pl.BlockSpec((1, t, LANES), lambda b, hp, i: (b, i, hp)),
        out_shape=jax.ShapeDtypeStruct((batch, s, SB_WIDTH), F32),
        scratch_shapes=[pltpu.VMEM((LANES, t), F32), pltpu.VMEM((3, 2, t, t), F32),
                        pltpu.VMEM((3, 2, t, t), BF16), pltpu.VMEM((3, 2, t, t), BF16)],
        compiler_params=_params(("arbitrary", "arbitrary", "arbitrary")),
        name="sb_prompt",
    )(q3, k3, vt)
    return out.reshape(n, SB_WIDTH)


def _sb_sample_kernel(q_ref, kn_ref, vn_ref, ck_ref, cv_ref, o_ref, *, tq, past, kb):
    hd = SB_HEAD_DIM
    lane = lax.broadcasted_iota(jnp.int32, (tq, LANES), 1)
    qi = lax.broadcasted_iota(jnp.int32, (tq, tq), 0)
    ki = lax.broadcasted_iota(jnp.int32, (tq, tq), 1)
    valid = ki < qi
    tri_new = (qi >= ki).astype(BF16)
    a = lax.broadcasted_iota(jnp.int32, (kb, kb), 0)
    b = lax.broadcasted_iota(jnp.int32, (kb, kb), 1)
    tri = (a >= b).astype(BF16)

    def cumsum_right(s, m):
        s_hi, s_lo = _split_bf16(s)
        return _dot(s_hi, m) + _dot(s_lo, m)

    pairs = range(SB_WIDTH // LANES)
    chains = range(2 * len(pairs))
    slices = [slice(hp * LANES, (hp + 1) * LANES) for hp in pairs]
    qms = []
    for hp in pairs:
        q2 = q_ref[:, slices[hp]]
        qms += [jnp.where(lane < hd, q2, jnp.zeros_like(q2)),
                jnp.where(lane >= hd, q2, jnp.zeros_like(q2))]

    def sweep(kblks, vblks, m, cums, mask=None):
        zs = [_dot_nt(qms[c], kblks[c // 2]) for c in chains]
        ss = [_softplus2(z) for z in zs]
        if mask is not None:
            ss = [jnp.where(mask, s, 0.0) for s in ss]
        rs = [cumsum_right(s, m) for s in ss]
        ws = [jnp.exp2(zs[c] - rs[c] + cums[c]) for c in chains]
        if mask is not None:
            ws = [jnp.where(mask, w, 0.0) for w in ws]
        outs = [_dot(ws[c].astype(BF16), vblks[c // 2]) for c in chains]
        return outs, [cums[c] - rs[c][:, 0:1] for c in chains]

    zero = jnp.zeros((tq, 1), F32)
    accs, cums = sweep([kn_ref[:, sl].astype(BF16) for sl in slices],
                       [vn_ref[:, sl].astype(BF16) for sl in slices],
                       tri_new, [zero] * len(chains), valid)

    def body(n, carry):
        accs, cums = carry
        start = pl.multiple_of(past - (n + 1) * kb, kb)
        outs, cums = sweep([ck_ref[0, 0, pl.ds(start, kb), sl].astype(BF16) for sl in slices],
                           [cv_ref[0, 0, pl.ds(start, kb), sl].astype(BF16) for sl in slices],
                           tri, cums)
        return tuple(accs[c] + outs[c] for c in chains), tuple(cums)

    accs, _ = lax.fori_loop(0, past // kb, body, (tuple(accs), tuple(cums)))
    for hp in pairs:
        o_ref[:, slices[hp]] = jnp.where(lane < hd, accs[2 * hp], accs[2 * hp + 1])


def _sb_sample(q, k_new, v_new, cache_k, cache_v, layer, *, batch):
    n = q.shape[0]
    tq = n // batch
    past = cache_k.shape[2]
    kb = min(MXU_DIM, past)
    assert past % kb == 0
    row = pl.BlockSpec((tq, SB_WIDTH), lambda b: (b, 0))
    cache = pl.BlockSpec((1, 1, past, SB_WIDTH), lambda b: (layer, b, 0, 0))
    return pl.pallas_call(
        functools.partial(_sb_sample_kernel, tq=tq, past=past, kb=kb),
        grid=(batch,),
        in_specs=[row, row, row, cache, cache],
        out_specs=row,
        out_shape=jax.ShapeDtypeStruct((n, SB_WIDTH), F32),
        compiler_params=_params(("arbitrary",)),
        name="sb_sample",
    )(q, k_new, v_new, cache_k, cache_v)


def _ret_log_gamma(h):
    return math.log1p(-2.0 ** (-5 - h))


def _ret_kernel(q_ref, k_ref, v_ref, s0_ref, o_ref, sout_ref, state, dec, qdec, kdec, *, c, nchunks):
    seq = pl.program_id(0)
    ci = pl.program_id(1)

    @pl.when(jnp.logical_and(seq == 0, ci == 0))
    def _():
        li = lax.broadcasted_iota(jnp.int32, (c, c), 0)
        mi = lax.broadcasted_iota(jnp.int32, (c, c), 1)
        diff = (li - mi).astype(F32)
        pos = lax.broadcasted_iota(jnp.int32, (c, RET_HEAD_DIM), 0).astype(F32)
        for h in range(RET_HEADS):
            lg = _ret_log_gamma(h)
            dec[h] = jnp.where(diff >= 0.0, jnp.exp(jnp.maximum(diff, 0.0) * lg), 0.0)
            qdec[h] = jnp.exp((pos + 1.0) * lg)
            kdec[h] = jnp.exp((c - 1.0 - pos) * lg)

    @pl.when(ci == 0)
    def _():
        state[...] = s0_ref[0]

    for h in range(RET_HEADS):
        sl = slice(h * RET_HEAD_DIM, (h + 1) * RET_HEAD_DIM)
        qh = q_ref[:, sl]
        kh = k_ref[:, sl]
        vh = v_ref[:, sl]
        st = state[h]
        scores = _dot_nt(qh, kh) * dec[h]
        inner = _dot(scores.astype(BF16), vh)
        cross = _dot(qh, st.astype(BF16)) * qdec[h]
        o_ref[:, sl] = inner + cross
        kd = (kh.astype(F32) * kdec[h]).astype(BF16)
        state[h] = math.exp(c * _ret_log_gamma(h)) * st + _dot_tn(kd, vh)

    @pl.when(ci == nchunks - 1)
    def _():
        sout_ref[0] = state[...]


def _retention(q, k, v, state0, *, batch):
    n = q.shape[0]
    t = n // batch
    c = min(RET_CHUNK, t)
    nchunks = t // c
    row = pl.BlockSpec((c, RET_WIDTH), lambda b, i: (b * nchunks + i, 0))
    st = pl.BlockSpec((1, RET_HEADS, RET_HEAD_DIM, RET_HEAD_DIM), lambda b, i: (b, 0, 0, 0))
    return pl.pallas_call(
        functools.partial(_ret_kernel, c=c, nchunks=nchunks),
        grid=(batch, nchunks),
        in_specs=[row, row, row, st],
        out_specs=[row, st],
        out_shape=[jax.ShapeDtypeStruct((n, RET_WIDTH), F32),
                   jax.ShapeDtypeStruct((batch, RET_HEADS, RET_HEAD_DIM, RET_HEAD_DIM), F32)],
        scratch_shapes=[pltpu.VMEM((RET_HEADS, RET_HEAD_DIM, RET_HEAD_DIM), F32),
                        pltpu.VMEM((RET_HEADS, c, c), F32),
                        pltpu.VMEM((RET_HEADS, c, RET_HEAD_DIM), F32),
                        pltpu.VMEM((RET_HEADS, c, RET_HEAD_DIM), F32)],
        compiler_params=_params(("arbitrary", "arbitrary")),
        name="retention",
    )(q, k, v, state0)


def _merge_ffn_kernel(x_ref, osb_ref, oret_ref, gate_ref, gt1_ref, sc2_ref, sh2_ref, gt2_ref,
                      gsb_ref, gret_ref, gffn_ref, gfin_ref, wout_ref, wfi_ref, wfo_ref,
                      *outs, final):
    xo_ref = outs[0]
    a = _rms(osb_ref[...], gsb_ref[...]).astype(BF16)
    mix = _dot(a, wout_ref[0:SB_WIDTH, :])
    gate = gate_ref[...]
    gate = gate * jax.nn.sigmoid(gate)
    for h in range(RET_HEADS):
        sl = slice(h * RET_HEAD_DIM, (h + 1) * RET_HEAD_DIM)
        r = (_rms(oret_ref[:, sl], gret_ref[:, sl]) * gate[:, sl]).astype(BF16)
        mix += _dot(r, wout_ref[SB_WIDTH + h * RET_HEAD_DIM:SB_WIDTH + (h + 1) * RET_HEAD_DIM, :])
    x1 = x_ref[...] + gt1_ref[0] * mix
    h2 = (_rms(x1, gffn_ref[...]) * (1.0 + sc2_ref[0]) + sh2_ref[0]).astype(BF16)
    ff = jnp.zeros_like(x1)
    for j in range(D_FF // FF_CHUNK):
        cols = slice(j * FF_CHUNK, (j + 1) * FF_CHUNK)
        ucols = slice(D_FF + j * FF_CHUNK, D_FF + (j + 1) * FF_CHUNK)
        g = _dot(h2, wfi_ref[:, cols])
        u = _dot(h2, wfi_ref[:, ucols])
        act = (g * jax.nn.sigmoid(g) * u).astype(BF16)
        ff += _dot(act, wfo_ref[cols, :])
    x2 = x1 + gt2_ref[0] * ff
    xo_ref[...] = x2
    if final:
        outs[1][...] = _rms(x2, gfin_ref[...])


def _merge_ffn(x, o_sb, o_ret, gate, gt1, sc2, sh2, gt2, g_sb, g_ret, g_ffn, g_fin,
               w_out_b, w_fi_b, w_fo_b, *, batch, final):
    n = x.shape[0]
    tm = min(TOKEN_TILE, n)
    tiles = n // tm
    per_seq = max(tiles // batch, 1)
    mrows = gt1.shape[1]

    def row(i):
        return (i, 0)

    def mod(i):
        return (i // per_seq if mrows == 1 else i, 0, 0)

    full = pl.BlockSpec((tm, D_MODEL), row)
    half = pl.BlockSpec((tm, SB_WIDTH), row)
    mspec = pl.BlockSpec((1, mrows, D_MODEL), mod)
    n_out = 2 if final else 1
    res = pl.pallas_call(
        functools.partial(_merge_ffn_kernel, final=final),
        grid=(tiles,),
        in_specs=[full, half, half, half, mspec, mspec, mspec, mspec,
                  _resident((1, SB_WIDTH)), _resident((1, RET_WIDTH)),
                  _resident((1, D_MODEL)), _resident((1, D_MODEL)),
                  _resident((D_MODEL, D_MODEL)), _resident((D_MODEL, 2 * D_FF)),
                  _resident((D_FF, D_MODEL))],
        out_specs=[full] * n_out,
        out_shape=[jax.ShapeDtypeStruct((n, D_MODEL), F32)] * n_out,
        compiler_params=_params(("arbitrary",)),
        name="merge_ffn",
    )(x, o_sb, o_ret, gate, gt1, sc2, sh2, gt2, g_sb, g_ret, g_ffn, g_fin, w_out_b, w_fi_b, w_fo_b)
    return res if final else (res[0], None)


def kernel(x_prompt, x_sample, cache_sb_k, cache_sb_v, state_ret, c_prompt, c_sample,
           g_norm_mix, g_norm_ffn, w_ada, b_ada, w_in, g_sb_out, g_ret_out, w_out,
           w_ff_in, w_ff_out, g_final):
    depth = w_in.shape[0]
    bp, sp, _ = x_prompt.shape
    bs, ts, _ = x_sample.shape
    past = cache_sb_k.shape[2]
    n_p = bp * sp
    n_s = bs * ts

    c_all = jnp.concatenate([c_prompt, c_sample], axis=0)
    pad = (-c_all.shape[0]) % 16
    mods = _modulation(jnp.pad(c_all, ((0, pad), (0, 0))), w_ada, b_ada)
    mods = mods.reshape(depth, -1, N_MOD, D_MODEL)
    mods_p = mods[:, :bp]
    mods_s = jnp.repeat(mods[:, bp:bp + bs], ts, axis=1)

    cos_p, sin_p = _rope_tables(sp, 0)
    cos_s, sin_s = _rope_tables(ts, past)
    cos_s = jnp.tile(cos_s, (bs, 1))
    sin_s = jnp.tile(sin_s, (bs, 1))

    cache_k = cache_sb_k.reshape(depth, bs, past, SB_WIDTH)
    cache_v = cache_sb_v.reshape(depth, bs, past, SB_WIDTH)
    zero_state = jnp.zeros((bp, RET_HEADS, RET_HEAD_DIM, RET_HEAD_DIM), F32)

    xp = x_prompt.reshape(n_p, D_MODEL)
    xs = x_sample.reshape(n_s, D_MODEL)
    kp, vp, rp, ks, vs, rs = [], [], [], [], [], []
    yp = ys = None
    for l in range(depth):
        final = l == depth - 1
        w_in_b = w_in[l].astype(BF16)
        wvt_b = w_in[l][:, 2 * SB_WIDTH:3 * SB_WIDTH].T.astype(BF16)
        w_out_b = w_out[l].astype(BF16)
        w_fi_b = w_ff_in[l].astype(BF16)
        w_fo_b = w_ff_out[l].astype(BF16)
        g_mix = g_norm_mix[l].reshape(1, D_MODEL)
        g_ffn = g_norm_ffn[l].reshape(1, D_MODEL)
        g_sb = g_sb_out[l].reshape(1, SB_WIDTH)
        g_ret = g_ret_out[l].reshape(1, RET_WIDTH)
        g_fin = g_final.reshape(1, D_MODEL)

        m = [mods_p[l, :, j].reshape(bp, 1, D_MODEL) for j in range(N_MOD)]
        q, k, k16, v, vt, qr, kr, vr, gate = _proj(
            xp, m[1], m[0], g_mix, cos_p, sin_p, w_in_b, wvt_b, batch=bp, prompt=True)
        o_sb = _sb_prompt(q, k16, vt, batch=bp)
        o_ret, st = _retention(qr, kr, vr, zero_state, batch=bp)
        xp, yp = _merge_ffn(xp, o_sb, o_ret, gate, m[2], m[4], m[3], m[5], g_sb, g_ret, g_ffn,
                            g_fin, w_out_b, w_fi_b, w_fo_b, batch=bp, final=final)
        kp.append(k)
        vp.append(v)
        rp.append(st)

        m = [mods_s[l, :, j].reshape(1, n_s, D_MODEL) for j in range(N_MOD)]
        q, k, v, qr, kr, vr, gate = _proj(
            xs, m[1], m[0], g_mix, cos_s, sin_s, w_in_b, None, batch=bs, prompt=False)
        o_sb = _sb_sample(q, k, v, cache_k, cache_v, l, batch=bs)
        o_ret, st = _retention(qr, kr, vr, state_ret[l], batch=bs)
        xs, ys = _merge_ffn(xs, o_sb, o_ret, gate, m[2], m[4], m[3], m[5], g_sb, g_ret, g_ffn,
                            g_fin, w_out_b, w_fi_b, w_fo_b, batch=bs, final=final)
        ks.append(k)
        vs.append(v)
        rs.append(st)

    def heads(a, b, t):
        return jnp.stack(a).reshape(depth, b, t, SB_HEADS, SB_HEAD_DIM)

    return (yp.reshape(bp, sp, D_MODEL), ys.reshape(bs, ts, D_MODEL),
            heads(kp, bp, sp), heads(vp, bp, sp), jnp.stack(rp),
            heads(ks, bs, ts), heads(vs, bs, ts), jnp.stack(rs))
```

```python
import functools
import math

import jax
import jax.numpy as jnp
from jax import lax
from jax.experimental import pallas as pl
from jax.experimental.pallas import tpu as pltpu

F32 = jnp.float32
BF16 = jnp.bfloat16

D_MODEL = 1024
SB_WIDTH = 512
RET_WIDTH = 512
SB_HEAD_DIM = 64
SB_HEADS = SB_WIDTH // SB_HEAD_DIM
RET_HEAD_DIM = 128
RET_HEADS = RET_WIDTH // RET_HEAD_DIM
IN_WIDTH = 3 * SB_WIDTH + 4 * RET_WIDTH
D_FF = 2816
N_MOD = 6
ROPE_BASE = 10000.0
EPS = 1e-6
LOG2E = 1.4426950408889634
LN2 = 0.6931471805599453

LANES = 128
MXU_DIM = 256
VMEM_LIMIT_BYTES = 56 * 1024 * 1024

TOKEN_TILE = 512
SB_TILE = MXU_DIM
SB_UNROLL = 3
RET_CHUNK = MXU_DIM
FF_CHUNK = MXU_DIM
SOFTPLUS_CLAMP = 30.0
SHIFT_OUT = 1e30
UNDERFLOW_LOG2 = 160.0


def _dot(a, b):
    return jnp.dot(a, b, preferred_element_type=F32)


def _dot_nt(a, b):
    return lax.dot_general(a, b, (((1,), (1,)), ((), ())), preferred_element_type=F32)


def _dot_tn(a, b):
    return lax.dot_general(a, b, (((0,), (0,)), ((), ())), preferred_element_type=F32)


def _split_bf16(x):
    hi = x.astype(BF16)
    lo = (x - hi.astype(F32)).astype(BF16)
    return hi, lo


def _rms(x, g):
    return x * lax.rsqrt(jnp.mean(x * x, axis=-1, keepdims=True) + EPS) * g


def _softplus2(z2):
    e = jnp.exp2(jnp.minimum(z2, SOFTPLUS_CLAMP))
    return jnp.maximum(z2, jnp.log(1.0 + e) * LOG2E)


def _params(semantics, flags=None):
    return pltpu.CompilerParams(dimension_semantics=semantics, vmem_limit_bytes=VMEM_LIMIT_BYTES,
                                flags=flags)


def _resident(shape):
    zeros = (0,) * len(shape)
    return pl.BlockSpec(shape, lambda *_: zeros, pipeline_mode=pl.Buffered(1))


def _mod_kernel(c_ref, w_ref, b_ref, o_ref):
    c = c_ref[...]
    a = c * jax.nn.sigmoid(c)
    a_hi, a_lo = _split_bf16(a)
    w_hi, w_lo = _split_bf16(w_ref[0])
    o_ref[0] = _dot(a_hi, w_hi) + _dot(a_lo, w_hi) + _dot(a_hi, w_lo) + b_ref[0]


def _modulation(c, w_ada, b_ada):
    depth = w_ada.shape[0]
    rows = c.shape[0]
    return pl.pallas_call(
        _mod_kernel,
        grid=(depth, N_MOD),
        in_specs=[
            pl.BlockSpec((rows, D_MODEL), lambda l, j: (0, 0)),
            pl.BlockSpec((1, D_MODEL, D_MODEL), lambda l, j: (l, 0, j)),
            pl.BlockSpec((1, 1, D_MODEL), lambda l, j: (l, 0, j)),
        ],
        out_specs=pl.BlockSpec((1, rows, D_MODEL), lambda l, j: (l, 0, j)),
        out_shape=jax.ShapeDtypeStruct((depth, rows, N_MOD * D_MODEL), F32),
        compiler_params=_params(("arbitrary", "arbitrary")),
        name="modulation",
    )(c, w_ada, b_ada.reshape(depth, 1, N_MOD * D_MODEL))


def _rope_kernel(inv_ref, cos_ref, sin_ref, *, rows, pos0):
    row = lax.broadcasted_iota(jnp.int32, (rows, LANES), 0) + (pl.program_id(0) * rows + pos0)
    lane = lax.broadcasted_iota(jnp.int32, (rows, LANES), 1)
    ang = row.astype(F32) * inv_ref[...]
    sin = jnp.sin(ang)
    cos_ref[...] = jnp.cos(ang)
    sin_ref[...] = jnp.where(lane < RET_HEAD_DIM // 2, -sin, sin)


def _rope_tables(n, pos0):
    half = RET_HEAD_DIM // 2
    inv = ROPE_BASE ** (-jnp.arange(half, dtype=F32) / half)
    inv2 = jnp.concatenate([inv, inv]).reshape(1, LANES)
    rows = min(n, 2048)
    assert n % rows == 0
    return pl.pallas_call(
        functools.partial(_rope_kernel, rows=rows, pos0=pos0),
        grid=(n // rows,),
        in_specs=[pl.BlockSpec((1, LANES), lambda i: (0, 0))],
        out_specs=[pl.BlockSpec((rows, LANES), lambda i: (i, 0))] * 2,
        out_shape=[jax.ShapeDtypeStruct((n, LANES), F32)] * 2,
        compiler_params=_params(("arbitrary",)),
        name="rope_tables",
    )(inv2)


def _proj_kernel(x_ref, sc_ref, sh_ref, g_ref, cos_ref, sin_ref, w_ref, *rest, tm, kb, prompt):
    if prompt:
        wvt_ref, _, _, q_ref, k_ref, kb_ref, v_ref, vt_ref, qr_ref, kr_ref, vr_ref, gate_ref = rest
    else:
        q_ref, k_ref, v_ref, qr_ref, kr_ref, vr_ref, gate_ref = rest
    h = _rms(x_ref[...], g_ref[...]) * (1.0 + sc_ref[0]) + sh_ref[0]
    hb = h.astype(BF16)

    def seg(a, b):
        return _dot(hb, w_ref[:, a:b])

    q_ref[...] = (seg(0, SB_WIDTH) * (SB_HEAD_DIM ** -0.5 * LOG2E)).astype(BF16)
    k = seg(SB_WIDTH, 2 * SB_WIDTH)
    v = seg(2 * SB_WIDTH, 3 * SB_WIDTH)
    if prompt:
        k_ref[0] = k.reshape(tm, SB_HEADS, SB_HEAD_DIM)
        v_ref[0] = v.reshape(tm, SB_HEADS, SB_HEAD_DIM)
        kb_ref[...] = k.astype(BF16)
        vt = _dot_nt(wvt_ref[...], hb).astype(BF16)
        for hp in range(SB_WIDTH // LANES):
            for j in range(tm // kb):
                vt_ref[0, hp, j] = vt[hp * LANES:(hp + 1) * LANES, j * kb:(j + 1) * kb]
    else:
        k_ref[...] = k
        v_ref[...] = v
    cos = cos_ref[...]
    sin = sin_ref[...]
    base = 3 * SB_WIDTH
    qr = seg(base, base + RET_WIDTH)
    kr = seg(base + RET_WIDTH, base + 2 * RET_WIDTH)
    for hh in range(RET_HEADS):
        sl = slice(hh * RET_HEAD_DIM, (hh + 1) * RET_HEAD_DIM)
        qh = qr[:, sl]
        kh = kr[:, sl]
        qr_ref[:, sl] = (qh * cos + pltpu.roll(qh, RET_HEAD_DIM // 2, 1) * sin).astype(BF16)
        kr_ref[:, sl] = ((kh * cos + pltpu.roll(kh, RET_HEAD_DIM // 2, 1) * sin)
                         * RET_HEAD_DIM ** -0.5).astype(BF16)
    vr_ref[...] = seg(base + 2 * RET_WIDTH, base + 3 * RET_WIDTH).astype(BF16)
    gate_ref[...] = seg(base + 3 * RET_WIDTH, base + 4 * RET_WIDTH)


def _proj(x, sc, sh, g, cos, sin, w_in_b, wvt_b=None, kv_out=None, layer=0, *, batch, prompt):
    n = x.shape[0]
    tm = min(TOKEN_TILE, n)
    tiles = n // tm
    per_seq = max(tiles // batch, 1)
    pos_tiles = cos.shape[0] // tm
    kb = SB_TILE
    mrows = sc.shape[1]

    def row(i):
        return (i, 0)

    def mod(i):
        return (i // per_seq if mrows == 1 else i, 0, 0)

    def pos(i):
        return (i % pos_tiles, 0)

    in_specs = [
        pl.BlockSpec((tm, D_MODEL), row),
        pl.BlockSpec((1, mrows, D_MODEL), mod),
        pl.BlockSpec((1, mrows, D_MODEL), mod),
        _resident((1, D_MODEL)),
        pl.BlockSpec((tm, LANES), pos),
        pl.BlockSpec((tm, LANES), pos),
        _resident((D_MODEL, IN_WIDTH)),
    ]
    args = [x, sc, sh, g, cos, sin, w_in_b]
    wide_b = jax.ShapeDtypeStruct((n, SB_WIDTH), BF16)
    wide_f = jax.ShapeDtypeStruct((n, SB_WIDTH), F32)
    wide = pl.BlockSpec((tm, SB_WIDTH), row)
    if prompt:
        in_specs += [_resident((SB_WIDTH, D_MODEL)), pl.BlockSpec(memory_space=pl.ANY),
                     pl.BlockSpec(memory_space=pl.ANY)]
        aliases = {len(args) + 1: 1, len(args) + 2: 3}
        args += [wvt_b, *kv_out]
        nkb = n // batch // kb
        hps = SB_WIDTH // LANES
        head_f = jax.ShapeDtypeStruct(kv_out[0].shape, F32)
        head = pl.BlockSpec((1, tm, SB_HEADS, SB_HEAD_DIM), lambda i: (layer, i, 0, 0))
        out_shape = [wide_b, head_f, wide_b, head_f,
                     jax.ShapeDtypeStruct((batch, hps, nkb, LANES, kb), BF16),
                     wide_b, wide_b, wide_b, wide_f]
        out_specs = [wide, head, wide, head,
                     pl.BlockSpec((1, hps, tm // kb, LANES, kb),
                                  lambda i: (i // per_seq, 0, i % per_seq, 0, 0)),
                     wide, wide, wide, wide]
    else:
        aliases = {}
        out_shape = [wide_b, wide_f, wide_f, wide_b, wide_b, wide_b, wide_f]
        out_specs = [wide] * 7
    return pl.pallas_call(
        functools.partial(_proj_kernel, tm=tm, kb=kb, prompt=prompt),
        grid=(tiles,),
        in_specs=in_specs,
        out_specs=out_specs,
        out_shape=out_shape,
        input_output_aliases=aliases,
        compiler_params=_params(("arbitrary",)),
        name="proj_prompt" if prompt else "proj_sample",
    )(*args)


def _sb_prompt_kernel(q_ref, k_ref, vt_ref, o_ref, acc_ref, zt_ref, s_ref, w_ref, *, t, nq):
    i = pl.program_id(2)
    hd = SB_HEAD_DIM
    heads = range(2)
    q = q_ref[0]
    lane = lax.broadcasted_iota(jnp.int32, (t, LANES), 1)
    qm = (jnp.where(lane < hd, q, jnp.zeros_like(q)), jnp.where(lane >= hd, q, jnp.zeros_like(q)))
    kk = lax.broadcasted_iota(jnp.int32, (t, t), 0)
    qq = lax.broadcasted_iota(jnp.int32, (t, t), 1)
    tri = (qq >= kk).astype(BF16)
    valid = kk < qq

    def scores(j):
        kblk = k_ref[0, pl.ds(pl.multiple_of(j * t, t), t), :]
        return [_dot_nt(kblk, qm[h]) for h in heads]

    def survive(zt, s, cum, shift):
        r = _dot(tri, s)
        return jnp.exp2(jnp.minimum(zt - r, 0.0) + (cum + shift)), cum - r[0:1, :]

    def gather(j, ws):
        vblk = vt_ref[0, 0, j]
        return [_dot(vblk[h * hd:(h + 1) * hd, :], ws[h]) for h in heads]

    zts = scores(i)
    zero = jnp.zeros((1, t), F32)
    ss = [jnp.where(valid, _softplus2(zts[h]), 0.0).astype(BF16) for h in heads]
    ws, cums = zip(*[survive(zts[h], ss[h], zero, 0.0) for h in heads])
    parts = gather(i, [jnp.where(valid, ws[h], 0.0).astype(BF16) for h in heads])
    for h in heads:
        acc_ref[h * hd:(h + 1) * hd, :] = parts[h]

    def block_of(m):
        return jnp.clip(i - 1 - m, 0, nq - 1)

    first = scores(block_of(0))
    for h in heads:
        zt_ref[0, h] = first[h]
        zt_ref[2, h] = jnp.zeros((t, t), F32)
        s_ref[2, h] = jnp.zeros((t, t), BF16)
        w_ref[1, h] = jnp.zeros((t, t), BF16)

    def trip(n, r, cums):
        nxt_slot, cur, old, done = (r + 1) % 3, r, (r + 2) % 3, (r + 1) % 3
        shift = jnp.where(jnp.logical_and(n >= 1, n <= i), 0.0, -SHIFT_OUT)
        new_ws, new_cums = zip(*[survive(zt_ref[old, h], s_ref[old, h], cums[h], shift)
                                 for h in heads])
        nxt = scores(block_of(n + 1))
        parts = gather(block_of(n - 2), [w_ref[done, h] for h in heads])
        for h in heads:
            s_ref[cur, h] = _softplus2(zt_ref[cur, h]).astype(BF16)
        for h in heads:
            acc_ref[h * hd:(h + 1) * hd, :] += parts[h]
            w_ref[old, h] = new_ws[h].astype(BF16)
            zt_ref[nxt_slot, h] = nxt[h]
        return tuple(new_cums)

    rounds = jnp.where(i > 0, (i + 1 + SB_UNROLL) // SB_UNROLL, 0)

    def more(state):
        p, faded, _ = state
        return jnp.logical_and(p < rounds, jnp.logical_not(faded))

    def body(state):
        p, faded, cums = state
        for r in range(SB_UNROLL):
            cums = trip(SB_UNROLL * p + r, r % 3, cums)
            if r == SB_UNROLL - 2:
                faded = jnp.max(jnp.maximum(cums[0], cums[1])) < -UNDERFLOW_LOG2
        return p + 1, faded, cums

    lax.while_loop(more, body, (jnp.int32(0), jnp.bool_(False), tuple(cums)))
    o_ref[0] = acc_ref[...].T


def _sb_prompt(q, kb16, vt, *, batch):
    n = q.shape[0]
    s = n // batch
    t = SB_TILE
    nq = s // t
    hps = SB_WIDTH // LANES
    q3 = q.reshape(batch, s, SB_WIDTH)
    k3 = kb16.reshape(batch, s, SB_WIDTH)
    out = pl.pallas_call(
        functools.partial(_sb_prompt_kernel, t=t, nq=nq),
        grid=(batch, hps, nq),
        in_specs=[
            pl.BlockSpec((1, t, LANES), lambda b, hp, i: (b, i, hp)),
            pl.BlockSpec((1, s, LANES), lambda b, hp, i: (b, 0, hp)),
            pl.BlockSpec((1, 1, nq, LANES, t), lambda b, hp, i: (b, hp, 0, 0, 0)),
        ],
        out_specs=pl.BlockSpec((1, t, LANES), lambda b, hp, i: (b, i, hp)),
        out_shape=jax.ShapeDtypeStruct((batch, s, SB_WIDTH), F32),
        scratch_shapes=[pltpu.VMEM((LANES, t), F32), pltpu.VMEM((3, 2, t, t), F32),
                        pltpu.VMEM((3, 2, t, t), BF16), pltpu.VMEM((3, 2, t, t), BF16)],
        compiler_params=_params(("arbitrary", "arbitrary", "arbitrary")),
        name="sb_prompt",
    )(q3, k3, vt)
    return out.reshape(n, SB_WIDTH)


def _sb_sample_kernel(q_ref, kn_ref, vn_ref, ck_ref, cv_ref, o_ref, *, tq, past, kb):
    hd = SB_HEAD_DIM
    heads = range(SB_HEADS)
    qi = lax.broadcasted_iota(jnp.int32, (tq, tq), 0)
    ki = lax.broadcasted_iota(jnp.int32, (tq, tq), 1)
    valid = ki < qi
    tri_new = (qi >= ki).astype(BF16)
    a = lax.broadcasted_iota(jnp.int32, (kb, kb), 0)
    b = lax.broadcasted_iota(jnp.int32, (kb, kb), 1)
    tri = (a >= b).astype(BF16)
    cols = [slice(h * hd, (h + 1) * hd) for h in heads]
    qs = [q_ref[:, c] for c in cols]

    def cumsum_right(s, m):
        s_hi, s_lo = _split_bf16(s)
        return _dot(s_hi, m) + _dot(s_lo, m)

    def sweep(ks, vs, m, cums, mask=None):
        zs = [_dot_nt(qs[h], ks[h]) for h in heads]
        ss = [_softplus2(z) for z in zs]
        if mask is not None:
            ss = [jnp.where(mask, s, 0.0) for s in ss]
        rs = [cumsum_right(s, m) for s in ss]
        ws = [jnp.exp2(zs[h] - rs[h] + cums[h]) for h in heads]
        if mask is not None:
            ws = [jnp.where(mask, w, 0.0) for w in ws]
        outs = [_dot(ws[h].astype(BF16), vs[h]) for h in heads]
        return outs, [cums[h] - rs[h][:, 0:1] for h in heads]

    zero = jnp.zeros((tq, 1), F32)
    accs, cums = sweep([kn_ref[:, c].astype(BF16) for c in cols],
                       [vn_ref[:, c].astype(BF16) for c in cols],
                       tri_new, [zero] * SB_HEADS, valid)

    def body(n, carry):
        accs, cums = carry
        row0 = pl.multiple_of((past - (n + 1) * kb) * SB_HEADS, kb * SB_HEADS)

        def rows(ref, h):
            return ref[0, 0, pl.ds(row0 + h, kb, stride=SB_HEADS), :].astype(BF16)

        outs, cums = sweep([rows(ck_ref, h) for h in heads], [rows(cv_ref, h) for h in heads],
                           tri, cums)
        return tuple(accs[h] + outs[h] for h in heads), tuple(cums)

    accs, _ = lax.fori_loop(0, past // kb, body, (tuple(accs), tuple(cums)))
    for h in heads:
        o_ref[:, cols[h]] = accs[h]


def _sb_sample(q, k_new, v_new, cache_k, cache_v, layer, *, batch):
    n = q.shape[0]
    tq = n // batch
    past = cache_k.shape[2] // SB_HEADS
    kb = min(MXU_DIM, past)
    assert past % kb == 0
    row = pl.BlockSpec((tq, SB_WIDTH), lambda b: (b, 0))
    cache = pl.BlockSpec((1, 1, past * SB_HEADS, SB_HEAD_DIM), lambda b: (layer, b, 0, 0))
    return pl.pallas_call(
        functools.partial(_sb_sample_kernel, tq=tq, past=past, kb=kb),
        grid=(batch,),
        in_specs=[row, row, row, cache, cache],
        out_specs=row,
        out_shape=jax.ShapeDtypeStruct((n, SB_WIDTH), F32),
        compiler_params=_params(("arbitrary",)),
        name="sb_sample",
    )(q, k_new, v_new, cache_k, cache_v)


def _ret_log_gamma(h):
    return math.log1p(-2.0 ** (-5 - h))


def _ret_kernel(q_ref, k_ref, v_ref, s0_ref, o_ref, sout_ref, state, dec, qdec, kdec, *, c, nchunks):
    seq = pl.program_id(0)
    ci = pl.program_id(1)

    @pl.when(jnp.logical_and(seq == 0, ci == 0))
    def _():
        li = lax.broadcasted_iota(jnp.int32, (c, c), 0)
        mi = lax.broadcasted_iota(jnp.int32, (c, c), 1)
        diff = (li - mi).astype(F32)
        pos = lax.broadcasted_iota(jnp.int32, (c, RET_HEAD_DIM), 0).astype(F32)
        for h in range(RET_HEADS):
            lg = _ret_log_gamma(h)
            dec[h] = jnp.where(diff >= 0.0, jnp.exp(jnp.maximum(diff, 0.0) * lg), 0.0)
            qdec[h] = jnp.exp((pos + 1.0) * lg)
            kdec[h] = jnp.exp((c - 1.0 - pos) * lg)

    @pl.when(ci == 0)
    def _():
        state[...] = s0_ref[0]

    for h in range(RET_HEADS):
        sl = slice(h * RET_HEAD_DIM, (h + 1) * RET_HEAD_DIM)
        qh = q_ref[:, sl]
        kh = k_ref[:, sl]
        vh = v_ref[:, sl]
        st = state[h]
        scores = _dot_nt(qh, kh) * dec[h]
        inner = _dot(scores.astype(BF16), vh)
        cross = _dot(qh, st.astype(BF16)) * qdec[h]
        o_ref[:, sl] = inner + cross
        kd = (kh.astype(F32) * kdec[h]).astype(BF16)
        state[h] = math.exp(c * _ret_log_gamma(h)) * st + _dot_tn(kd, vh)

    @pl.when(ci == nchunks - 1)
    def _():
        sout_ref[0] = state[...]


def _retention(q, k, v, state0, *, batch):
    n = q.shape[0]
    t = n // batch
    c = min(RET_CHUNK, t)
    nchunks = t // c
    row = pl.BlockSpec((c, RET_WIDTH), lambda b, i: (b * nchunks + i, 0))
    st = pl.BlockSpec((1, RET_HEADS, RET_HEAD_DIM, RET_HEAD_DIM), lambda b, i: (b, 0, 0, 0))
    return pl.pallas_call(
        functools.partial(_ret_kernel, c=c, nchunks=nchunks),
        grid=(batch, nchunks),
        in_specs=[row, row, row, st],
        out_specs=[row, st],
        out_shape=[jax.ShapeDtypeStruct((n, RET_WIDTH), F32),
                   jax.ShapeDtypeStruct((batch, RET_HEADS, RET_HEAD_DIM, RET_HEAD_DIM), F32)],
        scratch_shapes=[pltpu.VMEM((RET_HEADS, RET_HEAD_DIM, RET_HEAD_DIM), F32),
                        pltpu.VMEM((RET_HEADS, c, c), F32),
                        pltpu.VMEM((RET_HEADS, c, RET_HEAD_DIM), F32),
                        pltpu.VMEM((RET_HEADS, c, RET_HEAD_DIM), F32)],
        compiler_params=_params(("arbitrary", "arbitrary")),
        name="retention",
    )(q, k, v, state0)


def _merge_ffn_kernel(x_ref, osb_ref, oret_ref, gate_ref, gt1_ref, sc2_ref, sh2_ref, gt2_ref,
                      gsb_ref, gret_ref, gffn_ref, gfin_ref, wout_ref, wfi_ref, wfo_ref,
                      *outs, final):
    xo_ref = outs[0]
    a = _rms(osb_ref[...], gsb_ref[...]).astype(BF16)
    mix = _dot(a, wout_ref[0:SB_WIDTH, :])
    gate = gate_ref[...]
    gate = gate * jax.nn.sigmoid(gate)
    for h in range(RET_HEADS):
        sl = slice(h * RET_HEAD_DIM, (h + 1) * RET_HEAD_DIM)
        r = (_rms(oret_ref[:, sl], gret_ref[:, sl]) * gate[:, sl]).astype(BF16)
        mix += _dot(r, wout_ref[SB_WIDTH + h * RET_HEAD_DIM:SB_WIDTH + (h + 1) * RET_HEAD_DIM, :])
    x1 = x_ref[...] + gt1_ref[0] * mix
    h2 = (_rms(x1, gffn_ref[...]) * (1.0 + sc2_ref[0]) + sh2_ref[0]).astype(BF16)
    ff = jnp.zeros_like(x1)
    for j in range(D_FF // FF_CHUNK):
        cols = slice(j * FF_CHUNK, (j + 1) * FF_CHUNK)
        ucols = slice(D_FF + j * FF_CHUNK, D_FF + (j + 1) * FF_CHUNK)
        g = _dot(h2, wfi_ref[:, cols])
        u = _dot(h2, wfi_ref[:, ucols])
        act = (g * jax.nn.sigmoid(g) * u).astype(BF16)
        ff += _dot(act, wfo_ref[cols, :])
    x2 = x1 + gt2_ref[0] * ff
    xo_ref[...] = x2
    if final:
        outs[1][...] = _rms(x2, gfin_ref[...])


def _merge_ffn(x, o_sb, o_ret, gate, gt1, sc2, sh2, gt2, g_sb, g_ret, g_ffn, g_fin,
               w_out_b, w_fi_b, w_fo_b, *, batch, final):
    n = x.shape[0]
    tm = min(TOKEN_TILE, n)
    tiles = n // tm
    per_seq = max(tiles // batch, 1)
    mrows = gt1.shape[1]

    def row(i):
        return (i, 0)

    def mod(i):
        return (i // per_seq if mrows == 1 else i, 0, 0)

    full = pl.BlockSpec((tm, D_MODEL), row)
    half = pl.BlockSpec((tm, SB_WIDTH), row)
    mspec = pl.BlockSpec((1, mrows, D_MODEL), mod)
    n_out = 2 if final else 1
    res = pl.pallas_call(
        functools.partial(_merge_ffn_kernel, final=final),
        grid=(tiles,),
        in_specs=[full, half, half, half, mspec, mspec, mspec, mspec,
                  _resident((1, SB_WIDTH)), _resident((1, RET_WIDTH)),
                  _resident((1, D_MODEL)), _resident((1, D_MODEL)),
                  _resident((D_MODEL, D_MODEL)), _resident((D_MODEL, 2 * D_FF)),
                  _resident((D_FF, D_MODEL))],
        out_specs=[full] * n_out,
        out_shape=[jax.ShapeDtypeStruct((n, D_MODEL), F32)] * n_out,
        compiler_params=_params(("arbitrary",)),
        name="merge_ffn",
    )(x, o_sb, o_ret, gate, gt1, sc2, sh2, gt2, g_sb, g_ret, g_ffn, g_fin, w_out_b, w_fi_b, w_fo_b)
    return res if final else (res[0], None)


def kernel(x_prompt, x_sample, cache_sb_k, cache_sb_v, state_ret, c_prompt, c_sample,
           g_norm_mix, g_norm_ffn, w_ada, b_ada, w_in, g_sb_out, g_ret_out, w_out,
           w_ff_in, w_ff_out, g_final):
    depth = w_in.shape[0]
    bp, sp, _ = x_prompt.shape
    bs, ts, _ = x_sample.shape
    past = cache_sb_k.shape[2]
    n_p = bp * sp
    n_s = bs * ts

    c_all = jnp.concatenate([c_prompt, c_sample], axis=0)
    pad = (-c_all.shape[0]) % 16
    mods = _modulation(jnp.pad(c_all, ((0, pad), (0, 0))), w_ada, b_ada)
    mods = mods.reshape(depth, -1, N_MOD, D_MODEL)
    mods_p = mods[:, :bp]
    mods_s = jnp.repeat(mods[:, bp:bp + bs], ts, axis=1)

    cos_p, sin_p = _rope_tables(sp, 0)
    cos_s, sin_s = _rope_tables(ts, past)
    cos_s = jnp.tile(cos_s, (bs, 1))
    sin_s = jnp.tile(sin_s, (bs, 1))

    cache_k = cache_sb_k.reshape(depth, bs, past * SB_HEADS, SB_HEAD_DIM)
    cache_v = cache_sb_v.reshape(depth, bs, past * SB_HEADS, SB_HEAD_DIM)
    zero_state = jnp.zeros((bp, RET_HEADS, RET_HEAD_DIM, RET_HEAD_DIM), F32)

    xp = x_prompt.reshape(n_p, D_MODEL)
    xs = x_sample.reshape(n_s, D_MODEL)
    kv_p = (jnp.zeros((depth, n_p, SB_HEADS, SB_HEAD_DIM), F32),) * 2
    rp, ks, vs, rs = [], [], [], []
    yp = ys = None
    for l in range(depth):
        final = l == depth - 1
        w_in_b = w_in[l].astype(BF16)
        wvt_b = w_in[l][:, 2 * SB_WIDTH:3 * SB_WIDTH].T.astype(BF16)
        w_out_b = w_out[l].astype(BF16)
        w_fi_b = w_ff_in[l].astype(BF16)
        w_fo_b = w_ff_out[l].astype(BF16)
        g_mix = g_norm_mix[l].reshape(1, D_MODEL)
        g_ffn = g_norm_ffn[l].reshape(1, D_MODEL)
        g_sb = g_sb_out[l].reshape(1, SB_WIDTH)
        g_ret = g_ret_out[l].reshape(1, RET_WIDTH)
        g_fin = g_final.reshape(1, D_MODEL)

        m = [mods_p[l, :, j].reshape(bp, 1, D_MODEL) for j in range(N_MOD)]
        q, k, k16, v, vt, qr, kr, vr, gate = _proj(
            xp, m[1], m[0], g_mix, cos_p, sin_p, w_in_b, wvt_b, kv_p, l, batch=bp, prompt=True)
        kv_p = (k, v)
        o_sb = _sb_prompt(q, k16, vt, batch=bp)
        o_ret, st = _retention(qr, kr, vr, zero_state, batch=bp)
        xp, yp = _merge_ffn(xp, o_sb, o_ret, gate, m[2], m[4], m[3], m[5], g_sb, g_ret, g_ffn,
                            g_fin, w_out_b, w_fi_b, w_fo_b, batch=bp, final=final)
        rp.append(st)

        m = [mods_s[l, :, j].reshape(1, n_s, D_MODEL) for j in range(N_MOD)]
        q, k, v, qr, kr, vr, gate = _proj(
            xs, m[1], m[0], g_mix, cos_s, sin_s, w_in_b, batch=bs, prompt=False)
        o_sb = _sb_sample(q, k, v, cache_k, cache_v, l, batch=bs)
        o_ret, st = _retention(qr, kr, vr, state_ret[l], batch=bs)
        xs, ys = _merge_ffn(xs, o_sb, o_ret, gate, m[2], m[4], m[3], m[5], g_sb, g_ret, g_ffn,
                            g_fin, w_out_b, w_fi_b, w_fo_b, batch=bs, final=final)
        ks.append(k)
        vs.append(v)
        rs.append(st)

    def heads(a, b, t):
        return a.reshape(depth, b, t, SB_HEADS, SB_HEAD_DIM)

    return (yp.reshape(bp, sp, D_MODEL), ys.reshape(bs, ts, D_MODEL),
            heads(kv_p[0], bp, sp), heads(kv_p[1], bp, sp), jnp.stack(rp),
            heads(jnp.stack(ks), bs, ts), heads(jnp.stack(vs), bs, ts), jnp.stack(rs))
```

```python
import functools
import math

import jax
import jax.numpy as jnp
from jax import lax
from jax.experimental import pallas as pl
from jax.experimental.pallas import tpu as pltpu

F32 = jnp.float32
BF16 = jnp.bfloat16

D_MODEL = 1024
SB_WIDTH = 512
RET_WIDTH = 512
SB_HEAD_DIM = 64
SB_HEADS = SB_WIDTH // SB_HEAD_DIM
RET_HEAD_DIM = 128
RET_HEADS = RET_WIDTH // RET_HEAD_DIM
IN_WIDTH = 3 * SB_WIDTH + 4 * RET_WIDTH
D_FF = 2816
N_MOD = 6
ROPE_BASE = 10000.0
EPS = 1e-6
LOG2E = 1.4426950408889634
LN2 = 0.6931471805599453

LANES = 128
MXU_DIM = 256
VMEM_LIMIT_BYTES = 56 * 1024 * 1024

TOKEN_TILE = 512
SB_TILE = MXU_DIM
SB_UNROLL = 3
RET_CHUNK = MXU_DIM
FF_CHUNK = MXU_DIM
SOFTPLUS_CLAMP = 30.0
SHIFT_OUT = 1e30
UNDERFLOW_LOG2 = 160.0


def _dot(a, b):
    return jnp.dot(a, b, preferred_element_type=F32)


def _dot_nt(a, b):
    return lax.dot_general(a, b, (((1,), (1,)), ((), ())), preferred_element_type=F32)


def _dot_tn(a, b):
    return lax.dot_general(a, b, (((0,), (0,)), ((), ())), preferred_element_type=F32)


def _split_bf16(x):
    hi = x.astype(BF16)
    lo = (x - hi.astype(F32)).astype(BF16)
    return hi, lo


def _rms(x, g):
    return x * lax.rsqrt(jnp.mean(x * x, axis=-1, keepdims=True) + EPS) * g


def _softplus2(z2):
    e = jnp.exp2(jnp.minimum(z2, SOFTPLUS_CLAMP))
    return jnp.maximum(z2, jnp.log(1.0 + e) * LOG2E)


def _params(semantics, flags=None):
    return pltpu.CompilerParams(dimension_semantics=semantics, vmem_limit_bytes=VMEM_LIMIT_BYTES,
                                flags=flags)


def _resident(shape):
    zeros = (0,) * len(shape)
    return pl.BlockSpec(shape, lambda *_: zeros, pipeline_mode=pl.Buffered(1))


def _mod_kernel(c_ref, w_ref, b_ref, o_ref):
    c = c_ref[...]
    a = c * jax.nn.sigmoid(c)
    a_hi, a_lo = _split_bf16(a)
    w_hi, w_lo = _split_bf16(w_ref[0])
    o_ref[0] = _dot(a_hi, w_hi) + _dot(a_lo, w_hi) + _dot(a_hi, w_lo) + b_ref[0]


def _modulation(c, w_ada, b_ada):
    depth = w_ada.shape[0]
    rows = c.shape[0]
    return pl.pallas_call(
        _mod_kernel,
        grid=(depth, N_MOD),
        in_specs=[
            pl.BlockSpec((rows, D_MODEL), lambda l, j: (0, 0)),
            pl.BlockSpec((1, D_MODEL, D_MODEL), lambda l, j: (l, 0, j)),
            pl.BlockSpec((1, 1, D_MODEL), lambda l, j: (l, 0, j)),
        ],
        out_specs=pl.BlockSpec((1, rows, D_MODEL), lambda l, j: (l, 0, j)),
        out_shape=jax.ShapeDtypeStruct((depth, rows, N_MOD * D_MODEL), F32),
        compiler_params=_params(("arbitrary", "arbitrary")),
        name="modulation",
    )(c, w_ada, b_ada.reshape(depth, 1, N_MOD * D_MODEL))


def _rope_kernel(inv_ref, cos_ref, sin_ref, *, rows, pos0):
    row = lax.broadcasted_iota(jnp.int32, (rows, LANES), 0) + (pl.program_id(0) * rows + pos0)
    lane = lax.broadcasted_iota(jnp.int32, (rows, LANES), 1)
    ang = row.astype(F32) * inv_ref[...]
    sin = jnp.sin(ang)
    cos_ref[...] = jnp.cos(ang)
    sin_ref[...] = jnp.where(lane < RET_HEAD_DIM // 2, -sin, sin)


def _rope_tables(n, pos0):
    half = RET_HEAD_DIM // 2
    inv = ROPE_BASE ** (-jnp.arange(half, dtype=F32) / half)
    inv2 = jnp.concatenate([inv, inv]).reshape(1, LANES)
    rows = min(n, 2048)
    assert n % rows == 0
    return pl.pallas_call(
        functools.partial(_rope_kernel, rows=rows, pos0=pos0),
        grid=(n // rows,),
        in_specs=[pl.BlockSpec((1, LANES), lambda i: (0, 0))],
        out_specs=[pl.BlockSpec((rows, LANES), lambda i: (i, 0))] * 2,
        out_shape=[jax.ShapeDtypeStruct((n, LANES), F32)] * 2,
        compiler_params=_params(("arbitrary",)),
        name="rope_tables",
    )(inv2)


def _proj_kernel(x_ref, sc_ref, sh_ref, g_ref, cos_ref, sin_ref, w_ref, *rest, tm, kb, prompt):
    if prompt:
        wkvt_ref, _, _, q_ref, kt_ref, kb_ref, vtf_ref, vt_ref, qr_ref, kr_ref, vr_ref, gate_ref = rest
    else:
        q_ref, k_ref, v_ref, qr_ref, kr_ref, vr_ref, gate_ref = rest
    h = _rms(x_ref[...], g_ref[...]) * (1.0 + sc_ref[0]) + sh_ref[0]
    hb = h.astype(BF16)

    def seg(a, b):
        return _dot(hb, w_ref[:, a:b])

    q_ref[...] = (seg(0, SB_WIDTH) * (SB_HEAD_DIM ** -0.5 * LOG2E)).astype(BF16)
    k = seg(SB_WIDTH, 2 * SB_WIDTH)
    if prompt:
        kb_ref[...] = k.astype(BF16)
        kvt = _dot_nt(wkvt_ref[...], hb)
        kt_ref[0, 0] = kvt[:SB_WIDTH].reshape(SB_HEADS, SB_HEAD_DIM, tm)
        vtf_ref[0, 0] = kvt[SB_WIDTH:].reshape(SB_HEADS, SB_HEAD_DIM, tm)
        vt = kvt[SB_WIDTH:].astype(BF16)
        for hp in range(SB_WIDTH // LANES):
            for j in range(tm // kb):
                vt_ref[0, hp, j] = vt[hp * LANES:(hp + 1) * LANES, j * kb:(j + 1) * kb]
    else:
        k_ref[...] = k
        v_ref[...] = seg(2 * SB_WIDTH, 3 * SB_WIDTH)
    cos = cos_ref[...]
    sin = sin_ref[...]
    base = 3 * SB_WIDTH
    qr = seg(base, base + RET_WIDTH)
    kr = seg(base + RET_WIDTH, base + 2 * RET_WIDTH)
    for hh in range(RET_HEADS):
        sl = slice(hh * RET_HEAD_DIM, (hh + 1) * RET_HEAD_DIM)
        qh = qr[:, sl]
        kh = kr[:, sl]
        qr_ref[:, sl] = (qh * cos + pltpu.roll(qh, RET_HEAD_DIM // 2, 1) * sin).astype(BF16)
        kr_ref[:, sl] = ((kh * cos + pltpu.roll(kh, RET_HEAD_DIM // 2, 1) * sin)
                         * RET_HEAD_DIM ** -0.5).astype(BF16)
    vr_ref[...] = seg(base + 2 * RET_WIDTH, base + 3 * RET_WIDTH).astype(BF16)
    gate_ref[...] = seg(base + 3 * RET_WIDTH, base + 4 * RET_WIDTH)


def _proj(x, sc, sh, g, cos, sin, w_in_b, wkvt_b=None, kv_out=None, layer=0, *, batch, prompt):
    n = x.shape[0]
    tm = min(TOKEN_TILE, n)
    tiles = n // tm
    per_seq = max(tiles // batch, 1)
    pos_tiles = cos.shape[0] // tm
    kb = SB_TILE
    mrows = sc.shape[1]

    def row(i):
        return (i, 0)

    def mod(i):
        return (i // per_seq if mrows == 1 else i, 0, 0)

    def pos(i):
        return (i % pos_tiles, 0)

    in_specs = [
        pl.BlockSpec((tm, D_MODEL), row),
        pl.BlockSpec((1, mrows, D_MODEL), mod),
        pl.BlockSpec((1, mrows, D_MODEL), mod),
        _resident((1, D_MODEL)),
        pl.BlockSpec((tm, LANES), pos),
        pl.BlockSpec((tm, LANES), pos),
        _resident((D_MODEL, IN_WIDTH)),
    ]
    args = [x, sc, sh, g, cos, sin, w_in_b]
    wide_b = jax.ShapeDtypeStruct((n, SB_WIDTH), BF16)
    wide_f = jax.ShapeDtypeStruct((n, SB_WIDTH), F32)
    wide = pl.BlockSpec((tm, SB_WIDTH), row)
    if prompt:
        in_specs += [_resident((2 * SB_WIDTH, D_MODEL)), pl.BlockSpec(memory_space=pl.ANY),
                     pl.BlockSpec(memory_space=pl.ANY)]
        aliases = {len(args) + 1: 1, len(args) + 2: 3}
        args += [wkvt_b, *kv_out]
        nkb = n // batch // kb
        hps = SB_WIDTH // LANES
        head_f = jax.ShapeDtypeStruct(kv_out[0].shape, F32)
        head = pl.BlockSpec((1, 1, SB_HEADS, SB_HEAD_DIM, tm),
                            lambda i: (layer, i // per_seq, 0, 0, i % per_seq))
        out_shape = [wide_b, head_f, wide_b, head_f,
                     jax.ShapeDtypeStruct((batch, hps, nkb, LANES, kb), BF16),
                     wide_b, wide_b, wide_b, wide_f]
        out_specs = [wide, head, wide, head,
                     pl.BlockSpec((1, hps, tm // kb, LANES, kb),
                                  lambda i: (i // per_seq, 0, i % per_seq, 0, 0)),
                     wide, wide, wide, wide]
    else:
        aliases = {}
        out_shape = [wide_b, wide_f, wide_f, wide_b, wide_b, wide_b, wide_f]
        out_specs = [wide] * 7
    return pl.pallas_call(
        functools.partial(_proj_kernel, tm=tm, kb=kb, prompt=prompt),
        grid=(tiles,),
        in_specs=in_specs,
        out_specs=out_specs,
        out_shape=out_shape,
        input_output_aliases=aliases,
        compiler_params=_params(("arbitrary",)),
        name="proj_prompt" if prompt else "proj_sample",
    )(*args)


def _sb_prompt_kernel(q_ref, k_ref, vt_ref, o_ref, acc_ref, zt_ref, s_ref, w_ref, *, t, nq):
    i = pl.program_id(2)
    hd = SB_HEAD_DIM
    heads = range(2)
    q = q_ref[0]
    lane = lax.broadcasted_iota(jnp.int32, (t, LANES), 1)
    qm = (jnp.where(lane < hd, q, jnp.zeros_like(q)), jnp.where(lane >= hd, q, jnp.zeros_like(q)))
    kk = lax.broadcasted_iota(jnp.int32, (t, t), 0)
    qq = lax.broadcasted_iota(jnp.int32, (t, t), 1)
    tri = (qq >= kk).astype(BF16)
    valid = kk < qq

    def scores(j):
        kblk = k_ref[0, pl.ds(pl.multiple_of(j * t, t), t), :]
        return [_dot_nt(kblk, qm[h]) for h in heads]

    def survive(zt, s, cum, shift):
        r = _dot(tri, s)
        return jnp.exp2(jnp.minimum(zt - r, 0.0) + (cum + shift)), cum - r[0:1, :]

    def gather(j, ws):
        vblk = vt_ref[0, 0, j]
        return [_dot(vblk[h * hd:(h + 1) * hd, :], ws[h]) for h in heads]

    zts = scores(i)
    zero = jnp.zeros((1, t), F32)
    ss = [jnp.where(valid, _softplus2(zts[h]), 0.0).astype(BF16) for h in heads]
    ws, cums = zip(*[survive(zts[h], ss[h], zero, 0.0) for h in heads])
    parts = gather(i, [jnp.where(valid, ws[h], 0.0).astype(BF16) for h in heads])
    for h in heads:
        acc_ref[h * hd:(h + 1) * hd, :] = parts[h]

    def block_of(m):
        return jnp.clip(i - 1 - m, 0, nq - 1)

    first = scores(block_of(0))
    for h in heads:
        zt_ref[0, h] = first[h]
        zt_ref[2, h] = jnp.zeros((t, t), F32)
        s_ref[2, h] = jnp.zeros((t, t), BF16)
        w_ref[1, h] = jnp.zeros((t, t), BF16)

    def trip(n, r, cums):
        nxt_slot, cur, old, done = (r + 1) % 3, r, (r + 2) % 3, (r + 1) % 3
        shift = jnp.where(jnp.logical_and(n >= 1, n <= i), 0.0, -SHIFT_OUT)
        new_ws, new_cums = zip(*[survive(zt_ref[old, h], s_ref[old, h], cums[h], shift)
                                 for h in heads])
        nxt = scores(block_of(n + 1))
        parts = gather(block_of(n - 2), [w_ref[done, h] for h in heads])
        for h in heads:
            s_ref[cur, h] = _softplus2(zt_ref[cur, h]).astype(BF16)
        for h in heads:
            acc_ref[h * hd:(h + 1) * hd, :] += parts[h]
            w_ref[old, h] = new_ws[h].astype(BF16)
            zt_ref[nxt_slot, h] = nxt[h]
        return tuple(new_cums)

    rounds = jnp.where(i > 0, (i + 1 + SB_UNROLL) // SB_UNROLL, 0)

    def more(state):
        p, faded, _ = state
        return jnp.logical_and(p < rounds, jnp.logical_not(faded))

    def body(state):
        p, faded, cums = state
        for r in range(SB_UNROLL):
            cums = trip(SB_UNROLL * p + r, r % 3, cums)
            if r == SB_UNROLL - 2:
                faded = jnp.max(jnp.maximum(cums[0], cums[1])) < -UNDERFLOW_LOG2
        return p + 1, faded, cums

    lax.while_loop(more, body, (jnp.int32(0), jnp.bool_(False), tuple(cums)))
    o_ref[0] = acc_ref[...].T


def _sb_prompt(q, kb16, vt, *, batch):
    n = q.shape[0]
    s = n // batch
    t = SB_TILE
    nq = s // t
    hps = SB_WIDTH // LANES
    q3 = q.reshape(batch, s, SB_WIDTH)
    k3 = kb16.reshape(batch, s, SB_WIDTH)
    out = pl.pallas_call(
        functools.partial(_sb_prompt_kernel, t=t, nq=nq),
        grid=(batch, hps, nq),
        in_specs=[
            pl.BlockSpec((1, t, LANES), lambda b, hp, i: (b, i, hp)),
            pl.BlockSpec((1, s, LANES), lambda b, hp, i: (b, 0, hp)),
            pl.BlockSpec((1, 1, nq, LANES, t), lambda b, hp, i: (b, hp, 0, 0, 0)),
        ],
        out_specs=pl.BlockSpec((1, t, LANES), lambda b, hp, i: (b, i, hp)),
        out_shape=jax.ShapeDtypeStruct((batch, s, SB_WIDTH), F32),
        scratch_shapes=[pltpu.VMEM((LANES, t), F32), pltpu.VMEM((3, 2, t, t), F32),
                        pltpu.VMEM((3, 2, t, t), BF16), pltpu.VMEM((3, 2, t, t), BF16)],
        compiler_params=_params(("arbitrary", "arbitrary", "arbitrary")),
        name="sb_prompt",
    )(q3, k3, vt)
    return out.reshape(n, SB_WIDTH)


def _sb_sample_kernel(q_ref, kn_ref, vn_ref, ck_ref, cv_ref, o_ref, acc_ref, cum_ref, *, tq, kb, nb):
    j = pl.program_id(1)
    hd = SB_HEAD_DIM
    heads = range(SB_HEADS)
    cols = [slice(h * hd, (h + 1) * hd) for h in heads]
    qs = [q_ref[:, c] for c in cols]

    def cumsum_right(s, m):
        s_hi, s_lo = _split_bf16(s)
        return _dot(s_hi, m) + _dot(s_lo, m)

    def sweep(zs, vts, m, cums, mask=None):
        ss = [_softplus2(z) for z in zs]
        if mask is not None:
            ss = [jnp.where(mask, s, 0.0) for s in ss]
        rs = [cumsum_right(s, m) for s in ss]
        ws = [jnp.exp2(zs[h] - rs[h] + cums[h]) for h in heads]
        if mask is not None:
            ws = [jnp.where(mask, w, 0.0) for w in ws]
        outs = [_dot_nt(ws[h].astype(BF16), vts[h]) for h in heads]
        return outs, [cums[h] - rs[h][:, 0:1] for h in heads]

    @pl.when(j == 0)
    def _():
        qi = lax.broadcasted_iota(jnp.int32, (tq, tq), 0)
        ki = lax.broadcasted_iota(jnp.int32, (tq, tq), 1)
        valid = ki < qi
        tri_new = (qi >= ki).astype(BF16)
        zs = [_dot_nt(qs[h], kn_ref[:, cols[h]].astype(BF16)) for h in heads]
        vts = [vn_ref[:, cols[h]].astype(BF16).T for h in heads]
        outs, cums = sweep(zs, vts, tri_new, [jnp.zeros((tq, 1), F32)] * SB_HEADS, valid)
        for h in heads:
            acc_ref[h] = outs[h]
            cum_ref[h] = cums[h]

    a = lax.broadcasted_iota(jnp.int32, (kb, kb), 0)
    b = lax.broadcasted_iota(jnp.int32, (kb, kb), 1)
    tri = (a >= b).astype(BF16)
    zs = [_dot(qs[h], ck_ref[0, 0, h].astype(BF16)) for h in heads]
    vts = [cv_ref[0, 0, h].astype(BF16) for h in heads]
    outs, cums = sweep(zs, vts, tri, [cum_ref[h] for h in heads])
    for h in heads:
        acc_ref[h] += outs[h]
        cum_ref[h] = cums[h]

    @pl.when(j == nb - 1)
    def _():
        for h in heads:
            o_ref[:, cols[h]] = acc_ref[h]


def _sb_sample(q, k_new, v_new, cache_kt, cache_vt, layer, *, batch):
    n = q.shape[0]
    tq = n // batch
    past = cache_kt.shape[-1]
    kb = min(MXU_DIM, past)
    nb = past // kb
    assert past == nb * kb
    row = pl.BlockSpec((tq, SB_WIDTH), lambda b, j: (b, 0))
    cache = pl.BlockSpec((1, 1, SB_HEADS, SB_HEAD_DIM, kb), lambda b, j: (layer, b, 0, 0, nb - 1 - j))
    return pl.pallas_call(
        functools.partial(_sb_sample_kernel, tq=tq, kb=kb, nb=nb),
        grid=(batch, nb),
        in_specs=[row, row, row, cache, cache],
        out_specs=row,
        out_shape=jax.ShapeDtypeStruct((n, SB_WIDTH), F32),
        scratch_shapes=[pltpu.VMEM((SB_HEADS, tq, SB_HEAD_DIM), F32),
                        pltpu.VMEM((SB_HEADS, tq, 1), F32)],
        compiler_params=_params(("arbitrary", "arbitrary")),
        name="sb_sample",
    )(q, k_new, v_new, cache_kt, cache_vt)


def _ret_log_gamma(h):
    return math.log1p(-2.0 ** (-5 - h))


def _ret_kernel(q_ref, k_ref, v_ref, s0_ref, o_ref, sout_ref, state, dec, qdec, kdec, *, c, nchunks):
    seq = pl.program_id(0)
    ci = pl.program_id(1)

    @pl.when(jnp.logical_and(seq == 0, ci == 0))
    def _():
        li = lax.broadcasted_iota(jnp.int32, (c, c), 0)
        mi = lax.broadcasted_iota(jnp.int32, (c, c), 1)
        diff = (li - mi).astype(F32)
        pos = lax.broadcasted_iota(jnp.int32, (c, RET_HEAD_DIM), 0).astype(F32)
        for h in range(RET_HEADS):
            lg = _ret_log_gamma(h)
            dec[h] = jnp.where(diff >= 0.0, jnp.exp(jnp.maximum(diff, 0.0) * lg), 0.0)
            qdec[h] = jnp.exp((pos + 1.0) * lg)
            kdec[h] = jnp.exp((c - 1.0 - pos) * lg)

    @pl.when(ci == 0)
    def _():
        state[...] = s0_ref[0]

    for h in range(RET_HEADS):
        sl = slice(h * RET_HEAD_DIM, (h + 1) * RET_HEAD_DIM)
        qh = q_ref[:, sl]
        kh = k_ref[:, sl]
        vh = v_ref[:, sl]
        st = state[h]
        scores = _dot_nt(qh, kh) * dec[h]
        inner = _dot(scores.astype(BF16), vh)
        cross = _dot(qh, st.astype(BF16)) * qdec[h]
        o_ref[:, sl] = inner + cross
        kd = (kh.astype(F32) * kdec[h]).astype(BF16)
        state[h] = math.exp(c * _ret_log_gamma(h)) * st + _dot_tn(kd, vh)

    @pl.when(ci == nchunks - 1)
    def _():
        sout_ref[0] = state[...]


def _retention(q, k, v, state0, *, batch):
    n = q.shape[0]
    t = n // batch
    c = min(RET_CHUNK, t)
    nchunks = t // c
    row = pl.BlockSpec((c, RET_WIDTH), lambda b, i: (b * nchunks + i, 0))
    st = pl.BlockSpec((1, RET_HEADS, RET_HEAD_DIM, RET_HEAD_DIM), lambda b, i: (b, 0, 0, 0))
    return pl.pallas_call(
        functools.partial(_ret_kernel, c=c, nchunks=nchunks),
        grid=(batch, nchunks),
        in_specs=[row, row, row, st],
        out_specs=[row, st],
        out_shape=[jax.ShapeDtypeStruct((n, RET_WIDTH), F32),
                   jax.ShapeDtypeStruct((batch, RET_HEADS, RET_HEAD_DIM, RET_HEAD_DIM), F32)],
        scratch_shapes=[pltpu.VMEM((RET_HEADS, RET_HEAD_DIM, RET_HEAD_DIM), F32),
                        pltpu.VMEM((RET_HEADS, c, c), F32),
                        pltpu.VMEM((RET_HEADS, c, RET_HEAD_DIM), F32),
                        pltpu.VMEM((RET_HEADS, c, RET_HEAD_DIM), F32)],
        compiler_params=_params(("arbitrary", "arbitrary")),
        name="retention",
    )(q, k, v, state0)


def _merge_ffn_kernel(x_ref, osb_ref, oret_ref, gate_ref, gt1_ref, sc2_ref, sh2_ref, gt2_ref,
                      gsb_ref, gret_ref, gffn_ref, gfin_ref, wout_ref, wfi_ref, wfo_ref,
                      *outs, final):
    xo_ref = outs[0]
    a = _rms(osb_ref[...], gsb_ref[...]).astype(BF16)
    mix = _dot(a, wout_ref[0:SB_WIDTH, :])
    gate = gate_ref[...]
    gate = gate * jax.nn.sigmoid(gate)
    for h in range(RET_HEADS):
        sl = slice(h * RET_HEAD_DIM, (h + 1) * RET_HEAD_DIM)
        r = (_rms(oret_ref[:, sl], gret_ref[:, sl]) * gate[:, sl]).astype(BF16)
        mix += _dot(r, wout_ref[SB_WIDTH + h * RET_HEAD_DIM:SB_WIDTH + (h + 1) * RET_HEAD_DIM, :])
    x1 = x_ref[...] + gt1_ref[0] * mix
    h2 = (_rms(x1, gffn_ref[...]) * (1.0 + sc2_ref[0]) + sh2_ref[0]).astype(BF16)
    ff = jnp.zeros_like(x1)
    for j in range(D_FF // FF_CHUNK):
        cols = slice(j * FF_CHUNK, (j + 1) * FF_CHUNK)
        ucols = slice(D_FF + j * FF_CHUNK, D_FF + (j + 1) * FF_CHUNK)
        g = _dot(h2, wfi_ref[:, cols])
        u = _dot(h2, wfi_ref[:, ucols])
        act = (g * jax.nn.sigmoid(g) * u).astype(BF16)
        ff += _dot(act, wfo_ref[cols, :])
    x2 = x1 + gt2_ref[0] * ff
    xo_ref[...] = x2
    if final:
        outs[1][...] = _rms(x2, gfin_ref[...])


def _merge_ffn(x, o_sb, o_ret, gate, gt1, sc2, sh2, gt2, g_sb, g_ret, g_ffn, g_fin,
               w_out_b, w_fi_b, w_fo_b, *, batch, final):
    n = x.shape[0]
    tm = min(TOKEN_TILE, n)
    tiles = n // tm
    per_seq = max(tiles // batch, 1)
    mrows = gt1.shape[1]

    def row(i):
        return (i, 0)

    def mod(i):
        return (i // per_seq if mrows == 1 else i, 0, 0)

    full = pl.BlockSpec((tm, D_MODEL), row)
    half = pl.BlockSpec((tm, SB_WIDTH), row)
    mspec = pl.BlockSpec((1, mrows, D_MODEL), mod)
    n_out = 2 if final else 1
    res = pl.pallas_call(
        functools.partial(_merge_ffn_kernel, final=final),
        grid=(tiles,),
        in_specs=[full, half, half, half, mspec, mspec, mspec, mspec,
                  _resident((1, SB_WIDTH)), _resident((1, RET_WIDTH)),
                  _resident((1, D_MODEL)), _resident((1, D_MODEL)),
                  _resident((D_MODEL, D_MODEL)), _resident((D_MODEL, 2 * D_FF)),
                  _resident((D_FF, D_MODEL))],
        out_specs=[full] * n_out,
        out_shape=[jax.ShapeDtypeStruct((n, D_MODEL), F32)] * n_out,
        compiler_params=_params(("arbitrary",)),
        name="merge_ffn",
    )(x, o_sb, o_ret, gate, gt1, sc2, sh2, gt2, g_sb, g_ret, g_ffn, g_fin, w_out_b, w_fi_b, w_fo_b)
    return res if final else (res[0], None)


def kernel(x_prompt, x_sample, cache_sb_k, cache_sb_v, state_ret, c_prompt, c_sample,
           g_norm_mix, g_norm_ffn, w_ada, b_ada, w_in, g_sb_out, g_ret_out, w_out,
           w_ff_in, w_ff_out, g_final):
    depth = w_in.shape[0]
    bp, sp, _ = x_prompt.shape
    bs, ts, _ = x_sample.shape
    past = cache_sb_k.shape[2]
    n_p = bp * sp
    n_s = bs * ts

    c_all = jnp.concatenate([c_prompt, c_sample], axis=0)
    pad = (-c_all.shape[0]) % 16
    mods = _modulation(jnp.pad(c_all, ((0, pad), (0, 0))), w_ada, b_ada)
    mods = mods.reshape(depth, -1, N_MOD, D_MODEL)
    mods_p = mods[:, :bp]
    mods_s = jnp.repeat(mods[:, bp:bp + bs], ts, axis=1)

    cos_p, sin_p = _rope_tables(sp, 0)
    cos_s, sin_s = _rope_tables(ts, past)
    cos_s = jnp.tile(cos_s, (bs, 1))
    sin_s = jnp.tile(sin_s, (bs, 1))

    cache_k = jnp.transpose(cache_sb_k, (0, 1, 3, 4, 2))
    cache_v = jnp.transpose(cache_sb_v, (0, 1, 3, 4, 2))
    zero_state = jnp.zeros((bp, RET_HEADS, RET_HEAD_DIM, RET_HEAD_DIM), F32)

    xp = x_prompt.reshape(n_p, D_MODEL)
    xs = x_sample.reshape(n_s, D_MODEL)
    kv_p = (jnp.zeros((depth, bp, SB_HEADS, SB_HEAD_DIM, sp), F32),) * 2
    rp, ks, vs, rs = [], [], [], []
    yp = ys = None
    for l in range(depth):
        final = l == depth - 1
        w_in_b = w_in[l].astype(BF16)
        wkvt_b = w_in[l][:, SB_WIDTH:3 * SB_WIDTH].T.astype(BF16)
        w_out_b = w_out[l].astype(BF16)
        w_fi_b = w_ff_in[l].astype(BF16)
        w_fo_b = w_ff_out[l].astype(BF16)
        g_mix = g_norm_mix[l].reshape(1, D_MODEL)
        g_ffn = g_norm_ffn[l].reshape(1, D_MODEL)
        g_sb = g_sb_out[l].reshape(1, SB_WIDTH)
        g_ret = g_ret_out[l].reshape(1, RET_WIDTH)
        g_fin = g_final.reshape(1, D_MODEL)

        m = [mods_p[l, :, j].reshape(bp, 1, D_MODEL) for j in range(N_MOD)]
        q, k, k16, v, vt, qr, kr, vr, gate = _proj(
            xp, m[1], m[0], g_mix, cos_p, sin_p, w_in_b, wkvt_b, kv_p, l, batch=bp, prompt=True)
        kv_p = (k, v)
        o_sb = _sb_prompt(q, k16, vt, batch=bp)
        o_ret, st = _retention(qr, kr, vr, zero_state, batch=bp)
        xp, yp = _merge_ffn(xp, o_sb, o_ret, gate, m[2], m[4], m[3], m[5], g_sb, g_ret, g_ffn,
                            g_fin, w_out_b, w_fi_b, w_fo_b, batch=bp, final=final)
        rp.append(st)

        m = [mods_s[l, :, j].reshape(1, n_s, D_MODEL) for j in range(N_MOD)]
        q, k, v, qr, kr, vr, gate = _proj(
            xs, m[1], m[0], g_mix, cos_s, sin_s, w_in_b, batch=bs, prompt=False)
        o_sb = _sb_sample(q, k, v, cache_k, cache_v, l, batch=bs)
        o_ret, st = _retention(qr, kr, vr, state_ret[l], batch=bs)
        xs, ys = _merge_ffn(xs, o_sb, o_ret, gate, m[2], m[4], m[3], m[5], g_sb, g_ret, g_ffn,
                            g_fin, w_out_b, w_fi_b, w_fo_b, batch=bs, final=final)
        ks.append(k)
        vs.append(v)
        rs.append(st)

    def heads(a, b, t):
        return a.reshape(depth, b, t, SB_HEADS, SB_HEAD_DIM)

    def rows(a):
        return jnp.transpose(a, (0, 1, 4, 2, 3))

    return (yp.reshape(bp, sp, D_MODEL), ys.reshape(bs, ts, D_MODEL),
            rows(kv_p[0]), rows(kv_p[1]), jnp.stack(rp),
            heads(jnp.stack(ks), bs, ts), heads(jnp.stack(vs), bs, ts), jnp.stack(rs))
```

```python
import functools
import math

import jax
import jax.numpy as jnp
from jax import lax
from jax.experimental import pallas as pl
from jax.experimental.pallas import tpu as pltpu

F32 = jnp.float32
BF16 = jnp.bfloat16

D_MODEL = 1024
SB_WIDTH = 512
RET_WIDTH = 512
SB_HEAD_DIM = 64
SB_HEADS = SB_WIDTH // SB_HEAD_DIM
RET_HEAD_DIM = 128
RET_HEADS = RET_WIDTH // RET_HEAD_DIM
IN_WIDTH = 3 * SB_WIDTH + 4 * RET_WIDTH
D_FF = 2816
N_MOD = 6
ROPE_BASE = 10000.0
EPS = 1e-6
LOG2E = 1.4426950408889634
LN2 = 0.6931471805599453

LANES = 128
MXU_DIM = 256
VMEM_LIMIT_BYTES = 56 * 1024 * 1024

TOKEN_TILE = 512
SB_TILE = MXU_DIM
SB_UNROLL = 3
RET_CHUNK = MXU_DIM
FF_CHUNK = MXU_DIM
SOFTPLUS_CLAMP = 30.0
SHIFT_OUT = 1e30
UNDERFLOW_LOG2 = 160.0


def _dot(a, b):
    return jnp.dot(a, b, preferred_element_type=F32)


def _dot_nt(a, b):
    return lax.dot_general(a, b, (((1,), (1,)), ((), ())), preferred_element_type=F32)


def _dot_tn(a, b):
    return lax.dot_general(a, b, (((0,), (0,)), ((), ())), preferred_element_type=F32)


def _split_bf16(x):
    hi = x.astype(BF16)
    lo = (x - hi.astype(F32)).astype(BF16)
    return hi, lo


def _rms(x, g):
    return x * lax.rsqrt(jnp.mean(x * x, axis=-1, keepdims=True) + EPS) * g


def _softplus2(z2):
    e = jnp.exp2(jnp.minimum(z2, SOFTPLUS_CLAMP))
    return jnp.maximum(z2, jnp.log(1.0 + e) * LOG2E)


def _params(semantics, flags=None):
    return pltpu.CompilerParams(dimension_semantics=semantics, vmem_limit_bytes=VMEM_LIMIT_BYTES,
                                flags=flags)


def _resident(shape):
    zeros = (0,) * len(shape)
    return pl.BlockSpec(shape, lambda *_: zeros, pipeline_mode=pl.Buffered(1))


def _mod_kernel(c_ref, w_ref, b_ref, o_ref):
    c = c_ref[...]
    a = c * jax.nn.sigmoid(c)
    a_hi, a_lo = _split_bf16(a)
    w_hi, w_lo = _split_bf16(w_ref[0])
    o_ref[0] = _dot(a_hi, w_hi) + _dot(a_lo, w_hi) + _dot(a_hi, w_lo) + b_ref[0]


def _modulation(c, w_ada, b_ada):
    depth = w_ada.shape[0]
    rows = c.shape[0]
    return pl.pallas_call(
        _mod_kernel,
        grid=(depth, N_MOD),
        in_specs=[
            pl.BlockSpec((rows, D_MODEL), lambda l, j: (0, 0)),
            pl.BlockSpec((1, D_MODEL, D_MODEL), lambda l, j: (l, 0, j)),
            pl.BlockSpec((1, 1, D_MODEL), lambda l, j: (l, 0, j)),
        ],
        out_specs=pl.BlockSpec((1, rows, D_MODEL), lambda l, j: (l, 0, j)),
        out_shape=jax.ShapeDtypeStruct((depth, rows, N_MOD * D_MODEL), F32),
        compiler_params=_params(("arbitrary", "arbitrary")),
        name="modulation",
    )(c, w_ada, b_ada.reshape(depth, 1, N_MOD * D_MODEL))


def _rope_kernel(inv_ref, cos_ref, sin_ref, *, rows, pos0):
    row = lax.broadcasted_iota(jnp.int32, (rows, LANES), 0) + (pl.program_id(0) * rows + pos0)
    lane = lax.broadcasted_iota(jnp.int32, (rows, LANES), 1)
    ang = row.astype(F32) * inv_ref[...]
    sin = jnp.sin(ang)
    cos_ref[...] = jnp.cos(ang)
    sin_ref[...] = jnp.where(lane < RET_HEAD_DIM // 2, -sin, sin)


def _rope_tables(n, pos0):
    half = RET_HEAD_DIM // 2
    inv = ROPE_BASE ** (-jnp.arange(half, dtype=F32) / half)
    inv2 = jnp.concatenate([inv, inv]).reshape(1, LANES)
    rows = min(n, 2048)
    assert n % rows == 0
    return pl.pallas_call(
        functools.partial(_rope_kernel, rows=rows, pos0=pos0),
        grid=(n // rows,),
        in_specs=[pl.BlockSpec((1, LANES), lambda i: (0, 0))],
        out_specs=[pl.BlockSpec((rows, LANES), lambda i: (i, 0))] * 2,
        out_shape=[jax.ShapeDtypeStruct((n, LANES), F32)] * 2,
        compiler_params=_params(("arbitrary",)),
        name="rope_tables",
    )(inv2)


def _proj_kernel(x_ref, sc_ref, sh_ref, g_ref, cos_ref, sin_ref, w_ref, *rest, tm, kb, prompt):
    if prompt:
        wkvt_ref, _, _, q_ref, kt_ref, kb_ref, vtf_ref, vt_ref, qr_ref, kr_ref, vr_ref, gate_ref = rest
    else:
        q_ref, k_ref, v_ref, qr_ref, kr_ref, vr_ref, gate_ref = rest
    h = _rms(x_ref[...], g_ref[...]) * (1.0 + sc_ref[0]) + sh_ref[0]
    hb = h.astype(BF16)

    def seg(a, b):
        return _dot(hb, w_ref[:, a:b])

    q_ref[...] = (seg(0, SB_WIDTH) * (SB_HEAD_DIM ** -0.5 * LOG2E)).astype(BF16)
    k = seg(SB_WIDTH, 2 * SB_WIDTH)
    if prompt:
        kb_ref[...] = k.astype(BF16)
        kvt = _dot_nt(wkvt_ref[...], hb)
        kt_ref[0, 0] = kvt[:SB_WIDTH].reshape(SB_HEADS, SB_HEAD_DIM, tm)
        vtf_ref[0, 0] = kvt[SB_WIDTH:].reshape(SB_HEADS, SB_HEAD_DIM, tm)
        vt = kvt[SB_WIDTH:].astype(BF16)
        for hp in range(SB_WIDTH // LANES):
            for j in range(tm // kb):
                vt_ref[0, hp, j] = vt[hp * LANES:(hp + 1) * LANES, j * kb:(j + 1) * kb]
    else:
        k_ref[...] = k
        v_ref[...] = seg(2 * SB_WIDTH, 3 * SB_WIDTH)
    cos = cos_ref[...]
    sin = sin_ref[...]
    base = 3 * SB_WIDTH
    qr = seg(base, base + RET_WIDTH)
    kr = seg(base + RET_WIDTH, base + 2 * RET_WIDTH)
    for hh in range(RET_HEADS):
        sl = slice(hh * RET_HEAD_DIM, (hh + 1) * RET_HEAD_DIM)
        qh = qr[:, sl]
        kh = kr[:, sl]
        qr_ref[:, sl] = (qh * cos + pltpu.roll(qh, RET_HEAD_DIM // 2, 1) * sin).astype(BF16)
        kr_ref[:, sl] = ((kh * cos + pltpu.roll(kh, RET_HEAD_DIM // 2, 1) * sin)
                         * RET_HEAD_DIM ** -0.5).astype(BF16)
    vr_ref[...] = seg(base + 2 * RET_WIDTH, base + 3 * RET_WIDTH).astype(BF16)
    gate_ref[...] = seg(base + 3 * RET_WIDTH, base + 4 * RET_WIDTH)


def _proj(x, sc, sh, g, cos, sin, w_in_b, wkvt_b=None, kv_out=None, layer=0, *, batch, prompt):
    n = x.shape[0]
    tm = min(TOKEN_TILE, n)
    tiles = n // tm
    per_seq = max(tiles // batch, 1)
    pos_tiles = cos.shape[0] // tm
    kb = SB_TILE
    mrows = sc.shape[1]

    def row(i):
        return (i, 0)

    def mod(i):
        return (i // per_seq if mrows == 1 else i, 0, 0)

    def pos(i):
        return (i % pos_tiles, 0)

    in_specs = [
        pl.BlockSpec((tm, D_MODEL), row),
        pl.BlockSpec((1, mrows, D_MODEL), mod),
        pl.BlockSpec((1, mrows, D_MODEL), mod),
        _resident((1, D_MODEL)),
        pl.BlockSpec((tm, LANES), pos),
        pl.BlockSpec((tm, LANES), pos),
        _resident((D_MODEL, IN_WIDTH)),
    ]
    args = [x, sc, sh, g, cos, sin, w_in_b]
    wide_b = jax.ShapeDtypeStruct((n, SB_WIDTH), BF16)
    wide_f = jax.ShapeDtypeStruct((n, SB_WIDTH), F32)
    wide = pl.BlockSpec((tm, SB_WIDTH), row)
    if prompt:
        in_specs += [_resident((2 * SB_WIDTH, D_MODEL)), pl.BlockSpec(memory_space=pl.ANY),
                     pl.BlockSpec(memory_space=pl.ANY)]
        aliases = {len(args) + 1: 1, len(args) + 2: 3}
        args += [wkvt_b, *kv_out]
        nkb = n // batch // kb
        hps = SB_WIDTH // LANES
        head_f = jax.ShapeDtypeStruct(kv_out[0].shape, F32)
        head = pl.BlockSpec((1, 1, SB_HEADS, SB_HEAD_DIM, tm),
                            lambda i: (layer, i // per_seq, 0, 0, i % per_seq))
        out_shape = [wide_b, head_f, wide_b, head_f,
                     jax.ShapeDtypeStruct((batch, hps, nkb, LANES, kb), BF16),
                     wide_b, wide_b, wide_b, wide_f]
        out_specs = [wide, head, wide, head,
                     pl.BlockSpec((1, hps, tm // kb, LANES, kb),
                                  lambda i: (i // per_seq, 0, i % per_seq, 0, 0)),
                     wide, wide, wide, wide]
    else:
        aliases = {}
        out_shape = [wide_b, wide_f, wide_f, wide_b, wide_b, wide_b, wide_f]
        out_specs = [wide] * 7
    return pl.pallas_call(
        functools.partial(_proj_kernel, tm=tm, kb=kb, prompt=prompt),
        grid=(tiles,),
        in_specs=in_specs,
        out_specs=out_specs,
        out_shape=out_shape,
        input_output_aliases=aliases,
        compiler_params=_params(("arbitrary",)),
        name="proj_prompt" if prompt else "proj_sample",
    )(*args)


def _sb_prompt_kernel(q_ref, k_ref, vt_ref, o_ref, acc_ref, zt_ref, s_ref, w_ref, *, t, nq):
    i = pl.program_id(2)
    hd = SB_HEAD_DIM
    heads = range(2)
    q = q_ref[0]
    lane = lax.broadcasted_iota(jnp.int32, (t, LANES), 1)
    qm = (jnp.where(lane < hd, q, jnp.zeros_like(q)), jnp.where(lane >= hd, q, jnp.zeros_like(q)))
    kk = lax.broadcasted_iota(jnp.int32, (t, t), 0)
    qq = lax.broadcasted_iota(jnp.int32, (t, t), 1)
    tri = (qq >= kk).astype(BF16)
    valid = kk < qq

    def scores(j):
        kblk = k_ref[0, pl.ds(pl.multiple_of(j * t, t), t), :]
        return [_dot_nt(kblk, qm[h]) for h in heads]


    def gather(j, ws):
        vblk = vt_ref[0, 0, j]
        return [_dot(vblk[h * hd:(h + 1) * hd, :], ws[h]) for h in heads]

    def weights(zt, r, carried):
        return jnp.exp2(jnp.minimum(zt - r, 0.0) + carried)

    left = jnp.maximum(i - 1, 0)
    shift_left = jnp.where(i > 0, 0.0, -SHIFT_OUT)
    z_diag = scores(i)
    z_left = scores(left)
    s_diag = [jnp.where(valid, _softplus2(z_diag[h]), 0.0).astype(BF16) for h in heads]
    s_left = [_softplus2(z_left[h]).astype(BF16) for h in heads]
    r_diag = [_dot(tri, s_diag[h]) for h in heads]
    r_left = [_dot(tri, s_left[h]) for h in heads]
    cum_diag = [-r_diag[h][0:1, :] for h in heads]
    w_diag = [jnp.where(valid, weights(z_diag[h], r_diag[h], 0.0), 0.0).astype(BF16) for h in heads]
    w_left = [weights(z_left[h], r_left[h], cum_diag[h] + shift_left).astype(BF16) for h in heads]
    cums = tuple(cum_diag[h] - r_left[h][0:1, :] for h in heads)
    p_diag = gather(i, w_diag)
    p_left = gather(left, w_left)
    for h in heads:
        acc_ref[h * hd:(h + 1) * hd, :] = p_diag[h] + p_left[h]

    def all_faded(cums):
        return jnp.max(jnp.maximum(cums[0], cums[1])) < -UNDERFLOW_LOG2

    def block_of(m):
        return jnp.clip(i - 2 - m, 0, nq - 1)

    rounds = jnp.where(i > 1, (i + SB_UNROLL) // SB_UNROLL, 0)
    faded = all_faded(cums)

    @pl.when(jnp.logical_and(rounds > 0, jnp.logical_not(faded)))
    def _():
        first = scores(block_of(0))
        for h in heads:
            zt_ref[0, h] = first[h]
            zt_ref[2, h] = jnp.zeros((t, t), F32)
            s_ref[2, h] = jnp.zeros((t, t), BF16)
            w_ref[1, h] = jnp.zeros((t, t), BF16)

    def trip(n, r, cums):
        nxt_slot, cur, old, done = (r + 1) % 3, r, (r + 2) % 3, (r + 1) % 3
        shift = jnp.where(jnp.logical_and(n >= 1, n < i), 0.0, -SHIFT_OUT)
        rs = [_dot(tri, s_ref[old, h]) for h in heads]
        new_ws = [weights(zt_ref[old, h], rs[h], cums[h] + shift) for h in heads]
        new_cums = [cums[h] - rs[h][0:1, :] for h in heads]
        nxt = scores(block_of(n + 1))
        parts = gather(block_of(n - 2), [w_ref[done, h] for h in heads])
        for h in heads:
            s_ref[cur, h] = _softplus2(zt_ref[cur, h]).astype(BF16)
        for h in heads:
            acc_ref[h * hd:(h + 1) * hd, :] += parts[h]
            w_ref[old, h] = new_ws[h].astype(BF16)
            zt_ref[nxt_slot, h] = nxt[h]
        return tuple(new_cums)

    def more(state):
        p, faded, _ = state
        return jnp.logical_and(p < rounds, jnp.logical_not(faded))

    def body(state):
        p, faded, cums = state
        for r in range(SB_UNROLL):
            cums = trip(SB_UNROLL * p + r, r % 3, cums)
            if r == SB_UNROLL - 2:
                faded = all_faded(cums)
        return p + 1, faded, cums

    lax.while_loop(more, body, (jnp.int32(0), faded, cums))
    o_ref[0] = acc_ref[...].T


def _sb_prompt(q, kb16, vt, *, batch):
    n = q.shape[0]
    s = n // batch
    t = SB_TILE
    nq = s // t
    hps = SB_WIDTH // LANES
    q3 = q.reshape(batch, s, SB_WIDTH)
    k3 = kb16.reshape(batch, s, SB_WIDTH)
    out = pl.pallas_call(
        functools.partial(_sb_prompt_kernel, t=t, nq=nq),
        grid=(batch, hps, nq),
        in_specs=[
            pl.BlockSpec((1, t, LANES), lambda b, hp, i: (b, i, hp)),
            pl.BlockSpec((1, s, LANES), lambda b, hp, i: (b, 0, hp)),
            pl.BlockSpec((1, 1, nq, LANES, t), lambda b, hp, i: (b, hp, 0, 0, 0)),
        ],
        out_specs=pl.BlockSpec((1, t, LANES), lambda b, hp, i: (b, i, hp)),
        out_shape=jax.ShapeDtypeStruct((batch, s, SB_WIDTH), F32),
        scratch_shapes=[pltpu.VMEM((LANES, t), F32), pltpu.VMEM((3, 2, t, t), F32),
                        pltpu.VMEM((3, 2, t, t), BF16), pltpu.VMEM((3, 2, t, t), BF16)],
        compiler_params=_params(("arbitrary", "arbitrary", "arbitrary")),
        name="sb_prompt",
    )(q3, k3, vt)
    return out.reshape(n, SB_WIDTH)


def _sb_sample_kernel(q_ref, kn_ref, vn_ref, ck_ref, cv_ref, o_ref, acc_ref, cum_ref, *, tq, kb, nb):
    j = pl.program_id(1)
    hd = SB_HEAD_DIM
    heads = range(SB_HEADS)
    cols = [slice(h * hd, (h + 1) * hd) for h in heads]
    qs = [q_ref[:, c] for c in cols]

    def cumsum_right(s, m):
        s_hi, s_lo = _split_bf16(s)
        return _dot(s_hi, m) + _dot(s_lo, m)

    def sweep(zs, vts, m, cums, mask=None):
        ss = [_softplus2(z) for z in zs]
        if mask is not None:
            ss = [jnp.where(mask, s, 0.0) for s in ss]
        rs = [cumsum_right(s, m) for s in ss]
        ws = [jnp.exp2(zs[h] - rs[h] + cums[h]) for h in heads]
        if mask is not None:
            ws = [jnp.where(mask, w, 0.0) for w in ws]
        outs = [_dot_nt(ws[h].astype(BF16), vts[h]) for h in heads]
        return outs, [cums[h] - rs[h][:, 0:1] for h in heads]

    @pl.when(j == 0)
    def _():
        qi = lax.broadcasted_iota(jnp.int32, (tq, tq), 0)
        ki = lax.broadcasted_iota(jnp.int32, (tq, tq), 1)
        valid = ki < qi
        tri_new = (qi >= ki).astype(BF16)
        zs = [_dot_nt(qs[h], kn_ref[:, cols[h]].astype(BF16)) for h in heads]
        vts = [vn_ref[:, cols[h]].astype(BF16).T for h in heads]
        outs, cums = sweep(zs, vts, tri_new, [jnp.zeros((tq, 1), F32)] * SB_HEADS, valid)
        for h in heads:
            acc_ref[h] = outs[h]
            cum_ref[h] = cums[h]

    a = lax.broadcasted_iota(jnp.int32, (kb, kb), 0)
    b = lax.broadcasted_iota(jnp.int32, (kb, kb), 1)
    tri = (a >= b).astype(BF16)
    zs = [_dot(qs[h], ck_ref[0, 0, h].astype(BF16)) for h in heads]
    vts = [cv_ref[0, 0, h].astype(BF16) for h in heads]
    outs, cums = sweep(zs, vts, tri, [cum_ref[h] for h in heads])
    for h in heads:
        acc_ref[h] += outs[h]
        cum_ref[h] = cums[h]

    @pl.when(j == nb - 1)
    def _():
        for h in heads:
            o_ref[:, cols[h]] = acc_ref[h]


def _sb_sample(q, k_new, v_new, cache_kt, cache_vt, layer, *, batch):
    n = q.shape[0]
    tq = n // batch
    past = cache_kt.shape[-1]
    kb = min(MXU_DIM, past)
    nb = past // kb
    assert past == nb * kb
    row = pl.BlockSpec((tq, SB_WIDTH), lambda b, j: (b, 0))
    cache = pl.BlockSpec((1, 1, SB_HEADS, SB_HEAD_DIM, kb), lambda b, j: (layer, b, 0, 0, nb - 1 - j))
    return pl.pallas_call(
        functools.partial(_sb_sample_kernel, tq=tq, kb=kb, nb=nb),
        grid=(batch, nb),
        in_specs=[row, row, row, cache, cache],
        out_specs=row,
        out_shape=jax.ShapeDtypeStruct((n, SB_WIDTH), F32),
        scratch_shapes=[pltpu.VMEM((SB_HEADS, tq, SB_HEAD_DIM), F32),
                        pltpu.VMEM((SB_HEADS, tq, 1), F32)],
        compiler_params=_params(("arbitrary", "arbitrary")),
        name="sb_sample",
    )(q, k_new, v_new, cache_kt, cache_vt)


def _ret_log_gamma(h):
    return math.log1p(-2.0 ** (-5 - h))


def _ret_kernel(q_ref, k_ref, v_ref, s0_ref, o_ref, sout_ref, state, dec, qdec, kdec, *, c, nchunks):
    seq = pl.program_id(0)
    ci = pl.program_id(1)

    @pl.when(jnp.logical_and(seq == 0, ci == 0))
    def _():
        li = lax.broadcasted_iota(jnp.int32, (c, c), 0)
        mi = lax.broadcasted_iota(jnp.int32, (c, c), 1)
        diff = (li - mi).astype(F32)
        pos = lax.broadcasted_iota(jnp.int32, (c, RET_HEAD_DIM), 0).astype(F32)
        for h in range(RET_HEADS):
            lg = _ret_log_gamma(h)
            dec[h] = jnp.where(diff >= 0.0, jnp.exp(jnp.maximum(diff, 0.0) * lg), 0.0)
            qdec[h] = jnp.exp((pos + 1.0) * lg)
            kdec[h] = jnp.exp((c - 1.0 - pos) * lg)

    @pl.when(ci == 0)
    def _():
        state[...] = s0_ref[0]

    heads = range(RET_HEADS)
    cols = [slice(h * RET_HEAD_DIM, (h + 1) * RET_HEAD_DIM) for h in heads]
    qs = [q_ref[:, sl] for sl in cols]
    ks = [k_ref[:, sl] for sl in cols]
    vs = [v_ref[:, sl] for sl in cols]
    sts = [state[h] for h in heads]
    scores = [_dot_nt(qs[h], ks[h]) for h in heads]
    cross = [_dot(qs[h], sts[h].astype(BF16)) for h in heads]
    kds = [(ks[h].astype(F32) * kdec[h]).astype(BF16) for h in heads]
    grown = [_dot_tn(kds[h], vs[h]) for h in heads]
    inner = [_dot((scores[h] * dec[h]).astype(BF16), vs[h]) for h in heads]
    for h in heads:
        o_ref[:, cols[h]] = inner[h] + cross[h] * qdec[h]
        state[h] = math.exp(c * _ret_log_gamma(h)) * sts[h] + grown[h]

    @pl.when(ci == nchunks - 1)
    def _():
        sout_ref[0] = state[...]


def _retention(q, k, v, state0, *, batch):
    n = q.shape[0]
    t = n // batch
    c = min(RET_CHUNK, t)
    nchunks = t // c
    row = pl.BlockSpec((c, RET_WIDTH), lambda b, i: (b * nchunks + i, 0))
    st = pl.BlockSpec((1, RET_HEADS, RET_HEAD_DIM, RET_HEAD_DIM), lambda b, i: (b, 0, 0, 0))
    return pl.pallas_call(
        functools.partial(_ret_kernel, c=c, nchunks=nchunks),
        grid=(batch, nchunks),
        in_specs=[row, row, row, st],
        out_specs=[row, st],
        out_shape=[jax.ShapeDtypeStruct((n, RET_WIDTH), F32),
                   jax.ShapeDtypeStruct((batch, RET_HEADS, RET_HEAD_DIM, RET_HEAD_DIM), F32)],
        scratch_shapes=[pltpu.VMEM((RET_HEADS, RET_HEAD_DIM, RET_HEAD_DIM), F32),
                        pltpu.VMEM((RET_HEADS, c, c), F32),
                        pltpu.VMEM((RET_HEADS, c, RET_HEAD_DIM), F32),
                        pltpu.VMEM((RET_HEADS, c, RET_HEAD_DIM), F32)],
        compiler_params=_params(("arbitrary", "arbitrary")),
        name="retention",
    )(q, k, v, state0)


def _merge_ffn_kernel(x_ref, osb_ref, oret_ref, gate_ref, gt1_ref, sc2_ref, sh2_ref, gt2_ref,
                      gsb_ref, gret_ref, gffn_ref, gfin_ref, wout_ref, wfi_ref, wfo_ref,
                      *outs, final):
    xo_ref = outs[0]
    a = _rms(osb_ref[...], gsb_ref[...]).astype(BF16)
    mix = _dot(a, wout_ref[0:SB_WIDTH, :])
    gate = gate_ref[...]
    gate = gate * jax.nn.sigmoid(gate)
    for h in range(RET_HEADS):
        sl = slice(h * RET_HEAD_DIM, (h + 1) * RET_HEAD_DIM)
        r = (_rms(oret_ref[:, sl], gret_ref[:, sl]) * gate[:, sl]).astype(BF16)
        mix += _dot(r, wout_ref[SB_WIDTH + h * RET_HEAD_DIM:SB_WIDTH + (h + 1) * RET_HEAD_DIM, :])
    x1 = x_ref[...] + gt1_ref[0] * mix
    h2 = (_rms(x1, gffn_ref[...]) * (1.0 + sc2_ref[0]) + sh2_ref[0]).astype(BF16)
    ff = jnp.zeros_like(x1)
    for j in range(D_FF // FF_CHUNK):
        cols = slice(j * FF_CHUNK, (j + 1) * FF_CHUNK)
        ucols = slice(D_FF + j * FF_CHUNK, D_FF + (j + 1) * FF_CHUNK)
        g = _dot(h2, wfi_ref[:, cols])
        u = _dot(h2, wfi_ref[:, ucols])
        act = (g * jax.nn.sigmoid(g) * u).astype(BF16)
        ff += _dot(act, wfo_ref[cols, :])
    x2 = x1 + gt2_ref[0] * ff
    xo_ref[...] = x2
    if final:
        outs[1][...] = _rms(x2, gfin_ref[...])


def _merge_ffn(x, o_sb, o_ret, gate, gt1, sc2, sh2, gt2, g_sb, g_ret, g_ffn, g_fin,
               w_out_b, w_fi_b, w_fo_b, *, batch, final):
    n = x.shape[0]
    tm = min(TOKEN_TILE, n)
    tiles = n // tm
    per_seq = max(tiles // batch, 1)
    mrows = gt1.shape[1]

    def row(i):
        return (i, 0)

    def mod(i):
        return (i // per_seq if mrows == 1 else i, 0, 0)

    full = pl.BlockSpec((tm, D_MODEL), row)
    half = pl.BlockSpec((tm, SB_WIDTH), row)
    mspec = pl.BlockSpec((1, mrows, D_MODEL), mod)
    n_out = 2 if final else 1
    res = pl.pallas_call(
        functools.partial(_merge_ffn_kernel, final=final),
        grid=(tiles,),
        in_specs=[full, half, half, half, mspec, mspec, mspec, mspec,
                  _resident((1, SB_WIDTH)), _resident((1, RET_WIDTH)),
                  _resident((1, D_MODEL)), _resident((1, D_MODEL)),
                  _resident((D_MODEL, D_MODEL)), _resident((D_MODEL, 2 * D_FF)),
                  _resident((D_FF, D_MODEL))],
        out_specs=[full] * n_out,
        out_shape=[jax.ShapeDtypeStruct((n, D_MODEL), F32)] * n_out,
        compiler_params=_params(("arbitrary",)),
        name="merge_ffn",
    )(x, o_sb, o_ret, gate, gt1, sc2, sh2, gt2, g_sb, g_ret, g_ffn, g_fin, w_out_b, w_fi_b, w_fo_b)
    return res if final else (res[0], None)


def kernel(x_prompt, x_sample, cache_sb_k, cache_sb_v, state_ret, c_prompt, c_sample,
           g_norm_mix, g_norm_ffn, w_ada, b_ada, w_in, g_sb_out, g_ret_out, w_out,
           w_ff_in, w_ff_out, g_final):
    depth = w_in.shape[0]
    bp, sp, _ = x_prompt.shape
    bs, ts, _ = x_sample.shape
    past = cache_sb_k.shape[2]
    n_p = bp * sp
    n_s = bs * ts

    c_all = jnp.concatenate([c_prompt, c_sample], axis=0)
    pad = (-c_all.shape[0]) % 16
    mods = _modulation(jnp.pad(c_all, ((0, pad), (0, 0))), w_ada, b_ada)
    mods = mods.reshape(depth, -1, N_MOD, D_MODEL)
    mods_p = mods[:, :bp]
    mods_s = jnp.repeat(mods[:, bp:bp + bs], ts, axis=1)

    cos_p, sin_p = _rope_tables(sp, 0)
    cos_s, sin_s = _rope_tables(ts, past)
    cos_s = jnp.tile(cos_s, (bs, 1))
    sin_s = jnp.tile(sin_s, (bs, 1))

    cache_k = jnp.transpose(cache_sb_k, (0, 1, 3, 4, 2))
    cache_v = jnp.transpose(cache_sb_v, (0, 1, 3, 4, 2))
    zero_state = jnp.zeros((bp, RET_HEADS, RET_HEAD_DIM, RET_HEAD_DIM), F32)

    xp = x_prompt.reshape(n_p, D_MODEL)
    xs = x_sample.reshape(n_s, D_MODEL)
    kv_p = (jnp.zeros((depth, bp, SB_HEADS, SB_HEAD_DIM, sp), F32),) * 2
    rp, ks, vs, rs = [], [], [], []
    yp = ys = None
    for l in range(depth):
        final = l == depth - 1
        w_in_b = w_in[l].astype(BF16)
        wkvt_b = w_in[l][:, SB_WIDTH:3 * SB_WIDTH].T.astype(BF16)
        w_out_b = w_out[l].astype(BF16)
        w_fi_b = w_ff_in[l].astype(BF16)
        w_fo_b = w_ff_out[l].astype(BF16)
        g_mix = g_norm_mix[l].reshape(1, D_MODEL)
        g_ffn = g_norm_ffn[l].reshape(1, D_MODEL)
        g_sb = g_sb_out[l].reshape(1, SB_WIDTH)
        g_ret = g_ret_out[l].reshape(1, RET_WIDTH)
        g_fin = g_final.reshape(1, D_MODEL)

        m = [mods_p[l, :, j].reshape(bp, 1, D_MODEL) for j in range(N_MOD)]
        q, k, k16, v, vt, qr, kr, vr, gate = _proj(
            xp, m[1], m[0], g_mix, cos_p, sin_p, w_in_b, wkvt_b, kv_p, l, batch=bp, prompt=True)
        kv_p = (k, v)
        o_sb = _sb_prompt(q, k16, vt, batch=bp)
        o_ret, st = _retention(qr, kr, vr, zero_state, batch=bp)
        xp, yp = _merge_ffn(xp, o_sb, o_ret, gate, m[2], m[4], m[3], m[5], g_sb, g_ret, g_ffn,
                            g_fin, w_out_b, w_fi_b, w_fo_b, batch=bp, final=final)
        rp.append(st)

        m = [mods_s[l, :, j].reshape(1, n_s, D_MODEL) for j in range(N_MOD)]
        q, k, v, qr, kr, vr, gate = _proj(
            xs, m[1], m[0], g_mix, cos_s, sin_s, w_in_b, batch=bs, prompt=False)
        o_sb = _sb_sample(q, k, v, cache_k, cache_v, l, batch=bs)
        o_ret, st = _retention(qr, kr, vr, state_ret[l], batch=bs)
        xs, ys = _merge_ffn(xs, o_sb, o_ret, gate, m[2], m[4], m[3], m[5], g_sb, g_ret, g_ffn,
                            g_fin, w_out_b, w_fi_b, w_fo_b, batch=bs, final=final)
        ks.append(k)
        vs.append(v)
        rs.append(st)

    def heads(a, b, t):
        return a.reshape(depth, b, t, SB_HEADS, SB_HEAD_DIM)

    def rows(a):
        return jnp.transpose(a, (0, 1, 4, 2, 3))

    return (yp.reshape(bp, sp, D_MODEL), ys.reshape(bs, ts, D_MODEL),
            rows(kv_p[0]), rows(kv_p[1]), jnp.stack(rp),
            heads(jnp.stack(ks), bs, ts), heads(jnp.stack(vs), bs, ts), jnp.stack(rs))
```

```python
import functools
import math

import jax
import jax.numpy as jnp
from jax import lax
from jax.experimental import pallas as pl
from jax.experimental.pallas import tpu as pltpu

F32 = jnp.float32
BF16 = jnp.bfloat16

D_MODEL = 1024
SB_WIDTH = 512
RET_WIDTH = 512
SB_HEAD_DIM = 64
SB_HEADS = SB_WIDTH // SB_HEAD_DIM
RET_HEAD_DIM = 128
RET_HEADS = RET_WIDTH // RET_HEAD_DIM
IN_WIDTH = 3 * SB_WIDTH + 4 * RET_WIDTH
D_FF = 2816
N_MOD = 6
ROPE_BASE = 10000.0
EPS = 1e-6
LOG2E = 1.4426950408889634

LANES = 128
MXU_DIM = 256
VMEM_LIMIT_BYTES = 56 * 1024 * 1024

TOKEN_TILE = 512
SB_TILE = MXU_DIM
SB_GROUP = 2
SB_UNROLL = 3
RET_CHUNK = MXU_DIM
FF_CHUNK = MXU_DIM
SOFTPLUS_CLAMP = 30.0
SHIFT_OUT = 1e30
UNDERFLOW_LOG2 = 160.0


def _dot(a, b):
    return jnp.dot(a, b, preferred_element_type=F32)


def _dot_nt(a, b):
    return lax.dot_general(a, b, (((1,), (1,)), ((), ())), preferred_element_type=F32)


def _dot_tn(a, b):
    return lax.dot_general(a, b, (((0,), (0,)), ((), ())), preferred_element_type=F32)


def _split_bf16(x):
    hi = x.astype(BF16)
    lo = (x - hi.astype(F32)).astype(BF16)
    return hi, lo


def _rms(x, g):
    return x * lax.rsqrt(jnp.mean(x * x, axis=-1, keepdims=True) + EPS) * g


def _softplus2(z2):
    e = jnp.exp2(jnp.minimum(z2, SOFTPLUS_CLAMP))
    return jnp.maximum(z2, jnp.log(1.0 + e) * LOG2E)


def _params(semantics, flags=None):
    return pltpu.CompilerParams(dimension_semantics=semantics, vmem_limit_bytes=VMEM_LIMIT_BYTES,
                                flags=flags)


def _resident(shape):
    zeros = (0,) * len(shape)
    return pl.BlockSpec(shape, lambda *_: zeros, pipeline_mode=pl.Buffered(1))


def _mod_kernel(c_ref, w_ref, b_ref, o_ref):
    c = c_ref[...]
    a = c * jax.nn.sigmoid(c)
    a_hi, a_lo = _split_bf16(a)
    w_hi, w_lo = _split_bf16(w_ref[0])
    o_ref[0] = _dot(a_hi, w_hi) + _dot(a_lo, w_hi) + _dot(a_hi, w_lo) + b_ref[0]


def _modulation(c, w_ada, b_ada):
    depth = w_ada.shape[0]
    rows = c.shape[0]
    return pl.pallas_call(
        _mod_kernel,
        grid=(depth, N_MOD),
        in_specs=[
            pl.BlockSpec((rows, D_MODEL), lambda l, j: (0, 0)),
            pl.BlockSpec((1, D_MODEL, D_MODEL), lambda l, j: (l, 0, j)),
            pl.BlockSpec((1, 1, D_MODEL), lambda l, j: (l, 0, j)),
        ],
        out_specs=pl.BlockSpec((1, rows, D_MODEL), lambda l, j: (l, 0, j)),
        out_shape=jax.ShapeDtypeStruct((depth, rows, N_MOD * D_MODEL), F32),
        compiler_params=_params(("arbitrary", "arbitrary")),
        name="modulation",
    )(c, w_ada, b_ada.reshape(depth, 1, N_MOD * D_MODEL))


def _rope_kernel(inv_ref, cos_ref, sin_ref, *, rows, pos0):
    row = lax.broadcasted_iota(jnp.int32, (rows, LANES), 0) + (pl.program_id(0) * rows + pos0)
    lane = lax.broadcasted_iota(jnp.int32, (rows, LANES), 1)
    ang = row.astype(F32) * inv_ref[...]
    sin = jnp.sin(ang)
    cos_ref[...] = jnp.cos(ang)
    sin_ref[...] = jnp.where(lane < RET_HEAD_DIM // 2, -sin, sin)


def _rope_tables(n, pos0):
    half = RET_HEAD_DIM // 2
    inv = ROPE_BASE ** (-jnp.arange(half, dtype=F32) / half)
    inv2 = jnp.concatenate([inv, inv]).reshape(1, LANES)
    rows = min(n, 2048)
    assert n % rows == 0
    return pl.pallas_call(
        functools.partial(_rope_kernel, rows=rows, pos0=pos0),
        grid=(n // rows,),
        in_specs=[pl.BlockSpec((1, LANES), lambda i: (0, 0))],
        out_specs=[pl.BlockSpec((rows, LANES), lambda i: (i, 0))] * 2,
        out_shape=[jax.ShapeDtypeStruct((n, LANES), F32)] * 2,
        compiler_params=_params(("arbitrary",)),
        name="rope_tables",
    )(inv2)


def _proj_kernel(x_ref, sc_ref, sh_ref, g_ref, cos_ref, sin_ref, w_ref, *rest, tm, kb, prompt):
    if prompt:
        wkvt_ref, _, _, q_ref, kt_ref, kb_ref, vtf_ref, vt_ref, qr_ref, kr_ref, vr_ref, gate_ref = rest
    else:
        q_ref, k_ref, v_ref, qr_ref, kr_ref, vr_ref, gate_ref = rest
    h = _rms(x_ref[...], g_ref[...]) * (1.0 + sc_ref[0]) + sh_ref[0]
    hb = h.astype(BF16)

    def seg(a, b):
        return _dot(hb, w_ref[:, a:b])

    q_ref[...] = (seg(0, SB_WIDTH) * (SB_HEAD_DIM ** -0.5 * LOG2E)).astype(BF16)
    k = seg(SB_WIDTH, 2 * SB_WIDTH)
    if prompt:
        kb_ref[...] = k.astype(BF16)
        kvt = _dot_nt(wkvt_ref[...], hb)
        kt_ref[0, 0] = kvt[:SB_WIDTH].reshape(SB_HEADS, SB_HEAD_DIM, tm)
        vtf_ref[0, 0] = kvt[SB_WIDTH:].reshape(SB_HEADS, SB_HEAD_DIM, tm)
        vt = kvt[SB_WIDTH:].astype(BF16)
        for hp in range(SB_WIDTH // LANES):
            for j in range(tm // kb):
                vt_ref[0, hp, j] = vt[hp * LANES:(hp + 1) * LANES, j * kb:(j + 1) * kb]
    else:
        k_ref[...] = k
        v_ref[...] = seg(2 * SB_WIDTH, 3 * SB_WIDTH)
    cos = cos_ref[...]
    sin = sin_ref[...]
    base = 3 * SB_WIDTH
    qr = seg(base, base + RET_WIDTH)
    kr = seg(base + RET_WIDTH, base + 2 * RET_WIDTH)
    for hh in range(RET_HEADS):
        sl = slice(hh * RET_HEAD_DIM, (hh + 1) * RET_HEAD_DIM)
        qh = qr[:, sl]
        kh = kr[:, sl]
        qr_ref[:, sl] = (qh * cos + pltpu.roll(qh, RET_HEAD_DIM // 2, 1) * sin).astype(BF16)
        kr_ref[:, sl] = ((kh * cos + pltpu.roll(kh, RET_HEAD_DIM // 2, 1) * sin)
                         * RET_HEAD_DIM ** -0.5).astype(BF16)
    vr_ref[...] = seg(base + 2 * RET_WIDTH, base + 3 * RET_WIDTH).astype(BF16)
    gate_ref[...] = seg(base + 3 * RET_WIDTH, base + 4 * RET_WIDTH)


def _proj(x, sc, sh, g, cos, sin, w_in_b, wkvt_b=None, kv_out=None, layer=0, *, batch, prompt):
    n = x.shape[0]
    tm = min(TOKEN_TILE, n)
    tiles = n // tm
    per_seq = max(tiles // batch, 1)
    pos_tiles = cos.shape[0] // tm
    kb = SB_TILE
    mrows = sc.shape[1]

    def row(i):
        return (i, 0)

    def mod(i):
        return (i // per_seq if mrows == 1 else i, 0, 0)

    def pos(i):
        return (i % pos_tiles, 0)

    in_specs = [
        pl.BlockSpec((tm, D_MODEL), row),
        pl.BlockSpec((1, mrows, D_MODEL), mod),
        pl.BlockSpec((1, mrows, D_MODEL), mod),
        _resident((1, D_MODEL)),
        pl.BlockSpec((tm, LANES), pos),
        pl.BlockSpec((tm, LANES), pos),
        _resident((D_MODEL, IN_WIDTH)),
    ]
    args = [x, sc, sh, g, cos, sin, w_in_b]
    wide_b = jax.ShapeDtypeStruct((n, SB_WIDTH), BF16)
    wide_f = jax.ShapeDtypeStruct((n, SB_WIDTH), F32)
    wide = pl.BlockSpec((tm, SB_WIDTH), row)
    if prompt:
        in_specs += [_resident((2 * SB_WIDTH, D_MODEL)), pl.BlockSpec(memory_space=pl.ANY),
                     pl.BlockSpec(memory_space=pl.ANY)]
        aliases = {len(args) + 1: 1, len(args) + 2: 3}
        args += [wkvt_b, *kv_out]
        nkb = n // batch // kb
        hps = SB_WIDTH // LANES
        head_f = jax.ShapeDtypeStruct(kv_out[0].shape, F32)
        head = pl.BlockSpec((1, 1, SB_HEADS, SB_HEAD_DIM, tm),
                            lambda i: (layer, i // per_seq, 0, 0, i % per_seq))
        out_shape = [wide_b, head_f, wide_b, head_f,
                     jax.ShapeDtypeStruct((batch, hps, nkb, LANES, kb), BF16),
                     wide_b, wide_b, wide_b, wide_f]
        out_specs = [wide, head, wide, head,
                     pl.BlockSpec((1, hps, tm // kb, LANES, kb),
                                  lambda i: (i // per_seq, 0, i % per_seq, 0, 0)),
                     wide, wide, wide, wide]
    else:
        aliases = {}
        out_shape = [wide_b, wide_f, wide_f, wide_b, wide_b, wide_b, wide_f]
        out_specs = [wide] * 7
    return pl.pallas_call(
        functools.partial(_proj_kernel, tm=tm, kb=kb, prompt=prompt),
        grid=(tiles,),
        in_specs=in_specs,
        out_specs=out_specs,
        out_shape=out_shape,
        input_output_aliases=aliases,
        compiler_params=_params(("arbitrary",)),
        name="proj_prompt" if prompt else "proj_sample",
    )(*args)


def _sb_prompt_kernel(q_ref, k_ref, vt_ref, o_ref, acc_ref, zt_ref, s_ref, w_ref, *, t, nq, group):
    first_tile = pl.program_id(2) * group
    hd = SB_HEAD_DIM
    heads = range(2)
    tiles = range(group)
    chains = [(g, h) for g in tiles for h in heads]
    lane = lax.broadcasted_iota(jnp.int32, (t, LANES), 1)
    qms = []
    for g in tiles:
        q = q_ref[0, g * t:(g + 1) * t, :]
        qms.append((jnp.where(lane < hd, q, jnp.zeros_like(q)),
                    jnp.where(lane >= hd, q, jnp.zeros_like(q))))
    kk = lax.broadcasted_iota(jnp.int32, (t, t), 0)
    qq = lax.broadcasted_iota(jnp.int32, (t, t), 1)
    tri = (qq >= kk).astype(BF16)
    valid = kk < qq

    def keys(j):
        return k_ref[0, pl.ds(pl.multiple_of(j * t, t), t), :]

    def gather(j, ws):
        vblk = vt_ref[0, 0, j]
        return [_dot(vblk[h * hd:(h + 1) * hd, :], ws[h]) for h in heads]

    def weights(zt, r, carried):
        return jnp.exp2(jnp.minimum(zt - r, 0.0) + carried)

    diag = [first_tile + g for g in tiles]
    left = [jnp.maximum(diag[g] - 1, 0) for g in tiles]
    shift_left = [jnp.where(diag[g] > 0, 0.0, -SHIFT_OUT) for g in tiles]
    k_diag = [keys(diag[g]) for g in tiles]
    k_left = [keys(left[g]) for g in tiles]
    z_diag = [_dot_nt(k_diag[g], qms[g][h]) for g, h in chains]
    z_left = [_dot_nt(k_left[g], qms[g][h]) for g, h in chains]
    s_diag = [jnp.where(valid, _softplus2(z), 0.0).astype(BF16) for z in z_diag]
    s_left = [_softplus2(z).astype(BF16) for z in z_left]
    r_diag = [_dot(tri, s) for s in s_diag]
    r_left = [_dot(tri, s) for s in s_left]
    cum_diag = [-r[0:1, :] for r in r_diag]
    w_diag = [jnp.where(valid, weights(z_diag[c], r_diag[c], 0.0), 0.0).astype(BF16)
              for c in range(len(chains))]
    w_left = [weights(z_left[c], r_left[c], cum_diag[c] + shift_left[chains[c][0]]).astype(BF16)
              for c in range(len(chains))]
    cum_left = [cum_diag[c] - r_left[c][0:1, :] for c in range(len(chains))]
    for g in tiles:
        p_diag = gather(diag[g], w_diag[2 * g:2 * g + 2])
        p_left = gather(left[g], w_left[2 * g:2 * g + 2])
        for h in heads:
            acc_ref[g, h * hd:(h + 1) * hd, :] = p_diag[h] + p_left[h]

    def all_faded(cums):
        return jnp.max(jnp.maximum(cums[0], cums[1])) < -UNDERFLOW_LOG2

    def rest_of_sweep(g, cums):
        i = diag[g]
        qm = qms[g]

        def block_of(m):
            return jnp.clip(i - 2 - m, 0, nq - 1)

        def scores(j):
            kblk = keys(j)
            return [_dot_nt(kblk, qm[h]) for h in heads]

        rounds = jnp.where(i > 1, (i + SB_UNROLL) // SB_UNROLL, 0)
        faded = all_faded(cums)

        @pl.when(jnp.logical_and(rounds > 0, jnp.logical_not(faded)))
        def _():
            first = scores(block_of(0))
            for h in heads:
                zt_ref[0, h] = first[h]
                zt_ref[2, h] = jnp.zeros((t, t), F32)
                s_ref[2, h] = jnp.zeros((t, t), BF16)
                w_ref[1, h] = jnp.zeros((t, t), BF16)

        def trip(n, r, cums):
            nxt_slot, cur, old, done = (r + 1) % 3, r, (r + 2) % 3, (r + 1) % 3
            shift = jnp.where(jnp.logical_and(n >= 1, n < i), 0.0, -SHIFT_OUT)
            rs = [_dot(tri, s_ref[old, h]) for h in heads]
            new_ws = [weights(zt_ref[old, h], rs[h], cums[h] + shift) for h in heads]
            new_cums = [cums[h] - rs[h][0:1, :] for h in heads]
            nxt = scores(block_of(n + 1))
            parts = gather(block_of(n - 2), [w_ref[done, h] for h in heads])
            for h in heads:
                s_ref[cur, h] = _softplus2(zt_ref[cur, h]).astype(BF16)
            for h in heads:
                acc_ref[g, h * hd:(h + 1) * hd, :] += parts[h]
                w_ref[old, h] = new_ws[h].astype(BF16)
                zt_ref[nxt_slot, h] = nxt[h]
            return tuple(new_cums)

        def more(state):
            p, faded, _ = state
            return jnp.logical_and(p < rounds, jnp.logical_not(faded))

        def body(state):
            p, faded, cums = state
            for r in range(SB_UNROLL):
                cums = trip(SB_UNROLL * p + r, r % 3, cums)
                if r == SB_UNROLL - 2:
                    faded = all_faded(cums)
            return p + 1, faded, cums

        lax.while_loop(more, body, (jnp.int32(0), faded, cums))

    for g in tiles:
        rest_of_sweep(g, tuple(cum_left[2 * g:2 * g + 2]))
        o_ref[0, g * t:(g + 1) * t, :] = acc_ref[g].T


def _sb_prompt(q, kb16, vt, *, batch):
    n = q.shape[0]
    s = n // batch
    t = SB_TILE
    nq = s // t
    group = SB_GROUP if nq % SB_GROUP == 0 else 1
    hps = SB_WIDTH // LANES
    q3 = q.reshape(batch, s, SB_WIDTH)
    k3 = kb16.reshape(batch, s, SB_WIDTH)
    out = pl.pallas_call(
        functools.partial(_sb_prompt_kernel, t=t, nq=nq, group=group),
        grid=(batch, hps, nq // group),
        in_specs=[
            pl.BlockSpec((1, group * t, LANES), lambda b, hp, i: (b, i, hp)),
            pl.BlockSpec((1, s, LANES), lambda b, hp, i: (b, 0, hp)),
            pl.BlockSpec((1, 1, nq, LANES, t), lambda b, hp, i: (b, hp, 0, 0, 0)),
        ],
        out_specs=pl.BlockSpec((1, group * t, LANES), lambda b, hp, i: (b, i, hp)),
        out_shape=jax.ShapeDtypeStruct((batch, s, SB_WIDTH), F32),
        scratch_shapes=[pltpu.VMEM((group, LANES, t), F32), pltpu.VMEM((3, 2, t, t), F32),
                        pltpu.VMEM((3, 2, t, t), BF16), pltpu.VMEM((3, 2, t, t), BF16)],
        compiler_params=_params(("arbitrary", "arbitrary", "arbitrary")),
        name="sb_prompt",
    )(q3, k3, vt)
    return out.reshape(n, SB_WIDTH)


def _sb_sample_kernel(q_ref, kn_ref, vn_ref, ck_ref, cv_ref, o_ref, acc_ref, cum_ref, fade_ref,
                      *, tq, kb, nb):
    j = pl.program_id(1)
    hd = SB_HEAD_DIM
    heads = range(SB_HEADS)
    cols = [slice(h * hd, (h + 1) * hd) for h in heads]
    qs = [q_ref[:, c] for c in cols]

    def cumsum_right(s, m):
        s_hi, s_lo = _split_bf16(s)
        return _dot(s_hi, m) + _dot(s_lo, m)

    def sweep(zs, vts, m, cums, mask=None):
        ss = [_softplus2(z) for z in zs]
        if mask is not None:
            ss = [jnp.where(mask, s, 0.0) for s in ss]
        rs = [cumsum_right(s, m) for s in ss]
        ws = [jnp.exp2(jnp.minimum(zs[h] - rs[h], 0.0) + cums[h]) for h in heads]
        if mask is not None:
            ws = [jnp.where(mask, w, 0.0) for w in ws]
        outs = [_dot_nt(ws[h].astype(BF16), vts[h]) for h in heads]
        return outs, [cums[h] - rs[h][:, 0:1] for h in heads]

    @pl.when(j == 0)
    def _():
        qi = lax.broadcasted_iota(jnp.int32, (tq, tq), 0)
        ki = lax.broadcasted_iota(jnp.int32, (tq, tq), 1)
        valid = ki < qi
        tri_new = (qi >= ki).astype(BF16)
        zs = [_dot_nt(qs[h], kn_ref[:, cols[h]].astype(BF16)) for h in heads]
        vts = [vn_ref[:, cols[h]].astype(BF16).T for h in heads]
        outs, cums = sweep(zs, vts, tri_new, [jnp.zeros((tq, 1), F32)] * SB_HEADS, valid)
        for h in heads:
            acc_ref[h] = outs[h]
            cum_ref[h] = cums[h]
        fade_ref[0] = 0

    @pl.when(fade_ref[0] == 0)
    def _():
        a = lax.broadcasted_iota(jnp.int32, (kb, kb), 0)
        b = lax.broadcasted_iota(jnp.int32, (kb, kb), 1)
        tri = (a >= b).astype(BF16)
        zs = [_dot(qs[h], ck_ref[0, 0, h].astype(BF16)) for h in heads]
        vts = [cv_ref[0, 0, h].astype(BF16) for h in heads]
        outs, cums = sweep(zs, vts, tri, [cum_ref[h] for h in heads])
        for h in heads:
            acc_ref[h] += outs[h]
            cum_ref[h] = cums[h]
        faded = jnp.max(functools.reduce(jnp.maximum, cums)) < -UNDERFLOW_LOG2
        fade_ref[0] = faded.astype(jnp.int32)

    @pl.when(j == nb - 1)
    def _():
        for h in heads:
            o_ref[:, cols[h]] = acc_ref[h]


def _sb_sample(q, k_new, v_new, cache_kt, cache_vt, layer, *, batch):
    n = q.shape[0]
    tq = n // batch
    past = cache_kt.shape[-1]
    kb = min(MXU_DIM, past)
    nb = past // kb
    assert past == nb * kb
    row = pl.BlockSpec((tq, SB_WIDTH), lambda b, j: (b, 0))
    cache = pl.BlockSpec((1, 1, SB_HEADS, SB_HEAD_DIM, kb), lambda b, j: (layer, b, 0, 0, nb - 1 - j))
    return pl.pallas_call(
        functools.partial(_sb_sample_kernel, tq=tq, kb=kb, nb=nb),
        grid=(batch, nb),
        in_specs=[row, row, row, cache, cache],
        out_specs=row,
        out_shape=jax.ShapeDtypeStruct((n, SB_WIDTH), F32),
        scratch_shapes=[pltpu.VMEM((SB_HEADS, tq, SB_HEAD_DIM), F32),
                        pltpu.VMEM((SB_HEADS, tq, 1), F32), pltpu.SMEM((1,), jnp.int32)],
        compiler_params=_params(("arbitrary", "arbitrary")),
        name="sb_sample",
    )(q, k_new, v_new, cache_kt, cache_vt)


def _ret_log_gamma(h):
    return math.log1p(-2.0 ** (-5 - h))


def _ret_kernel(q_ref, k_ref, v_ref, s0_ref, o_ref, sout_ref, state, dec, qdec, kdec, *, c, nchunks):
    seq = pl.program_id(0)
    ci = pl.program_id(1)

    @pl.when(jnp.logical_and(seq == 0, ci == 0))
    def _():
        li = lax.broadcasted_iota(jnp.int32, (c, c), 0)
        mi = lax.broadcasted_iota(jnp.int32, (c, c), 1)
        diff = (li - mi).astype(F32)
        pos = lax.broadcasted_iota(jnp.int32, (c, RET_HEAD_DIM), 0).astype(F32)
        for h in range(RET_HEADS):
            lg = _ret_log_gamma(h)
            dec[h] = jnp.where(diff >= 0.0, jnp.exp(jnp.maximum(diff, 0.0) * lg), 0.0)
            qdec[h] = jnp.exp((pos + 1.0) * lg)
            kdec[h] = jnp.exp((c - 1.0 - pos) * lg)

    @pl.when(ci == 0)
    def _():
        state[...] = s0_ref[0]

    heads = range(RET_HEADS)
    cols = [slice(h * RET_HEAD_DIM, (h + 1) * RET_HEAD_DIM) for h in heads]
    qs = [q_ref[:, sl] for sl in cols]
    ks = [k_ref[:, sl] for sl in cols]
    vs = [v_ref[:, sl] for sl in cols]
    sts = [state[h] for h in heads]
    scores = [_dot_nt(qs[h], ks[h]) for h in heads]
    cross = [_dot(qs[h], sts[h].astype(BF16)) for h in heads]
    kds = [(ks[h].astype(F32) * kdec[h]).astype(BF16) for h in heads]
    grown = [_dot_tn(kds[h], vs[h]) for h in heads]
    inner = [_dot((scores[h] * dec[h]).astype(BF16), vs[h]) for h in heads]
    for h in heads:
        o_ref[:, cols[h]] = inner[h] + cross[h] * qdec[h]
        state[h] = math.exp(c * _ret_log_gamma(h)) * sts[h] + grown[h]

    @pl.when(ci == nchunks - 1)
    def _():
        sout_ref[0] = state[...]


def _retention(q, k, v, state0, *, batch):
    n = q.shape[0]
    t = n // batch
    c = min(RET_CHUNK, t)
    nchunks = t // c
    row = pl.BlockSpec((c, RET_WIDTH), lambda b, i: (b * nchunks + i, 0))
    st = pl.BlockSpec((1, RET_HEADS, RET_HEAD_DIM, RET_HEAD_DIM), lambda b, i: (b, 0, 0, 0))
    return pl.pallas_call(
        functools.partial(_ret_kernel, c=c, nchunks=nchunks),
        grid=(batch, nchunks),
        in_specs=[row, row, row, st],
        out_specs=[row, st],
        out_shape=[jax.ShapeDtypeStruct((n, RET_WIDTH), F32),
                   jax.ShapeDtypeStruct((batch, RET_HEADS, RET_HEAD_DIM, RET_HEAD_DIM), F32)],
        scratch_shapes=[pltpu.VMEM((RET_HEADS, RET_HEAD_DIM, RET_HEAD_DIM), F32),
                        pltpu.VMEM((RET_HEADS, c, c), F32),
                        pltpu.VMEM((RET_HEADS, c, RET_HEAD_DIM), F32),
                        pltpu.VMEM((RET_HEADS, c, RET_HEAD_DIM), F32)],
        compiler_params=_params(("arbitrary", "arbitrary")),
        name="retention",
    )(q, k, v, state0)


def _merge_ffn_kernel(x_ref, osb_ref, oret_ref, gate_ref, gt1_ref, sc2_ref, sh2_ref, gt2_ref,
                      gsb_ref, gret_ref, gffn_ref, gfin_ref, wout_ref, wfi_ref, wfo_ref,
                      *outs, final):
    xo_ref = outs[0]
    a = _rms(osb_ref[...], gsb_ref[...]).astype(BF16)
    mix = _dot(a, wout_ref[0:SB_WIDTH, :])
    gate = gate_ref[...]
    gate = gate * jax.nn.sigmoid(gate)
    for h in range(RET_HEADS):
        sl = slice(h * RET_HEAD_DIM, (h + 1) * RET_HEAD_DIM)
        r = (_rms(oret_ref[:, sl], gret_ref[:, sl]) * gate[:, sl]).astype(BF16)
        mix += _dot(r, wout_ref[SB_WIDTH + h * RET_HEAD_DIM:SB_WIDTH + (h + 1) * RET_HEAD_DIM, :])
    x1 = x_ref[...] + gt1_ref[0] * mix
    h2 = (_rms(x1, gffn_ref[...]) * (1.0 + sc2_ref[0]) + sh2_ref[0]).astype(BF16)
    ff = jnp.zeros_like(x1)
    for j in range(D_FF // FF_CHUNK):
        cols = slice(j * FF_CHUNK, (j + 1) * FF_CHUNK)
        ucols = slice(D_FF + j * FF_CHUNK, D_FF + (j + 1) * FF_CHUNK)
        g = _dot(h2, wfi_ref[:, cols])
        u = _dot(h2, wfi_ref[:, ucols])
        act = (g * jax.nn.sigmoid(g) * u).astype(BF16)
        ff += _dot(act, wfo_ref[cols, :])
    x2 = x1 + gt2_ref[0] * ff
    xo_ref[...] = x2
    if final:
        outs[1][...] = _rms(x2, gfin_ref[...])


def _merge_ffn(x, o_sb, o_ret, gate, gt1, sc2, sh2, gt2, g_sb, g_ret, g_ffn, g_fin,
               w_out_b, w_fi_b, w_fo_b, *, batch, final):
    n = x.shape[0]
    tm = min(TOKEN_TILE, n)
    tiles = n // tm
    per_seq = max(tiles // batch, 1)
    mrows = gt1.shape[1]

    def row(i):
        return (i, 0)

    def mod(i):
        return (i // per_seq if mrows == 1 else i, 0, 0)

    full = pl.BlockSpec((tm, D_MODEL), row)
    half = pl.BlockSpec((tm, SB_WIDTH), row)
    mspec = pl.BlockSpec((1, mrows, D_MODEL), mod)
    n_out = 2 if final else 1
    res = pl.pallas_call(
        functools.partial(_merge_ffn_kernel, final=final),
        grid=(tiles,),
        in_specs=[full, half, half, half, mspec, mspec, mspec, mspec,
                  _resident((1, SB_WIDTH)), _resident((1, RET_WIDTH)),
                  _resident((1, D_MODEL)), _resident((1, D_MODEL)),
                  _resident((D_MODEL, D_MODEL)), _resident((D_MODEL, 2 * D_FF)),
                  _resident((D_FF, D_MODEL))],
        out_specs=[full] * n_out,
        out_shape=[jax.ShapeDtypeStruct((n, D_MODEL), F32)] * n_out,
        compiler_params=_params(("arbitrary",)),
        name="merge_ffn",
    )(x, o_sb, o_ret, gate, gt1, sc2, sh2, gt2, g_sb, g_ret, g_ffn, g_fin, w_out_b, w_fi_b, w_fo_b)
    return res if final else (res[0], None)


def kernel(x_prompt, x_sample, cache_sb_k, cache_sb_v, state_ret, c_prompt, c_sample,
           g_norm_mix, g_norm_ffn, w_ada, b_ada, w_in, g_sb_out, g_ret_out, w_out,
           w_ff_in, w_ff_out, g_final):
    depth = w_in.shape[0]
    bp, sp, _ = x_prompt.shape
    bs, ts, _ = x_sample.shape
    past = cache_sb_k.shape[2]
    n_p = bp * sp
    n_s = bs * ts

    c_all = jnp.concatenate([c_prompt, c_sample], axis=0)
    pad = (-c_all.shape[0]) % 16
    mods = _modulation(jnp.pad(c_all, ((0, pad), (0, 0))), w_ada, b_ada)
    mods = mods.reshape(depth, -1, N_MOD, D_MODEL)
    mods_p = mods[:, :bp]
    mods_s = jnp.repeat(mods[:, bp:bp + bs], ts, axis=1)

    cos_p, sin_p = _rope_tables(sp, 0)
    cos_s, sin_s = _rope_tables(ts, past)
    cos_s = jnp.tile(cos_s, (bs, 1))
    sin_s = jnp.tile(sin_s, (bs, 1))

    cache_k = jnp.transpose(cache_sb_k, (0, 1, 3, 4, 2))
    cache_v = jnp.transpose(cache_sb_v, (0, 1, 3, 4, 2))
    zero_state = jnp.zeros((bp, RET_HEADS, RET_HEAD_DIM, RET_HEAD_DIM), F32)

    xp = x_prompt.reshape(n_p, D_MODEL)
    xs = x_sample.reshape(n_s, D_MODEL)
    kv_p = (jnp.zeros((depth, bp, SB_HEADS, SB_HEAD_DIM, sp), F32),) * 2
    rp, ks, vs, rs = [], [], [], []
    yp = ys = None
    for l in range(depth):
        final = l == depth - 1
        w_in_b = w_in[l].astype(BF16)
        wkvt_b = w_in[l][:, SB_WIDTH:3 * SB_WIDTH].T.astype(BF16)
        w_out_b = w_out[l].astype(BF16)
        w_fi_b = w_ff_in[l].astype(BF16)
        w_fo_b = w_ff_out[l].astype(BF16)
        g_mix = g_norm_mix[l].reshape(1, D_MODEL)
        g_ffn = g_norm_ffn[l].reshape(1, D_MODEL)
        g_sb = g_sb_out[l].reshape(1, SB_WIDTH)
        g_ret = g_ret_out[l].reshape(1, RET_WIDTH)
        g_fin = g_final.reshape(1, D_MODEL)

        m = [mods_p[l, :, j].reshape(bp, 1, D_MODEL) for j in range(N_MOD)]
        q, k, k16, v, vt, qr, kr, vr, gate = _proj(
            xp, m[1], m[0], g_mix, cos_p, sin_p, w_in_b, wkvt_b, kv_p, l, batch=bp, prompt=True)
        kv_p = (k, v)
        o_sb = _sb_prompt(q, k16, vt, batch=bp)
        o_ret, st = _retention(qr, kr, vr, zero_state, batch=bp)
        xp, yp = _merge_ffn(xp, o_sb, o_ret, gate, m[2], m[4], m[3], m[5], g_sb, g_ret, g_ffn,
                            g_fin, w_out_b, w_fi_b, w_fo_b, batch=bp, final=final)
        rp.append(st)

        m = [mods_s[l, :, j].reshape(1, n_s, D_MODEL) for j in range(N_MOD)]
        q, k, v, qr, kr, vr, gate = _proj(
            xs, m[1], m[0], g_mix, cos_s, sin_s, w_in_b, batch=bs, prompt=False)
        o_sb = _sb_sample(q, k, v, cache_k, cache_v, l, batch=bs)
        o_ret, st = _retention(qr, kr, vr, state_ret[l], batch=bs)
        xs, ys = _merge_ffn(xs, o_sb, o_ret, gate, m[2], m[4], m[3], m[5], g_sb, g_ret, g_ffn,
                            g_fin, w_out_b, w_fi_b, w_fo_b, batch=bs, final=final)
        ks.append(k)
        vs.append(v)
        rs.append(st)

    def heads(a, b, t):
        return a.reshape(depth, b, t, SB_HEADS, SB_HEAD_DIM)

    def rows(a):
        return jnp.transpose(a, (0, 1, 4, 2, 3))

    return (yp.reshape(bp, sp, D_MODEL), ys.reshape(bs, ts, D_MODEL),
            rows(kv_p[0]), rows(kv_p[1]), jnp.stack(rp),
            heads(jnp.stack(ks), bs, ts), heads(jnp.stack(vs), bs, ts), jnp.stack(rs))
```

```python
import functools
import math

import jax
import jax.numpy as jnp
from jax import lax
from jax.experimental import pallas as pl
from jax.experimental.pallas import tpu as pltpu

F32 = jnp.float32
BF16 = jnp.bfloat16

D_MODEL = 1024
SB_WIDTH = 512
RET_WIDTH = 512
SB_HEAD_DIM = 64
SB_HEADS = SB_WIDTH // SB_HEAD_DIM
RET_HEAD_DIM = 128
RET_HEADS = RET_WIDTH // RET_HEAD_DIM
IN_WIDTH = 3 * SB_WIDTH + 4 * RET_WIDTH
D_FF = 2816
N_MOD = 6
ROPE_BASE = 10000.0
EPS = 1e-6
LOG2E = 1.4426950408889634

LANES = 128
MXU_DIM = 256
VMEM_LIMIT_BYTES = 56 * 1024 * 1024

TOKEN_TILE = 512
SB_TILE = MXU_DIM
SB_GROUP = 2
SB_UNROLL = 3
SAMPLE_SPAN = 1024
RET_CHUNK = MXU_DIM
RET_GROUP = 2
FF_CHUNK = MXU_DIM
SOFTPLUS_CLAMP = 30.0
SHIFT_OUT = 1e30
UNDERFLOW_LOG2 = 160.0


def _dot(a, b):
    return jnp.dot(a, b, preferred_element_type=F32)


def _dot_nt(a, b):
    return lax.dot_general(a, b, (((1,), (1,)), ((), ())), preferred_element_type=F32)


def _dot_tn(a, b):
    return lax.dot_general(a, b, (((0,), (0,)), ((), ())), preferred_element_type=F32)


def _split_bf16(x):
    hi = x.astype(BF16)
    lo = (x - hi.astype(F32)).astype(BF16)
    return hi, lo


def _rms(x, g):
    return x * lax.rsqrt(jnp.mean(x * x, axis=-1, keepdims=True) + EPS) * g


def _softplus2(z2):
    e = jnp.exp2(jnp.minimum(z2, SOFTPLUS_CLAMP))
    return jnp.maximum(z2, jnp.log(1.0 + e) * LOG2E)


def _params(semantics, flags=None):
    return pltpu.CompilerParams(dimension_semantics=semantics, vmem_limit_bytes=VMEM_LIMIT_BYTES,
                                flags=flags)


def _resident(shape):
    zeros = (0,) * len(shape)
    return pl.BlockSpec(shape, lambda *_: zeros, pipeline_mode=pl.Buffered(1))


def _mod_kernel(c_ref, w_ref, b_ref, o_ref):
    c = c_ref[...]
    a = c * jax.nn.sigmoid(c)
    a_hi, a_lo = _split_bf16(a)
    w_hi, w_lo = _split_bf16(w_ref[0])
    o_ref[0] = _dot(a_hi, w_hi) + _dot(a_lo, w_hi) + _dot(a_hi, w_lo) + b_ref[0]


def _modulation(c, w_ada, b_ada):
    depth = w_ada.shape[0]
    rows = c.shape[0]
    return pl.pallas_call(
        _mod_kernel,
        grid=(depth, N_MOD),
        in_specs=[
            pl.BlockSpec((rows, D_MODEL), lambda l, j: (0, 0)),
            pl.BlockSpec((1, D_MODEL, D_MODEL), lambda l, j: (l, 0, j)),
            pl.BlockSpec((1, 1, D_MODEL), lambda l, j: (l, 0, j)),
        ],
        out_specs=pl.BlockSpec((1, rows, D_MODEL), lambda l, j: (l, 0, j)),
        out_shape=jax.ShapeDtypeStruct((depth, rows, N_MOD * D_MODEL), F32),
        compiler_params=_params(("arbitrary", "arbitrary")),
        name="modulation",
    )(c, w_ada, b_ada.reshape(depth, 1, N_MOD * D_MODEL))


def _rope_kernel(inv_ref, cos_ref, sin_ref, *, rows, pos0):
    row = lax.broadcasted_iota(jnp.int32, (rows, LANES), 0) + (pl.program_id(0) * rows + pos0)
    lane = lax.broadcasted_iota(jnp.int32, (rows, LANES), 1)
    ang = row.astype(F32) * inv_ref[...]
    sin = jnp.sin(ang)
    cos_ref[...] = jnp.cos(ang)
    sin_ref[...] = jnp.where(lane < RET_HEAD_DIM // 2, -sin, sin)


def _rope_tables(n, pos0):
    half = RET_HEAD_DIM // 2
    inv = ROPE_BASE ** (-jnp.arange(half, dtype=F32) / half)
    inv2 = jnp.concatenate([inv, inv]).reshape(1, LANES)
    rows = min(n, 2048)
    assert n % rows == 0
    return pl.pallas_call(
        functools.partial(_rope_kernel, rows=rows, pos0=pos0),
        grid=(n // rows,),
        in_specs=[pl.BlockSpec((1, LANES), lambda i: (0, 0))],
        out_specs=[pl.BlockSpec((rows, LANES), lambda i: (i, 0))] * 2,
        out_shape=[jax.ShapeDtypeStruct((n, LANES), F32)] * 2,
        compiler_params=_params(("arbitrary",)),
        name="rope_tables",
    )(inv2)


def _proj_kernel(x_ref, sc_ref, sh_ref, g_ref, cos_ref, sin_ref, w_ref, *rest, tm, kb, prompt):
    if prompt:
        wkvt_ref, _, _, q_ref, kt_ref, kb_ref, vtf_ref, vt_ref, qr_ref, kr_ref, vr_ref, gate_ref = rest
    else:
        q_ref, k_ref, v_ref, qr_ref, kr_ref, vr_ref, gate_ref = rest
    h = _rms(x_ref[...], g_ref[...]) * (1.0 + sc_ref[0]) + sh_ref[0]
    hb = h.astype(BF16)

    def seg(a, b):
        return _dot(hb, w_ref[:, a:b])

    q_ref[...] = (seg(0, SB_WIDTH) * (SB_HEAD_DIM ** -0.5 * LOG2E)).astype(BF16)
    k = seg(SB_WIDTH, 2 * SB_WIDTH)
    if prompt:
        kb_ref[...] = k.astype(BF16)
        kvt = _dot_nt(wkvt_ref[...], hb)
        kt_ref[0, 0] = kvt[:SB_WIDTH].reshape(SB_HEADS, SB_HEAD_DIM, tm)
        vtf_ref[0, 0] = kvt[SB_WIDTH:].reshape(SB_HEADS, SB_HEAD_DIM, tm)
        vt = kvt[SB_WIDTH:].astype(BF16)
        for hp in range(SB_WIDTH // LANES):
            for j in range(tm // kb):
                vt_ref[0, hp, j] = vt[hp * LANES:(hp + 1) * LANES, j * kb:(j + 1) * kb]
    else:
        k_ref[...] = k
        v_ref[...] = seg(2 * SB_WIDTH, 3 * SB_WIDTH)
    cos = cos_ref[...]
    sin = sin_ref[...]
    base = 3 * SB_WIDTH
    qr = seg(base, base + RET_WIDTH)
    kr = seg(base + RET_WIDTH, base + 2 * RET_WIDTH)
    for hh in range(RET_HEADS):
        sl = slice(hh * RET_HEAD_DIM, (hh + 1) * RET_HEAD_DIM)
        qh = qr[:, sl]
        kh = kr[:, sl]
        qr_ref[:, sl] = (qh * cos + pltpu.roll(qh, RET_HEAD_DIM // 2, 1) * sin).astype(BF16)
        kr_ref[:, sl] = ((kh * cos + pltpu.roll(kh, RET_HEAD_DIM // 2, 1) * sin)
                         * RET_HEAD_DIM ** -0.5).astype(BF16)
    vr_ref[...] = seg(base + 2 * RET_WIDTH, base + 3 * RET_WIDTH).astype(BF16)
    gate_ref[...] = seg(base + 3 * RET_WIDTH, base + 4 * RET_WIDTH)


def _proj(x, sc, sh, g, cos, sin, w_in_b, wkvt_b=None, kv_out=None, layer=0, *, batch, prompt):
    n = x.shape[0]
    tm = min(TOKEN_TILE, n)
    tiles = n // tm
    per_seq = max(tiles // batch, 1)
    pos_tiles = cos.shape[0] // tm
    kb = SB_TILE
    mrows = sc.shape[1]

    def row(i):
        return (i, 0)

    def mod(i):
        return (i // per_seq if mrows == 1 else i, 0, 0)

    def pos(i):
        return (i % pos_tiles, 0)

    in_specs = [
        pl.BlockSpec((tm, D_MODEL), row),
        pl.BlockSpec((1, mrows, D_MODEL), mod),
        pl.BlockSpec((1, mrows, D_MODEL), mod),
        _resident((1, D_MODEL)),
        pl.BlockSpec((tm, LANES), pos),
        pl.BlockSpec((tm, LANES), pos),
        _resident((D_MODEL, IN_WIDTH)),
    ]
    args = [x, sc, sh, g, cos, sin, w_in_b]
    wide_b = jax.ShapeDtypeStruct((n, SB_WIDTH), BF16)
    wide_f = jax.ShapeDtypeStruct((n, SB_WIDTH), F32)
    wide = pl.BlockSpec((tm, SB_WIDTH), row)
    if prompt:
        in_specs += [_resident((2 * SB_WIDTH, D_MODEL)), pl.BlockSpec(memory_space=pl.ANY),
                     pl.BlockSpec(memory_space=pl.ANY)]
        aliases = {len(args) + 1: 1, len(args) + 2: 3}
        args += [wkvt_b, *kv_out]
        nkb = n // batch // kb
        hps = SB_WIDTH // LANES
        head_f = jax.ShapeDtypeStruct(kv_out[0].shape, F32)
        head = pl.BlockSpec((1, 1, SB_HEADS, SB_HEAD_DIM, tm),
                            lambda i: (layer, i // per_seq, 0, 0, i % per_seq))
        out_shape = [wide_b, head_f, wide_b, head_f,
                     jax.ShapeDtypeStruct((batch, hps, nkb, LANES, kb), BF16),
                     wide_b, wide_b, wide_b, wide_f]
        out_specs = [wide, head, wide, head,
                     pl.BlockSpec((1, hps, tm // kb, LANES, kb),
                                  lambda i: (i // per_seq, 0, i % per_seq, 0, 0)),
                     wide, wide, wide, wide]
    else:
        aliases = {}
        out_shape = [wide_b, wide_f, wide_f, wide_b, wide_b, wide_b, wide_f]
        out_specs = [wide] * 7
    return pl.pallas_call(
        functools.partial(_proj_kernel, tm=tm, kb=kb, prompt=prompt),
        grid=(tiles,),
        in_specs=in_specs,
        out_specs=out_specs,
        out_shape=out_shape,
        input_output_aliases=aliases,
        compiler_params=_params(("arbitrary",)),
        name="proj_prompt" if prompt else "proj_sample",
    )(*args)


def _sb_prompt_kernel(q_ref, k_ref, vt_ref, o_ref, acc_ref, zt_ref, s_ref, w_ref, *, t, nq, group):
    first_tile = pl.program_id(2) * group
    hd = SB_HEAD_DIM
    heads = range(2)
    tiles = range(group)
    chains = [(g, h) for g in tiles for h in heads]
    lane = lax.broadcasted_iota(jnp.int32, (t, LANES), 1)
    qms = []
    for g in tiles:
        q = q_ref[0, g * t:(g + 1) * t, :]
        qms.append((jnp.where(lane < hd, q, jnp.zeros_like(q)),
                    jnp.where(lane >= hd, q, jnp.zeros_like(q))))
    kk = lax.broadcasted_iota(jnp.int32, (t, t), 0)
    qq = lax.broadcasted_iota(jnp.int32, (t, t), 1)
    tri = (qq >= kk).astype(BF16)
    valid = kk < qq

    def keys(j):
        return k_ref[0, pl.ds(pl.multiple_of(j * t, t), t), :]

    def gather(j, ws):
        vblk = vt_ref[0, 0, j]
        return [_dot(vblk[h * hd:(h + 1) * hd, :], ws[h]) for h in heads]

    def weights(zt, r, carried):
        return jnp.exp2(jnp.minimum(zt - r, 0.0) + carried)

    diag = [first_tile + g for g in tiles]
    left = [jnp.maximum(diag[g] - 1, 0) for g in tiles]
    shift_left = [jnp.where(diag[g] > 0, 0.0, -SHIFT_OUT) for g in tiles]
    k_diag = [keys(diag[g]) for g in tiles]
    k_left = [keys(left[g]) for g in tiles]
    z_diag = [_dot_nt(k_diag[g], qms[g][h]) for g, h in chains]
    z_left = [_dot_nt(k_left[g], qms[g][h]) for g, h in chains]
    s_diag = [jnp.where(valid, _softplus2(z), 0.0).astype(BF16) for z in z_diag]
    s_left = [_softplus2(z).astype(BF16) for z in z_left]
    r_diag = [_dot(tri, s) for s in s_diag]
    r_left = [_dot(tri, s) for s in s_left]
    cum_diag = [-r[0:1, :] for r in r_diag]
    w_diag = [jnp.where(valid, weights(z_diag[c], r_diag[c], 0.0), 0.0).astype(BF16)
              for c in range(len(chains))]
    w_left = [weights(z_left[c], r_left[c], cum_diag[c] + shift_left[chains[c][0]]).astype(BF16)
              for c in range(len(chains))]
    cum_left = [cum_diag[c] - r_left[c][0:1, :] for c in range(len(chains))]
    for g in tiles:
        p_diag = gather(diag[g], w_diag[2 * g:2 * g + 2])
        p_left = gather(left[g], w_left[2 * g:2 * g + 2])
        for h in heads:
            acc_ref[g, h * hd:(h + 1) * hd, :] = p_diag[h] + p_left[h]

    def all_faded(cums):
        return jnp.max(jnp.maximum(cums[0], cums[1])) < -UNDERFLOW_LOG2

    def rest_of_sweep(g, cums):
        i = diag[g]
        qm = qms[g]

        def block_of(m):
            return jnp.clip(i - 2 - m, 0, nq - 1)

        def scores(j):
            kblk = keys(j)
            return [_dot_nt(kblk, qm[h]) for h in heads]

        rounds = jnp.where(i > 1, (i + SB_UNROLL) // SB_UNROLL, 0)
        faded = all_faded(cums)

        @pl.when(jnp.logical_and(rounds > 0, jnp.logical_not(faded)))
        def _():
            first = scores(block_of(0))
            for h in heads:
                zt_ref[0, h] = first[h]
                zt_ref[2, h] = jnp.zeros((t, t), F32)
                s_ref[2, h] = jnp.zeros((t, t), BF16)
                w_ref[1, h] = jnp.zeros((t, t), BF16)

        def trip(n, r, cums):
            nxt_slot, cur, old, done = (r + 1) % 3, r, (r + 2) % 3, (r + 1) % 3
            shift = jnp.where(jnp.logical_and(n >= 1, n < i), 0.0, -SHIFT_OUT)
            rs = [_dot(tri, s_ref[old, h]) for h in heads]
            new_ws = [weights(zt_ref[old, h], rs[h], cums[h] + shift) for h in heads]
            new_cums = [cums[h] - rs[h][0:1, :] for h in heads]
            nxt = scores(block_of(n + 1))
            parts = gather(block_of(n - 2), [w_ref[done, h] for h in heads])
            for h in heads:
                s_ref[cur, h] = _softplus2(zt_ref[cur, h]).astype(BF16)
            for h in heads:
                acc_ref[g, h * hd:(h + 1) * hd, :] += parts[h]
                w_ref[old, h] = new_ws[h].astype(BF16)
                zt_ref[nxt_slot, h] = nxt[h]
            return tuple(new_cums)

        def more(state):
            p, faded, _ = state
            return jnp.logical_and(p < rounds, jnp.logical_not(faded))

        def body(state):
            p, faded, cums = state
            for r in range(SB_UNROLL):
                cums = trip(SB_UNROLL * p + r, r % 3, cums)
                if r == SB_UNROLL - 2:
                    faded = all_faded(cums)
            return p + 1, faded, cums

        lax.while_loop(more, body, (jnp.int32(0), faded, cums))

    for g in tiles:
        rest_of_sweep(g, tuple(cum_left[2 * g:2 * g + 2]))
        o_ref[0, g * t:(g + 1) * t, :] = acc_ref[g].T


def _sb_prompt(q, kb16, vt, *, batch):
    n = q.shape[0]
    s = n // batch
    t = SB_TILE
    nq = s // t
    group = SB_GROUP if nq % SB_GROUP == 0 else 1
    hps = SB_WIDTH // LANES
    q3 = q.reshape(batch, s, SB_WIDTH)
    k3 = kb16.reshape(batch, s, SB_WIDTH)
    out = pl.pallas_call(
        functools.partial(_sb_prompt_kernel, t=t, nq=nq, group=group),
        grid=(batch, hps, nq // group),
        in_specs=[
            pl.BlockSpec((1, group * t, LANES), lambda b, hp, i: (b, i, hp)),
            pl.BlockSpec((1, s, LANES), lambda b, hp, i: (b, 0, hp)),
            pl.BlockSpec((1, 1, nq, LANES, t), lambda b, hp, i: (b, hp, 0, 0, 0)),
        ],
        out_specs=pl.BlockSpec((1, group * t, LANES), lambda b, hp, i: (b, i, hp)),
        out_shape=jax.ShapeDtypeStruct((batch, s, SB_WIDTH), F32),
        scratch_shapes=[pltpu.VMEM((group, LANES, t), F32), pltpu.VMEM((3, 2, t, t), F32),
                        pltpu.VMEM((3, 2, t, t), BF16), pltpu.VMEM((3, 2, t, t), BF16)],
        compiler_params=_params(("arbitrary", "arbitrary", "arbitrary")),
        name="sb_prompt",
    )(q3, k3, vt)
    return out.reshape(n, SB_WIDTH)


def _sb_sample_kernel(q_ref, kn_ref, vn_ref, ck_ref, cv_ref, o_ref, acc_ref, cum_ref, fade_ref,
                      *, tq, kb, span, nb):
    j = pl.program_id(1)
    hd = SB_HEAD_DIM
    heads = range(SB_HEADS)
    cols = [slice(h * hd, (h + 1) * hd) for h in heads]
    qs = [q_ref[:, c] for c in cols]

    def cumsum_right(s, m):
        s_hi, s_lo = _split_bf16(s)
        return _dot(s_hi, m) + _dot(s_lo, m)

    def sweep(zs, vts, m, cums, mask=None):
        ss = [_softplus2(z) for z in zs]
        if mask is not None:
            ss = [jnp.where(mask, s, 0.0) for s in ss]
        rs = [cumsum_right(s, m) for s in ss]
        ws = [jnp.exp2(jnp.minimum(zs[h] - rs[h], 0.0) + cums[h]) for h in heads]
        if mask is not None:
            ws = [jnp.where(mask, w, 0.0) for w in ws]
        outs = [_dot_nt(ws[h].astype(BF16), vts[h]) for h in heads]
        return outs, [cums[h] - rs[h][:, 0:1] for h in heads]

    @pl.when(j == 0)
    def _():
        qi = lax.broadcasted_iota(jnp.int32, (tq, tq), 0)
        ki = lax.broadcasted_iota(jnp.int32, (tq, tq), 1)
        valid = ki < qi
        tri_new = (qi >= ki).astype(BF16)
        zs = [_dot_nt(qs[h], kn_ref[:, cols[h]].astype(BF16)) for h in heads]
        vts = [vn_ref[:, cols[h]].astype(BF16).T for h in heads]
        outs, cums = sweep(zs, vts, tri_new, [jnp.zeros((tq, 1), F32)] * SB_HEADS, valid)
        for h in heads:
            acc_ref[h] = outs[h]
            cum_ref[h] = cums[h]
        fade_ref[0] = 0

    for blk in reversed(range(span // kb)):
        lanes = slice(blk * kb, (blk + 1) * kb)

        @pl.when(fade_ref[0] == 0)
        def _():
            a = lax.broadcasted_iota(jnp.int32, (kb, kb), 0)
            b = lax.broadcasted_iota(jnp.int32, (kb, kb), 1)
            tri = (a >= b).astype(BF16)
            zs = [_dot(qs[h], ck_ref[0, 0, h, :, lanes].astype(BF16)) for h in heads]
            vts = [cv_ref[0, 0, h, :, lanes].astype(BF16) for h in heads]
            outs, cums = sweep(zs, vts, tri, [cum_ref[h] for h in heads])
            for h in heads:
                acc_ref[h] += outs[h]
                cum_ref[h] = cums[h]
            faded = jnp.max(functools.reduce(jnp.maximum, cums)) < -UNDERFLOW_LOG2
            fade_ref[0] = faded.astype(jnp.int32)

    @pl.when(j == nb - 1)
    def _():
        for h in heads:
            o_ref[:, cols[h]] = acc_ref[h]


def _sb_sample(q, k_new, v_new, cache_kt, cache_vt, layer, *, batch):
    n = q.shape[0]
    tq = n // batch
    past = cache_kt.shape[-1]
    kb = min(MXU_DIM, past)
    span = min(SAMPLE_SPAN, past)
    nb = past // span
    assert past == nb * span and span % kb == 0
    row = pl.BlockSpec((tq, SB_WIDTH), lambda b, j: (b, 0))
    cache = pl.BlockSpec((1, 1, SB_HEADS, SB_HEAD_DIM, span), lambda b, j: (layer, b, 0, 0, nb - 1 - j))
    return pl.pallas_call(
        functools.partial(_sb_sample_kernel, tq=tq, kb=kb, span=span, nb=nb),
        grid=(batch, nb),
        in_specs=[row, row, row, cache, cache],
        out_specs=row,
        out_shape=jax.ShapeDtypeStruct((n, SB_WIDTH), F32),
        scratch_shapes=[pltpu.VMEM((SB_HEADS, tq, SB_HEAD_DIM), F32),
                        pltpu.VMEM((SB_HEADS, tq, 1), F32), pltpu.SMEM((1,), jnp.int32)],
        compiler_params=_params(("arbitrary", "arbitrary")),
        name="sb_sample",
    )(q, k_new, v_new, cache_kt, cache_vt)


def _ret_log_gamma(h):
    return math.log1p(-2.0 ** (-5 - h))


def _ret_kernel(q_ref, k_ref, v_ref, s0_ref, o_ref, sout_ref, state, dec, qdec, kdec,
                *, c, per, nsteps):
    seq = pl.program_id(0)
    ci = pl.program_id(1)

    @pl.when(jnp.logical_and(seq == 0, ci == 0))
    def _():
        li = lax.broadcasted_iota(jnp.int32, (c, c), 0)
        mi = lax.broadcasted_iota(jnp.int32, (c, c), 1)
        diff = (li - mi).astype(F32)
        pos = lax.broadcasted_iota(jnp.int32, (c, RET_HEAD_DIM), 0).astype(F32)
        for h in range(RET_HEADS):
            lg = _ret_log_gamma(h)
            dec[h] = jnp.where(diff >= 0.0, jnp.exp(jnp.maximum(diff, 0.0) * lg), 0.0)
            qdec[h] = jnp.exp((pos + 1.0) * lg)
            kdec[h] = jnp.exp((c - 1.0 - pos) * lg)

    @pl.when(ci == 0)
    def _():
        state[...] = s0_ref[0]

    heads = range(RET_HEADS)
    chunks = range(per)
    cols = [slice(h * RET_HEAD_DIM, (h + 1) * RET_HEAD_DIM) for h in heads]
    rows = [slice(a * c, (a + 1) * c) for a in chunks]
    qs = [[q_ref[rows[a], cols[h]] for h in heads] for a in chunks]
    ks = [[k_ref[rows[a], cols[h]] for h in heads] for a in chunks]
    vs = [[v_ref[rows[a], cols[h]] for h in heads] for a in chunks]
    scores = [[_dot_nt(qs[a][h], ks[a][h]) for h in heads] for a in chunks]
    kds = [[(ks[a][h].astype(F32) * kdec[h]).astype(BF16) for h in heads] for a in chunks]
    grown = [[_dot_tn(kds[a][h], vs[a][h]) for h in heads] for a in chunks]
    sts = [state[h] for h in heads]
    cross = []
    for a in chunks:
        cross.append([_dot(qs[a][h], sts[h].astype(BF16)) for h in heads])
        sts = [math.exp(c * _ret_log_gamma(h)) * sts[h] + grown[a][h] for h in heads]
    inner = [[_dot((scores[a][h] * dec[h]).astype(BF16), vs[a][h]) for h in heads] for a in chunks]
    for a in chunks:
        for h in heads:
            o_ref[rows[a], cols[h]] = inner[a][h] + cross[a][h] * qdec[h]
    for h in heads:
        state[h] = sts[h]

    @pl.when(ci == nsteps - 1)
    def _():
        sout_ref[0] = state[...]


def _retention(q, k, v, state0, *, batch):
    n = q.shape[0]
    t = n // batch
    c = min(RET_CHUNK, t)
    per = RET_GROUP if (t // c) % RET_GROUP == 0 else 1
    nsteps = t // (c * per)
    row = pl.BlockSpec((per * c, RET_WIDTH), lambda b, i: (b * nsteps + i, 0))
    st = pl.BlockSpec((1, RET_HEADS, RET_HEAD_DIM, RET_HEAD_DIM), lambda b, i: (b, 0, 0, 0))
    return pl.pallas_call(
        functools.partial(_ret_kernel, c=c, per=per, nsteps=nsteps),
        grid=(batch, nsteps),
        in_specs=[row, row, row, st],
        out_specs=[row, st],
        out_shape=[jax.ShapeDtypeStruct((n, RET_WIDTH), F32),
                   jax.ShapeDtypeStruct((batch, RET_HEADS, RET_HEAD_DIM, RET_HEAD_DIM), F32)],
        scratch_shapes=[pltpu.VMEM((RET_HEADS, RET_HEAD_DIM, RET_HEAD_DIM), F32),
                        pltpu.VMEM((RET_HEADS, c, c), F32),
                        pltpu.VMEM((RET_HEADS, c, RET_HEAD_DIM), F32),
                        pltpu.VMEM((RET_HEADS, c, RET_HEAD_DIM), F32)],
        compiler_params=_params(("arbitrary", "arbitrary")),
        name="retention",
    )(q, k, v, state0)


def _merge_ffn_kernel(x_ref, osb_ref, oret_ref, gate_ref, gt1_ref, sc2_ref, sh2_ref, gt2_ref,
                      gsb_ref, gret_ref, gffn_ref, gfin_ref, wout_ref, wfi_ref, wfo_ref,
                      *outs, final):
    xo_ref = outs[0]
    a = _rms(osb_ref[...], gsb_ref[...]).astype(BF16)
    mix = _dot(a, wout_ref[0:SB_WIDTH, :])
    gate = gate_ref[...]
    gate = gate * jax.nn.sigmoid(gate)
    for h in range(RET_HEADS):
        sl = slice(h * RET_HEAD_DIM, (h + 1) * RET_HEAD_DIM)
        r = (_rms(oret_ref[:, sl], gret_ref[:, sl]) * gate[:, sl]).astype(BF16)
        mix += _dot(r, wout_ref[SB_WIDTH + h * RET_HEAD_DIM:SB_WIDTH + (h + 1) * RET_HEAD_DIM, :])
    x1 = x_ref[...] + gt1_ref[0] * mix
    h2 = (_rms(x1, gffn_ref[...]) * (1.0 + sc2_ref[0]) + sh2_ref[0]).astype(BF16)
    ff = jnp.zeros_like(x1)
    for j in range(D_FF // FF_CHUNK):
        cols = slice(j * FF_CHUNK, (j + 1) * FF_CHUNK)
        ucols = slice(D_FF + j * FF_CHUNK, D_FF + (j + 1) * FF_CHUNK)
        g = _dot(h2, wfi_ref[:, cols])
        u = _dot(h2, wfi_ref[:, ucols])
        act = (g * jax.nn.sigmoid(g) * u).astype(BF16)
        ff += _dot(act, wfo_ref[cols, :])
    x2 = x1 + gt2_ref[0] * ff
    xo_ref[...] = x2
    if final:
        outs[1][...] = _rms(x2, gfin_ref[...])


def _merge_ffn(x, o_sb, o_ret, gate, gt1, sc2, sh2, gt2, g_sb, g_ret, g_ffn, g_fin,
               w_out_b, w_fi_b, w_fo_b, *, batch, final):
    n = x.shape[0]
    tm = min(TOKEN_TILE, n)
    tiles = n // tm
    per_seq = max(tiles // batch, 1)
    mrows = gt1.shape[1]

    def row(i):
        return (i, 0)

    def mod(i):
        return (i // per_seq if mrows == 1 else i, 0, 0)

    full = pl.BlockSpec((tm, D_MODEL), row)
    half = pl.BlockSpec((tm, SB_WIDTH), row)
    mspec = pl.BlockSpec((1, mrows, D_MODEL), mod)
    n_out = 2 if final else 1
    res = pl.pallas_call(
        functools.partial(_merge_ffn_kernel, final=final),
        grid=(tiles,),
        in_specs=[full, half, half, half, mspec, mspec, mspec, mspec,
                  _resident((1, SB_WIDTH)), _resident((1, RET_WIDTH)),
                  _resident((1, D_MODEL)), _resident((1, D_MODEL)),
                  _resident((D_MODEL, D_MODEL)), _resident((D_MODEL, 2 * D_FF)),
                  _resident((D_FF, D_MODEL))],
        out_specs=[full] * n_out,
        out_shape=[jax.ShapeDtypeStruct((n, D_MODEL), F32)] * n_out,
        compiler_params=_params(("arbitrary",)),
        name="merge_ffn",
    )(x, o_sb, o_ret, gate, gt1, sc2, sh2, gt2, g_sb, g_ret, g_ffn, g_fin, w_out_b, w_fi_b, w_fo_b)
    return res if final else (res[0], None)


def kernel(x_prompt, x_sample, cache_sb_k, cache_sb_v, state_ret, c_prompt, c_sample,
           g_norm_mix, g_norm_ffn, w_ada, b_ada, w_in, g_sb_out, g_ret_out, w_out,
           w_ff_in, w_ff_out, g_final):
    depth = w_in.shape[0]
    bp, sp, _ = x_prompt.shape
    bs, ts, _ = x_sample.shape
    past = cache_sb_k.shape[2]
    n_p = bp * sp
    n_s = bs * ts

    c_all = jnp.concatenate([c_prompt, c_sample], axis=0)
    pad = (-c_all.shape[0]) % 16
    mods = _modulation(jnp.pad(c_all, ((0, pad), (0, 0))), w_ada, b_ada)
    mods = mods.reshape(depth, -1, N_MOD, D_MODEL)
    mods_p = mods[:, :bp]
    mods_s = jnp.repeat(mods[:, bp:bp + bs], ts, axis=1)

    cos_p, sin_p = _rope_tables(sp, 0)
    cos_s, sin_s = _rope_tables(ts, past)
    cos_s = jnp.tile(cos_s, (bs, 1))
    sin_s = jnp.tile(sin_s, (bs, 1))

    cache_k = jnp.transpose(cache_sb_k, (0, 1, 3, 4, 2))
    cache_v = jnp.transpose(cache_sb_v, (0, 1, 3, 4, 2))
    zero_state = jnp.zeros((bp, RET_HEADS, RET_HEAD_DIM, RET_HEAD_DIM), F32)

    xp = x_prompt.reshape(n_p, D_MODEL)
    xs = x_sample.reshape(n_s, D_MODEL)
    kv_p = (jnp.zeros((depth, bp, SB_HEADS, SB_HEAD_DIM, sp), F32),) * 2
    rp, ks, vs, rs = [], [], [], []
    yp = ys = None
    for l in range(depth):
        final = l == depth - 1
        w_in_b = w_in[l].astype(BF16)
        wkvt_b = w_in[l][:, SB_WIDTH:3 * SB_WIDTH].T.astype(BF16)
        w_out_b = w_out[l].astype(BF16)
        w_fi_b = w_ff_in[l].astype(BF16)
        w_fo_b = w_ff_out[l].astype(BF16)
        g_mix = g_norm_mix[l].reshape(1, D_MODEL)
        g_ffn = g_norm_ffn[l].reshape(1, D_MODEL)
        g_sb = g_sb_out[l].reshape(1, SB_WIDTH)
        g_ret = g_ret_out[l].reshape(1, RET_WIDTH)
        g_fin = g_final.reshape(1, D_MODEL)

        m = [mods_p[l, :, j].reshape(bp, 1, D_MODEL) for j in range(N_MOD)]
        q, k, k16, v, vt, qr, kr, vr, gate = _proj(
            xp, m[1], m[0], g_mix, cos_p, sin_p, w_in_b, wkvt_b, kv_p, l, batch=bp, prompt=True)
        kv_p = (k, v)
        o_sb = _sb_prompt(q, k16, vt, batch=bp)
        o_ret, st = _retention(qr, kr, vr, zero_state, batch=bp)
        xp, yp = _merge_ffn(xp, o_sb, o_ret, gate, m[2], m[4], m[3], m[5], g_sb, g_ret, g_ffn,
                            g_fin, w_out_b, w_fi_b, w_fo_b, batch=bp, final=final)
        rp.append(st)

        m = [mods_s[l, :, j].reshape(1, n_s, D_MODEL) for j in range(N_MOD)]
        q, k, v, qr, kr, vr, gate = _proj(
            xs, m[1], m[0], g_mix, cos_s, sin_s, w_in_b, batch=bs, prompt=False)
        o_sb = _sb_sample(q, k, v, cache_k, cache_v, l, batch=bs)
        o_ret, st = _retention(qr, kr, vr, state_ret[l], batch=bs)
        xs, ys = _merge_ffn(xs, o_sb, o_ret, gate, m[2], m[4], m[3], m[5], g_sb, g_ret, g_ffn,
                            g_fin, w_out_b, w_fi_b, w_fo_b, batch=bs, final=final)
        ks.append(k)
        vs.append(v)
        rs.append(st)

    def heads(a, b, t):
        return a.reshape(depth, b, t, SB_HEADS, SB_HEAD_DIM)

    def rows(a):
        return jnp.transpose(a, (0, 1, 4, 2, 3))

    return (yp.reshape(bp, sp, D_MODEL), ys.reshape(bs, ts, D_MODEL),
            rows(kv_p[0]), rows(kv_p[1]), jnp.stack(rp),
            heads(jnp.stack(ks), bs, ts), heads(jnp.stack(vs), bs, ts), jnp.stack(rs))
```

```python
import functools
import math

import jax
import jax.numpy as jnp
from jax import lax
from jax.experimental import pallas as pl
from jax.experimental.pallas import tpu as pltpu

F32 = jnp.float32
BF16 = jnp.bfloat16

D_MODEL = 1024
SB_WIDTH = 512
RET_WIDTH = 512
SB_HEAD_DIM = 64
SB_HEADS = SB_WIDTH // SB_HEAD_DIM
RET_HEAD_DIM = 128
RET_HEADS = RET_WIDTH // RET_HEAD_DIM
IN_WIDTH = 3 * SB_WIDTH + 4 * RET_WIDTH
D_FF = 2816
N_MOD = 6
ROPE_BASE = 10000.0
EPS = 1e-6
LOG2E = 1.4426950408889634

LANES = 128
MXU_DIM = 256
VMEM_LIMIT_BYTES = 56 * 1024 * 1024

TOKEN_TILE = 512
SB_TILE = MXU_DIM
SB_GROUP = 4
SB_UNROLL = 3
SAMPLE_SPAN = 2048
RET_CHUNK = MXU_DIM
RET_GROUP = 4
FF_CHUNK = MXU_DIM
SOFTPLUS_CLAMP = 30.0
SHIFT_OUT = 1e30
UNDERFLOW_LOG2 = 160.0


def _dot(a, b):
    return jnp.dot(a, b, preferred_element_type=F32)


def _dot_nt(a, b):
    return lax.dot_general(a, b, (((1,), (1,)), ((), ())), preferred_element_type=F32)


def _dot_tn(a, b):
    return lax.dot_general(a, b, (((0,), (0,)), ((), ())), preferred_element_type=F32)


def _split_bf16(x):
    hi = x.astype(BF16)
    lo = (x - hi.astype(F32)).astype(BF16)
    return hi, lo


def _rms(x, g):
    return x * lax.rsqrt(jnp.mean(x * x, axis=-1, keepdims=True) + EPS) * g


def _softplus2(z2):
    e = jnp.exp2(jnp.minimum(z2, SOFTPLUS_CLAMP))
    return jnp.maximum(z2, jnp.log(1.0 + e) * LOG2E)


def _params(semantics, flags=None):
    return pltpu.CompilerParams(dimension_semantics=semantics, vmem_limit_bytes=VMEM_LIMIT_BYTES,
                                flags=flags)


def _resident(shape):
    zeros = (0,) * len(shape)
    return pl.BlockSpec(shape, lambda *_: zeros, pipeline_mode=pl.Buffered(1))


def _mod_kernel(c_ref, w_ref, b_ref, o_ref):
    c = c_ref[...]
    a = c * jax.nn.sigmoid(c)
    a_hi, a_lo = _split_bf16(a)
    w_hi, w_lo = _split_bf16(w_ref[0])
    o_ref[0] = _dot(a_hi, w_hi) + _dot(a_lo, w_hi) + _dot(a_hi, w_lo) + b_ref[0]


def _modulation(c, w_ada, b_ada):
    depth = w_ada.shape[0]
    rows = c.shape[0]
    return pl.pallas_call(
        _mod_kernel,
        grid=(depth, N_MOD),
        in_specs=[
            pl.BlockSpec((rows, D_MODEL), lambda l, j: (0, 0)),
            pl.BlockSpec((1, D_MODEL, D_MODEL), lambda l, j: (l, 0, j)),
            pl.BlockSpec((1, 1, D_MODEL), lambda l, j: (l, 0, j)),
        ],
        out_specs=pl.BlockSpec((1, rows, D_MODEL), lambda l, j: (l, 0, j)),
        out_shape=jax.ShapeDtypeStruct((depth, rows, N_MOD * D_MODEL), F32),
        compiler_params=_params(("arbitrary", "arbitrary")),
        name="modulation",
    )(c, w_ada, b_ada.reshape(depth, 1, N_MOD * D_MODEL))


def _rope_kernel(inv_ref, cos_ref, sin_ref, *, rows, pos0):
    row = lax.broadcasted_iota(jnp.int32, (rows, LANES), 0) + (pl.program_id(0) * rows + pos0)
    lane = lax.broadcasted_iota(jnp.int32, (rows, LANES), 1)
    ang = row.astype(F32) * inv_ref[...]
    sin = jnp.sin(ang)
    cos_ref[...] = jnp.cos(ang)
    sin_ref[...] = jnp.where(lane < RET_HEAD_DIM // 2, -sin, sin)


def _rope_tables(n, pos0):
    half = RET_HEAD_DIM // 2
    inv = ROPE_BASE ** (-jnp.arange(half, dtype=F32) / half)
    inv2 = jnp.concatenate([inv, inv]).reshape(1, LANES)
    rows = min(n, 2048)
    assert n % rows == 0
    return pl.pallas_call(
        functools.partial(_rope_kernel, rows=rows, pos0=pos0),
        grid=(n // rows,),
        in_specs=[pl.BlockSpec((1, LANES), lambda i: (0, 0))],
        out_specs=[pl.BlockSpec((rows, LANES), lambda i: (i, 0))] * 2,
        out_shape=[jax.ShapeDtypeStruct((n, LANES), F32)] * 2,
        compiler_params=_params(("arbitrary",)),
        name="rope_tables",
    )(inv2)


def _proj_kernel(x_ref, sc_ref, sh_ref, g_ref, cos_ref, sin_ref, w_ref, *rest, tm, kb, prompt):
    if prompt:
        wkvt_ref, _, _, q_ref, kt_ref, kb_ref, vtf_ref, vt_ref, qr_ref, kr_ref, vr_ref, gate_ref = rest
    else:
        q_ref, k_ref, v_ref, qr_ref, kr_ref, vr_ref, gate_ref = rest
    h = _rms(x_ref[...], g_ref[...]) * (1.0 + sc_ref[0]) + sh_ref[0]
    hb = h.astype(BF16)

    def seg(a, b):
        return _dot(hb, w_ref[:, a:b])

    q_ref[...] = (seg(0, SB_WIDTH) * (SB_HEAD_DIM ** -0.5 * LOG2E)).astype(BF16)
    k = seg(SB_WIDTH, 2 * SB_WIDTH)
    if prompt:
        kb_ref[...] = k.astype(BF16)
        kvt = _dot_nt(wkvt_ref[...], hb)
        kt_ref[0, 0] = kvt[:SB_WIDTH].reshape(SB_HEADS, SB_HEAD_DIM, tm)
        vtf_ref[0, 0] = kvt[SB_WIDTH:].reshape(SB_HEADS, SB_HEAD_DIM, tm)
        vt = kvt[SB_WIDTH:].astype(BF16)
        for hp in range(SB_WIDTH // LANES):
            for j in range(tm // kb):
                vt_ref[0, hp, j] = vt[hp * LANES:(hp + 1) * LANES, j * kb:(j + 1) * kb]
    else:
        k_ref[...] = k
        v_ref[...] = seg(2 * SB_WIDTH, 3 * SB_WIDTH)
    cos = cos_ref[...]
    sin = sin_ref[...]
    base = 3 * SB_WIDTH
    qr = seg(base, base + RET_WIDTH)
    kr = seg(base + RET_WIDTH, base + 2 * RET_WIDTH)
    for hh in range(RET_HEADS):
        sl = slice(hh * RET_HEAD_DIM, (hh + 1) * RET_HEAD_DIM)
        qh = qr[:, sl]
        kh = kr[:, sl]
        qr_ref[:, sl] = (qh * cos + pltpu.roll(qh, RET_HEAD_DIM // 2, 1) * sin).astype(BF16)
        kr_ref[:, sl] = ((kh * cos + pltpu.roll(kh, RET_HEAD_DIM // 2, 1) * sin)
                         * RET_HEAD_DIM ** -0.5).astype(BF16)
    vr_ref[...] = seg(base + 2 * RET_WIDTH, base + 3 * RET_WIDTH).astype(BF16)
    gate_ref[...] = seg(base + 3 * RET_WIDTH, base + 4 * RET_WIDTH)


def _proj(x, sc, sh, g, cos, sin, w_in_b, wkvt_b=None, kv_out=None, layer=0, *, batch, prompt):
    n = x.shape[0]
    tm = min(TOKEN_TILE, n)
    tiles = n // tm
    per_seq = max(tiles // batch, 1)
    pos_tiles = cos.shape[0] // tm
    kb = SB_TILE
    mrows = sc.shape[1]

    def row(i):
        return (i, 0)

    def mod(i):
        return (i // per_seq if mrows == 1 else i, 0, 0)

    def pos(i):
        return (i % pos_tiles, 0)

    in_specs = [
        pl.BlockSpec((tm, D_MODEL), row),
        pl.BlockSpec((1, mrows, D_MODEL), mod),
        pl.BlockSpec((1, mrows, D_MODEL), mod),
        _resident((1, D_MODEL)),
        pl.BlockSpec((tm, LANES), pos),
        pl.BlockSpec((tm, LANES), pos),
        _resident((D_MODEL, IN_WIDTH)),
    ]
    args = [x, sc, sh, g, cos, sin, w_in_b]
    wide_b = jax.ShapeDtypeStruct((n, SB_WIDTH), BF16)
    wide_f = jax.ShapeDtypeStruct((n, SB_WIDTH), F32)
    wide = pl.BlockSpec((tm, SB_WIDTH), row)
    if prompt:
        in_specs += [_resident((2 * SB_WIDTH, D_MODEL)), pl.BlockSpec(memory_space=pl.ANY),
                     pl.BlockSpec(memory_space=pl.ANY)]
        aliases = {len(args) + 1: 1, len(args) + 2: 3}
        args += [wkvt_b, *kv_out]
        nkb = n // batch // kb
        hps = SB_WIDTH // LANES
        head_f = jax.ShapeDtypeStruct(kv_out[0].shape, F32)
        head = pl.BlockSpec((1, 1, SB_HEADS, SB_HEAD_DIM, tm),
                            lambda i: (layer, i // per_seq, 0, 0, i % per_seq))
        out_shape = [wide_b, head_f, wide_b, head_f,
                     jax.ShapeDtypeStruct((batch, hps, nkb, LANES, kb), BF16),
                     wide_b, wide_b, wide_b, wide_f]
        out_specs = [wide, head, wide, head,
                     pl.BlockSpec((1, hps, tm // kb, LANES, kb),
                                  lambda i: (i // per_seq, 0, i % per_seq, 0, 0)),
                     wide, wide, wide, wide]
    else:
        aliases = {}
        out_shape = [wide_b, wide_f, wide_f, wide_b, wide_b, wide_b, wide_f]
        out_specs = [wide] * 7
    return pl.pallas_call(
        functools.partial(_proj_kernel, tm=tm, kb=kb, prompt=prompt),
        grid=(tiles,),
        in_specs=in_specs,
        out_specs=out_specs,
        out_shape=out_shape,
        input_output_aliases=aliases,
        compiler_params=_params(("arbitrary",)),
        name="proj_prompt" if prompt else "proj_sample",
    )(*args)


def _sb_prompt_kernel(q_ref, k_ref, vt_ref, o_ref, acc_ref, zt_ref, s_ref, w_ref, *, t, nq, group):
    first_tile = pl.program_id(2) * group
    hd = SB_HEAD_DIM
    heads = range(2)
    tiles = range(group)
    chains = [(g, h) for g in tiles for h in heads]
    lane = lax.broadcasted_iota(jnp.int32, (t, LANES), 1)
    qms = []
    for g in tiles:
        q = q_ref[0, g * t:(g + 1) * t, :]
        qms.append((jnp.where(lane < hd, q, jnp.zeros_like(q)),
                    jnp.where(lane >= hd, q, jnp.zeros_like(q))))
    kk = lax.broadcasted_iota(jnp.int32, (t, t), 0)
    qq = lax.broadcasted_iota(jnp.int32, (t, t), 1)
    tri = (qq >= kk).astype(BF16)
    valid = kk < qq

    def keys(j):
        return k_ref[0, pl.ds(pl.multiple_of(j * t, t), t), :]

    def gather(j, ws):
        vblk = vt_ref[0, 0, j]
        return [_dot(vblk[h * hd:(h + 1) * hd, :], ws[h]) for h in heads]

    def weights(zt, r, carried):
        return jnp.exp2(jnp.minimum(zt - r, 0.0) + carried)

    diag = [first_tile + g for g in tiles]
    left = [jnp.maximum(diag[g] - 1, 0) for g in tiles]
    shift_left = [jnp.where(diag[g] > 0, 0.0, -SHIFT_OUT) for g in tiles]
    k_diag = [keys(diag[g]) for g in tiles]
    k_left = [keys(left[g]) for g in tiles]
    z_diag = [_dot_nt(k_diag[g], qms[g][h]) for g, h in chains]
    z_left = [_dot_nt(k_left[g], qms[g][h]) for g, h in chains]
    s_diag = [jnp.where(valid, _softplus2(z), 0.0).astype(BF16) for z in z_diag]
    s_left = [_softplus2(z).astype(BF16) for z in z_left]
    r_diag = [_dot(tri, s) for s in s_diag]
    r_left = [_dot(tri, s) for s in s_left]
    cum_diag = [-r[0:1, :] for r in r_diag]
    w_diag = [jnp.where(valid, weights(z_diag[c], r_diag[c], 0.0), 0.0).astype(BF16)
              for c in range(len(chains))]
    w_left = [weights(z_left[c], r_left[c], cum_diag[c] + shift_left[chains[c][0]]).astype(BF16)
              for c in range(len(chains))]
    cum_left = [cum_diag[c] - r_left[c][0:1, :] for c in range(len(chains))]
    for g in tiles:
        p_diag = gather(diag[g], w_diag[2 * g:2 * g + 2])
        p_left = gather(left[g], w_left[2 * g:2 * g + 2])
        for h in heads:
            acc_ref[g, h * hd:(h + 1) * hd, :] = p_diag[h] + p_left[h]

    def all_faded(cums):
        return jnp.max(jnp.maximum(cums[0], cums[1])) < -UNDERFLOW_LOG2

    def rest_of_sweep(g, cums):
        i = diag[g]
        qm = qms[g]

        def block_of(m):
            return jnp.clip(i - 2 - m, 0, nq - 1)

        def scores(j):
            kblk = keys(j)
            return [_dot_nt(kblk, qm[h]) for h in heads]

        rounds = jnp.where(i > 1, (i + SB_UNROLL) // SB_UNROLL, 0)
        faded = all_faded(cums)

        @pl.when(jnp.logical_and(rounds > 0, jnp.logical_not(faded)))
        def _():
            first = scores(block_of(0))
            for h in heads:
                zt_ref[0, h] = first[h]
                zt_ref[2, h] = jnp.zeros((t, t), F32)
                s_ref[2, h] = jnp.zeros((t, t), BF16)
                w_ref[1, h] = jnp.zeros((t, t), BF16)

        def trip(n, r, cums):
            nxt_slot, cur, old, done = (r + 1) % 3, r, (r + 2) % 3, (r + 1) % 3
            shift = jnp.where(jnp.logical_and(n >= 1, n < i), 0.0, -SHIFT_OUT)
            rs = [_dot(tri, s_ref[old, h]) for h in heads]
            new_ws = [weights(zt_ref[old, h], rs[h], cums[h] + shift) for h in heads]
            new_cums = [cums[h] - rs[h][0:1, :] for h in heads]
            nxt = scores(block_of(n + 1))
            parts = gather(block_of(n - 2), [w_ref[done, h] for h in heads])
            for h in heads:
                s_ref[cur, h] = _softplus2(zt_ref[cur, h]).astype(BF16)
            for h in heads:
                acc_ref[g, h * hd:(h + 1) * hd, :] += parts[h]
                w_ref[old, h] = new_ws[h].astype(BF16)
                zt_ref[nxt_slot, h] = nxt[h]
            return tuple(new_cums)

        def more(state):
            p, faded, _ = state
            return jnp.logical_and(p < rounds, jnp.logical_not(faded))

        def body(state):
            p, faded, cums = state
            for r in range(SB_UNROLL):
                cums = trip(SB_UNROLL * p + r, r % 3, cums)
                if r == SB_UNROLL - 2:
                    faded = all_faded(cums)
            return p + 1, faded, cums

        lax.while_loop(more, body, (jnp.int32(0), faded, cums))

    for g in tiles:
        rest_of_sweep(g, tuple(cum_left[2 * g:2 * g + 2]))
        o_ref[0, g * t:(g + 1) * t, :] = acc_ref[g].T


def _sb_prompt(q, kb16, vt, *, batch):
    n = q.shape[0]
    s = n // batch
    t = SB_TILE
    nq = s // t
    group = SB_GROUP if nq % SB_GROUP == 0 else 1
    hps = SB_WIDTH // LANES
    q3 = q.reshape(batch, s, SB_WIDTH)
    k3 = kb16.reshape(batch, s, SB_WIDTH)
    out = pl.pallas_call(
        functools.partial(_sb_prompt_kernel, t=t, nq=nq, group=group),
        grid=(batch, hps, nq // group),
        in_specs=[
            pl.BlockSpec((1, group * t, LANES), lambda b, hp, i: (b, i, hp)),
            pl.BlockSpec((1, s, LANES), lambda b, hp, i: (b, 0, hp)),
            pl.BlockSpec((1, 1, nq, LANES, t), lambda b, hp, i: (b, hp, 0, 0, 0)),
        ],
        out_specs=pl.BlockSpec((1, group * t, LANES), lambda b, hp, i: (b, i, hp)),
        out_shape=jax.ShapeDtypeStruct((batch, s, SB_WIDTH), F32),
        scratch_shapes=[pltpu.VMEM((group, LANES, t), F32), pltpu.VMEM((3, 2, t, t), F32),
                        pltpu.VMEM((3, 2, t, t), BF16), pltpu.VMEM((3, 2, t, t), BF16)],
        compiler_params=_params(("arbitrary", "arbitrary", "arbitrary")),
        name="sb_prompt",
    )(q3, k3, vt)
    return out.reshape(n, SB_WIDTH)


def _sb_sample_kernel(q_ref, kn_ref, vn_ref, ck_ref, cv_ref, o_ref, acc_ref, cum_ref, fade_ref,
                      *, tq, kb, span, nb):
    j = pl.program_id(1)
    hd = SB_HEAD_DIM
    heads = range(SB_HEADS)
    cols = [slice(h * hd, (h + 1) * hd) for h in heads]
    qs = [q_ref[:, c] for c in cols]

    def cumsum_right(s, m):
        s_hi, s_lo = _split_bf16(s)
        return _dot(s_hi, m) + _dot(s_lo, m)

    def sweep(zs, vts, m, cums, mask=None):
        ss = [_softplus2(z) for z in zs]
        if mask is not None:
            ss = [jnp.where(mask, s, 0.0) for s in ss]
        rs = [cumsum_right(s, m) for s in ss]
        ws = [jnp.exp2(jnp.minimum(zs[h] - rs[h], 0.0) + cums[h]) for h in heads]
        if mask is not None:
            ws = [jnp.where(mask, w, 0.0) for w in ws]
        outs = [_dot_nt(ws[h].astype(BF16), vts[h]) for h in heads]
        return outs, [cums[h] - rs[h][:, 0:1] for h in heads]

    @pl.when(j == 0)
    def _():
        qi = lax.broadcasted_iota(jnp.int32, (tq, tq), 0)
        ki = lax.broadcasted_iota(jnp.int32, (tq, tq), 1)
        valid = ki < qi
        tri_new = (qi >= ki).astype(BF16)
        zs = [_dot_nt(qs[h], kn_ref[:, cols[h]].astype(BF16)) for h in heads]
        vts = [vn_ref[:, cols[h]].astype(BF16).T for h in heads]
        outs, cums = sweep(zs, vts, tri_new, [jnp.zeros((tq, 1), F32)] * SB_HEADS, valid)
        for h in heads:
            acc_ref[h] = outs[h]
            cum_ref[h] = cums[h]
        fade_ref[0] = 0

    for blk in reversed(range(span // kb)):
        lanes = slice(blk * kb, (blk + 1) * kb)

        @pl.when(fade_ref[0] == 0)
        def _():
            a = lax.broadcasted_iota(jnp.int32, (kb, kb), 0)
            b = lax.broadcasted_iota(jnp.int32, (kb, kb), 1)
            tri = (a >= b).astype(BF16)
            zs = [_dot(qs[h], ck_ref[0, 0, h, :, lanes].astype(BF16)) for h in heads]
            vts = [cv_ref[0, 0, h, :, lanes].astype(BF16) for h in heads]
            outs, cums = sweep(zs, vts, tri, [cum_ref[h] for h in heads])
            for h in heads:
                acc_ref[h] += outs[h]
                cum_ref[h] = cums[h]
            faded = jnp.max(functools.reduce(jnp.maximum, cums)) < -UNDERFLOW_LOG2
            fade_ref[0] = faded.astype(jnp.int32)

    @pl.when(j == nb - 1)
    def _():
        for h in heads:
            o_ref[:, cols[h]] = acc_ref[h]


def _sb_sample(q, k_new, v_new, cache_kt, cache_vt, layer, *, batch):
    n = q.shape[0]
    tq = n // batch
    past = cache_kt.shape[-1]
    kb = min(MXU_DIM, past)
    span = min(SAMPLE_SPAN, past)
    nb = past // span
    assert past == nb * span and span % kb == 0
    row = pl.BlockSpec((tq, SB_WIDTH), lambda b, j: (b, 0))
    cache = pl.BlockSpec((1, 1, SB_HEADS, SB_HEAD_DIM, span), lambda b, j: (layer, b, 0, 0, nb - 1 - j))
    return pl.pallas_call(
        functools.partial(_sb_sample_kernel, tq=tq, kb=kb, span=span, nb=nb),
        grid=(batch, nb),
        in_specs=[row, row, row, cache, cache],
        out_specs=row,
        out_shape=jax.ShapeDtypeStruct((n, SB_WIDTH), F32),
        scratch_shapes=[pltpu.VMEM((SB_HEADS, tq, SB_HEAD_DIM), F32),
                        pltpu.VMEM((SB_HEADS, tq, 1), F32), pltpu.SMEM((1,), jnp.int32)],
        compiler_params=_params(("arbitrary", "arbitrary")),
        name="sb_sample",
    )(q, k_new, v_new, cache_kt, cache_vt)


def _ret_log_gamma(h):
    return math.log1p(-2.0 ** (-5 - h))


def _ret_kernel(q_ref, k_ref, v_ref, s0_ref, o_ref, sout_ref, state, dec, qdec, kdec,
                *, c, per, nsteps):
    seq = pl.program_id(0)
    ci = pl.program_id(1)

    @pl.when(jnp.logical_and(seq == 0, ci == 0))
    def _():
        li = lax.broadcasted_iota(jnp.int32, (c, c), 0)
        mi = lax.broadcasted_iota(jnp.int32, (c, c), 1)
        diff = (li - mi).astype(F32)
        pos = lax.broadcasted_iota(jnp.int32, (c, RET_HEAD_DIM), 0).astype(F32)
        for h in range(RET_HEADS):
            lg = _ret_log_gamma(h)
            dec[h] = jnp.where(diff >= 0.0, jnp.exp(jnp.maximum(diff, 0.0) * lg), 0.0)
            qdec[h] = jnp.exp((pos + 1.0) * lg)
            kdec[h] = jnp.exp((c - 1.0 - pos) * lg)

    @pl.when(ci == 0)
    def _():
        state[...] = s0_ref[0]

    heads = range(RET_HEADS)
    chunks = range(per)
    cols = [slice(h * RET_HEAD_DIM, (h + 1) * RET_HEAD_DIM) for h in heads]
    rows = [slice(a * c, (a + 1) * c) for a in chunks]
    qs = [[q_ref[rows[a], cols[h]] for h in heads] for a in chunks]
    ks = [[k_ref[rows[a], cols[h]] for h in heads] for a in chunks]
    vs = [[v_ref[rows[a], cols[h]] for h in heads] for a in chunks]
    scores = [[_dot_nt(qs[a][h], ks[a][h]) for h in heads] for a in chunks]
    kds = [[(ks[a][h].astype(F32) * kdec[h]).astype(BF16) for h in heads] for a in chunks]
    grown = [[_dot_tn(kds[a][h], vs[a][h]) for h in heads] for a in chunks]
    sts = [state[h] for h in heads]
    cross = []
    for a in chunks:
        cross.append([_dot(qs[a][h], sts[h].astype(BF16)) for h in heads])
        sts = [math.exp(c * _ret_log_gamma(h)) * sts[h] + grown[a][h] for h in heads]
    inner = [[_dot((scores[a][h] * dec[h]).astype(BF16), vs[a][h]) for h in heads] for a in chunks]
    for a in chunks:
        for h in heads:
            o_ref[rows[a], cols[h]] = inner[a][h] + cross[a][h] * qdec[h]
    for h in heads:
        state[h] = sts[h]

    @pl.when(ci == nsteps - 1)
    def _():
        sout_ref[0] = state[...]


def _retention(q, k, v, state0, *, batch):
    n = q.shape[0]
    t = n // batch
    c = min(RET_CHUNK, t)
    per = RET_GROUP if (t // c) % RET_GROUP == 0 else 1
    nsteps = t // (c * per)
    row = pl.BlockSpec((per * c, RET_WIDTH), lambda b, i: (b * nsteps + i, 0))
    st = pl.BlockSpec((1, RET_HEADS, RET_HEAD_DIM, RET_HEAD_DIM), lambda b, i: (b, 0, 0, 0))
    return pl.pallas_call(
        functools.partial(_ret_kernel, c=c, per=per, nsteps=nsteps),
        grid=(batch, nsteps),
        in_specs=[row, row, row, st],
        out_specs=[row, st],
        out_shape=[jax.ShapeDtypeStruct((n, RET_WIDTH), F32),
                   jax.ShapeDtypeStruct((batch, RET_HEADS, RET_HEAD_DIM, RET_HEAD_DIM), F32)],
        scratch_shapes=[pltpu.VMEM((RET_HEADS, RET_HEAD_DIM, RET_HEAD_DIM), F32),
                        pltpu.VMEM((RET_HEADS, c, c), F32),
                        pltpu.VMEM((RET_HEADS, c, RET_HEAD_DIM), F32),
                        pltpu.VMEM((RET_HEADS, c, RET_HEAD_DIM), F32)],
        compiler_params=_params(("arbitrary", "arbitrary")),
        name="retention",
    )(q, k, v, state0)


def _merge_ffn_kernel(x_ref, osb_ref, oret_ref, gate_ref, gt1_ref, sc2_ref, sh2_ref, gt2_ref,
                      gsb_ref, gret_ref, gffn_ref, gfin_ref, wout_ref, wfi_ref, wfo_ref,
                      *outs, final):
    xo_ref = outs[0]
    a = _rms(osb_ref[...], gsb_ref[...]).astype(BF16)
    mix = _dot(a, wout_ref[0:SB_WIDTH, :])
    gate = gate_ref[...]
    gate = gate * jax.nn.sigmoid(gate)
    for h in range(RET_HEADS):
        sl = slice(h * RET_HEAD_DIM, (h + 1) * RET_HEAD_DIM)
        r = (_rms(oret_ref[:, sl], gret_ref[:, sl]) * gate[:, sl]).astype(BF16)
        mix += _dot(r, wout_ref[SB_WIDTH + h * RET_HEAD_DIM:SB_WIDTH + (h + 1) * RET_HEAD_DIM, :])
    x1 = x_ref[...] + gt1_ref[0] * mix
    h2 = (_rms(x1, gffn_ref[...]) * (1.0 + sc2_ref[0]) + sh2_ref[0]).astype(BF16)
    ff = jnp.zeros_like(x1)
    for j in range(D_FF // FF_CHUNK):
        cols = slice(j * FF_CHUNK, (j + 1) * FF_CHUNK)
        ucols = slice(D_FF + j * FF_CHUNK, D_FF + (j + 1) * FF_CHUNK)
        g = _dot(h2, wfi_ref[:, cols])
        u = _dot(h2, wfi_ref[:, ucols])
        act = (g * jax.nn.sigmoid(g) * u).astype(BF16)
        ff += _dot(act, wfo_ref[cols, :])
    x2 = x1 + gt2_ref[0] * ff
    xo_ref[...] = x2
    if final:
        outs[1][...] = _rms(x2, gfin_ref[...])


def _merge_ffn(x, o_sb, o_ret, gate, gt1, sc2, sh2, gt2, g_sb, g_ret, g_ffn, g_fin,
               w_out_b, w_fi_b, w_fo_b, *, batch, final):
    n = x.shape[0]
    tm = min(TOKEN_TILE, n)
    tiles = n // tm
    per_seq = max(tiles // batch, 1)
    mrows = gt1.shape[1]

    def row(i):
        return (i, 0)

    def mod(i):
        return (i // per_seq if mrows == 1 else i, 0, 0)

    full = pl.BlockSpec((tm, D_MODEL), row)
    half = pl.BlockSpec((tm, SB_WIDTH), row)
    mspec = pl.BlockSpec((1, mrows, D_MODEL), mod)
    n_out = 2 if final else 1
    res = pl.pallas_call(
        functools.partial(_merge_ffn_kernel, final=final),
        grid=(tiles,),
        in_specs=[full, half, half, half, mspec, mspec, mspec, mspec,
                  _resident((1, SB_WIDTH)), _resident((1, RET_WIDTH)),
                  _resident((1, D_MODEL)), _resident((1, D_MODEL)),
                  _resident((D_MODEL, D_MODEL)), _resident((D_MODEL, 2 * D_FF)),
                  _resident((D_FF, D_MODEL))],
        out_specs=[full] * n_out,
        out_shape=[jax.ShapeDtypeStruct((n, D_MODEL), F32)] * n_out,
        compiler_params=_params(("arbitrary",)),
        name="merge_ffn",
    )(x, o_sb, o_ret, gate, gt1, sc2, sh2, gt2, g_sb, g_ret, g_ffn, g_fin, w_out_b, w_fi_b, w_fo_b)
    return res if final else (res[0], None)


def kernel(x_prompt, x_sample, cache_sb_k, cache_sb_v, state_ret, c_prompt, c_sample,
           g_norm_mix, g_norm_ffn, w_ada, b_ada, w_in, g_sb_out, g_ret_out, w_out,
           w_ff_in, w_ff_out, g_final):
    depth = w_in.shape[0]
    bp, sp, _ = x_prompt.shape
    bs, ts, _ = x_sample.shape
    past = cache_sb_k.shape[2]
    n_p = bp * sp
    n_s = bs * ts

    c_all = jnp.concatenate([c_prompt, c_sample], axis=0)
    pad = (-c_all.shape[0]) % 16
    mods = _modulation(jnp.pad(c_all, ((0, pad), (0, 0))), w_ada, b_ada)
    mods = mods.reshape(depth, -1, N_MOD, D_MODEL)
    mods_p = mods[:, :bp]
    mods_s = jnp.repeat(mods[:, bp:bp + bs], ts, axis=1)

    cos_p, sin_p = _rope_tables(sp, 0)
    cos_s, sin_s = _rope_tables(ts, past)
    cos_s = jnp.tile(cos_s, (bs, 1))
    sin_s = jnp.tile(sin_s, (bs, 1))

    cache_k = jnp.transpose(cache_sb_k, (0, 1, 3, 4, 2))
    cache_v = jnp.transpose(cache_sb_v, (0, 1, 3, 4, 2))
    zero_state = jnp.zeros((bp, RET_HEADS, RET_HEAD_DIM, RET_HEAD_DIM), F32)

    xp = x_prompt.reshape(n_p, D_MODEL)
    xs = x_sample.reshape(n_s, D_MODEL)
    kv_p = (jnp.zeros((depth, bp, SB_HEADS, SB_HEAD_DIM, sp), F32),) * 2
    rp, ks, vs, rs = [], [], [], []
    yp = ys = None
    for l in range(depth):
        final = l == depth - 1
        w_in_b = w_in[l].astype(BF16)
        wkvt_b = w_in[l][:, SB_WIDTH:3 * SB_WIDTH].T.astype(BF16)
        w_out_b = w_out[l].astype(BF16)
        w_fi_b = w_ff_in[l].astype(BF16)
        w_fo_b = w_ff_out[l].astype(BF16)
        g_mix = g_norm_mix[l].reshape(1, D_MODEL)
        g_ffn = g_norm_ffn[l].reshape(1, D_MODEL)
        g_sb = g_sb_out[l].reshape(1, SB_WIDTH)
        g_ret = g_ret_out[l].reshape(1, RET_WIDTH)
        g_fin = g_final.reshape(1, D_MODEL)

        m = [mods_p[l, :, j].reshape(bp, 1, D_MODEL) for j in range(N_MOD)]
        q, k, k16, v, vt, qr, kr, vr, gate = _proj(
            xp, m[1], m[0], g_mix, cos_p, sin_p, w_in_b, wkvt_b, kv_p, l, batch=bp, prompt=True)
        kv_p = (k, v)
        o_sb = _sb_prompt(q, k16, vt, batch=bp)
        o_ret, st = _retention(qr, kr, vr, zero_state, batch=bp)
        xp, yp = _merge_ffn(xp, o_sb, o_ret, gate, m[2], m[4], m[3], m[5], g_sb, g_ret, g_ffn,
                            g_fin, w_out_b, w_fi_b, w_fo_b, batch=bp, final=final)
        rp.append(st)

        m = [mods_s[l, :, j].reshape(1, n_s, D_MODEL) for j in range(N_MOD)]
        q, k, v, qr, kr, vr, gate = _proj(
            xs, m[1], m[0], g_mix, cos_s, sin_s, w_in_b, batch=bs, prompt=False)
        o_sb = _sb_sample(q, k, v, cache_k, cache_v, l, batch=bs)
        o_ret, st = _retention(qr, kr, vr, state_ret[l], batch=bs)
        xs, ys = _merge_ffn(xs, o_sb, o_ret, gate, m[2], m[4], m[3], m[5], g_sb, g_ret, g_ffn,
                            g_fin, w_out_b, w_fi_b, w_fo_b, batch=bs, final=final)
        ks.append(k)
        vs.append(v)
        rs.append(st)

    def heads(a, b, t):
        return a.reshape(depth, b, t, SB_HEADS, SB_HEAD_DIM)

    def rows(a):
        return jnp.transpose(a, (0, 1, 4, 2, 3))

    return (yp.reshape(bp, sp, D_MODEL), ys.reshape(bs, ts, D_MODEL),
            rows(kv_p[0]), rows(kv_p[1]), jnp.stack(rp),
            heads(jnp.stack(ks), bs, ts), heads(jnp.stack(vs), bs, ts), jnp.stack(rs))
```

```python
import functools
import math

import jax
import jax.numpy as jnp
from jax import lax
from jax.experimental import pallas as pl
from jax.experimental.pallas import tpu as pltpu

F32 = jnp.float32
BF16 = jnp.bfloat16

D_MODEL = 1024
SB_WIDTH = 512
RET_WIDTH = 512
SB_HEAD_DIM = 64
SB_HEADS = SB_WIDTH // SB_HEAD_DIM
RET_HEAD_DIM = 128
RET_HEADS = RET_WIDTH // RET_HEAD_DIM
IN_WIDTH = 3 * SB_WIDTH + 4 * RET_WIDTH
D_FF = 2816
N_MOD = 6
ROPE_BASE = 10000.0
EPS = 1e-6
LOG2E = 1.4426950408889634

LANES = 128
BF16_SUBLANES = 16
MXU_DIM = 256
VMEM_LIMIT_BYTES = 56 * 1024 * 1024

TOKEN_TILE = 512
ROPE_ROWS = 2048
SB_TILE = MXU_DIM
SB_GROUP = 4
SB_UNROLL = 3
SAMPLE_SPAN = 2048
RET_CHUNK = MXU_DIM
RET_GROUP = 4
FF_CHUNK = MXU_DIM
SOFTPLUS_CLAMP = 30.0
SHIFT_OUT = 1e30
UNDERFLOW_LOG2 = 160.0


def _dot(a, b):
    return jnp.dot(a, b, preferred_element_type=F32)


def _dot_nt(a, b):
    return lax.dot_general(a, b, (((1,), (1,)), ((), ())), preferred_element_type=F32)


def _dot_tn(a, b):
    return lax.dot_general(a, b, (((0,), (0,)), ((), ())), preferred_element_type=F32)


def _split_bf16(x):
    hi = x.astype(BF16)
    lo = (x - hi.astype(F32)).astype(BF16)
    return hi, lo


def _rms(x, g):
    return x * lax.rsqrt(jnp.mean(x * x, axis=-1, keepdims=True) + EPS) * g


def _softplus2(z2):
    e = jnp.exp2(jnp.minimum(z2, SOFTPLUS_CLAMP))
    return jnp.maximum(z2, jnp.log(1.0 + e) * LOG2E)


def _params(semantics):
    return pltpu.CompilerParams(dimension_semantics=semantics, vmem_limit_bytes=VMEM_LIMIT_BYTES)


def _resident(shape):
    zeros = (0,) * len(shape)
    return pl.BlockSpec(shape, lambda *_: zeros, pipeline_mode=pl.Buffered(1))


def _mod_kernel(c_ref, w_ref, b_ref, o_ref):
    c = c_ref[...]
    a = c * jax.nn.sigmoid(c)
    a_hi, a_lo = _split_bf16(a)
    w_hi, w_lo = _split_bf16(w_ref[0])
    o_ref[0] = _dot(a_hi, w_hi) + _dot(a_lo, w_hi) + _dot(a_hi, w_lo) + b_ref[0]


def _modulation(c, w_ada, b_ada):
    depth = w_ada.shape[0]
    rows = c.shape[0]
    return pl.pallas_call(
        _mod_kernel,
        grid=(depth, N_MOD),
        in_specs=[
            pl.BlockSpec((rows, D_MODEL), lambda l, j: (0, 0)),
            pl.BlockSpec((1, D_MODEL, D_MODEL), lambda l, j: (l, 0, j)),
            pl.BlockSpec((1, 1, D_MODEL), lambda l, j: (l, 0, j)),
        ],
        out_specs=pl.BlockSpec((1, rows, D_MODEL), lambda l, j: (l, 0, j)),
        out_shape=jax.ShapeDtypeStruct((depth, rows, N_MOD * D_MODEL), F32),
        compiler_params=_params(("arbitrary", "arbitrary")),
        name="modulation",
    )(c, w_ada, b_ada.reshape(depth, 1, N_MOD * D_MODEL))


def _rope_kernel(inv_ref, cos_ref, sin_ref, *, rows, pos0):
    row = lax.broadcasted_iota(jnp.int32, (rows, LANES), 0) + (pl.program_id(0) * rows + pos0)
    lane = lax.broadcasted_iota(jnp.int32, (rows, LANES), 1)
    ang = row.astype(F32) * inv_ref[...]
    sin = jnp.sin(ang)
    cos_ref[...] = jnp.cos(ang)
    sin_ref[...] = jnp.where(lane < RET_HEAD_DIM // 2, -sin, sin)


def _rope_tables(n, pos0):
    half = RET_HEAD_DIM // 2
    inv = ROPE_BASE ** (-jnp.arange(half, dtype=F32) / half)
    inv2 = jnp.concatenate([inv, inv]).reshape(1, LANES)
    rows = min(n, ROPE_ROWS)
    assert n % rows == 0
    return pl.pallas_call(
        functools.partial(_rope_kernel, rows=rows, pos0=pos0),
        grid=(n // rows,),
        in_specs=[pl.BlockSpec((1, LANES), lambda i: (0, 0))],
        out_specs=[pl.BlockSpec((rows, LANES), lambda i: (i, 0))] * 2,
        out_shape=[jax.ShapeDtypeStruct((n, LANES), F32)] * 2,
        compiler_params=_params(("arbitrary",)),
        name="rope_tables",
    )(inv2)


def _proj_kernel(x_ref, sc_ref, sh_ref, g_ref, cos_ref, sin_ref, w_ref, *rest, tm, kb, prompt):
    if prompt:
        wkvt_ref, _, _, q_ref, kt_ref, kb_ref, vtf_ref, vt_ref, qr_ref, kr_ref, vr_ref, gate_ref = rest
    else:
        q_ref, k_ref, v_ref, qr_ref, kr_ref, vr_ref, gate_ref = rest
    h = _rms(x_ref[...], g_ref[...]) * (1.0 + sc_ref[0]) + sh_ref[0]
    hb = h.astype(BF16)

    def seg(a, b):
        return _dot(hb, w_ref[:, a:b])

    q_ref[...] = (seg(0, SB_WIDTH) * (SB_HEAD_DIM ** -0.5 * LOG2E)).astype(BF16)
    k = seg(SB_WIDTH, 2 * SB_WIDTH)
    if prompt:
        kb_ref[...] = k.astype(BF16)
        kvt = _dot_nt(wkvt_ref[...], hb)
        kt_ref[0, 0] = kvt[:SB_WIDTH].reshape(SB_HEADS, SB_HEAD_DIM, tm)
        vtf_ref[0, 0] = kvt[SB_WIDTH:].reshape(SB_HEADS, SB_HEAD_DIM, tm)
        vt = kvt[SB_WIDTH:].astype(BF16)
        for hp in range(SB_WIDTH // LANES):
            for j in range(tm // kb):
                vt_ref[0, hp, j] = vt[hp * LANES:(hp + 1) * LANES, j * kb:(j + 1) * kb]
    else:
        k_ref[...] = k
        v_ref[...] = seg(2 * SB_WIDTH, 3 * SB_WIDTH)
    cos = cos_ref[...]
    sin = sin_ref[...]
    base = 3 * SB_WIDTH
    qr = seg(base, base + RET_WIDTH)
    kr = seg(base + RET_WIDTH, base + 2 * RET_WIDTH)
    for hh in range(RET_HEADS):
        sl = slice(hh * RET_HEAD_DIM, (hh + 1) * RET_HEAD_DIM)
        qh = qr[:, sl]
        kh = kr[:, sl]
        qr_ref[:, sl] = (qh * cos + pltpu.roll(qh, RET_HEAD_DIM // 2, 1) * sin).astype(BF16)
        kr_ref[:, sl] = ((kh * cos + pltpu.roll(kh, RET_HEAD_DIM // 2, 1) * sin)
                         * RET_HEAD_DIM ** -0.5).astype(BF16)
    vr_ref[...] = seg(base + 2 * RET_WIDTH, base + 3 * RET_WIDTH).astype(BF16)
    gate_ref[...] = seg(base + 3 * RET_WIDTH, base + 4 * RET_WIDTH)


def _proj(x, sc, sh, g, cos, sin, w_in_b, wkvt_b=None, kv_out=None, layer=0, *, batch, prompt):
    n = x.shape[0]
    tm = min(TOKEN_TILE, n)
    tiles = n // tm
    per_seq = max(tiles // batch, 1)
    pos_tiles = cos.shape[0] // tm
    kb = SB_TILE
    mrows = sc.shape[1]

    def row(i):
        return (i, 0)

    def mod(i):
        return (i // per_seq if mrows == 1 else i, 0, 0)

    def pos(i):
        return (i % pos_tiles, 0)

    in_specs = [
        pl.BlockSpec((tm, D_MODEL), row),
        pl.BlockSpec((1, mrows, D_MODEL), mod),
        pl.BlockSpec((1, mrows, D_MODEL), mod),
        _resident((1, D_MODEL)),
        pl.BlockSpec((tm, LANES), pos),
        pl.BlockSpec((tm, LANES), pos),
        _resident((D_MODEL, IN_WIDTH)),
    ]
    args = [x, sc, sh, g, cos, sin, w_in_b]
    wide_b = jax.ShapeDtypeStruct((n, SB_WIDTH), BF16)
    wide_f = jax.ShapeDtypeStruct((n, SB_WIDTH), F32)
    wide = pl.BlockSpec((tm, SB_WIDTH), row)
    if prompt:
        in_specs += [_resident((2 * SB_WIDTH, D_MODEL))] + [pl.BlockSpec(memory_space=pl.ANY)] * 2
        aliases = {len(args) + 1: 1, len(args) + 2: 3}
        args += [wkvt_b, *kv_out]
        nkb = n // batch // kb
        hps = SB_WIDTH // LANES
        head_f = jax.ShapeDtypeStruct(kv_out[0].shape, F32)
        head = pl.BlockSpec((1, 1, SB_HEADS, SB_HEAD_DIM, tm),
                            lambda i: (layer, i // per_seq, 0, 0, i % per_seq))
        out_shape = [wide_b, head_f, wide_b, head_f,
                     jax.ShapeDtypeStruct((batch, hps, nkb, LANES, kb), BF16),
                     wide_b, wide_b, wide_b, wide_f]
        out_specs = [wide, head, wide, head,
                     pl.BlockSpec((1, hps, tm // kb, LANES, kb),
                                  lambda i: (i // per_seq, 0, i % per_seq, 0, 0)),
                     wide, wide, wide, wide]
    else:
        aliases = {}
        out_shape = [wide_b, wide_f, wide_f, wide_b, wide_b, wide_b, wide_f]
        out_specs = [wide] * 7
    return pl.pallas_call(
        functools.partial(_proj_kernel, tm=tm, kb=kb, prompt=prompt),
        grid=(tiles,),
        in_specs=in_specs,
        out_specs=out_specs,
        out_shape=out_shape,
        input_output_aliases=aliases,
        compiler_params=_params(("arbitrary",)),
        name="proj_prompt" if prompt else "proj_sample",
    )(*args)


def _sb_prompt_kernel(q_ref, k_ref, vt_ref, o_ref, acc_ref, zt_ref, s_ref, w_ref, *, t, nq, group):
    first_tile = pl.program_id(2) * group
    hd = SB_HEAD_DIM
    heads = range(2)
    tiles = range(group)
    chains = [(g, h) for g in tiles for h in heads]
    lane = lax.broadcasted_iota(jnp.int32, (t, LANES), 1)
    qms = []
    for g in tiles:
        q = q_ref[0, g * t:(g + 1) * t, :]
        qms.append((jnp.where(lane < hd, q, jnp.zeros_like(q)),
                    jnp.where(lane >= hd, q, jnp.zeros_like(q))))
    kk = lax.broadcasted_iota(jnp.int32, (t, t), 0)
    qq = lax.broadcasted_iota(jnp.int32, (t, t), 1)
    tri = (qq >= kk).astype(BF16)
    valid = kk < qq

    def keys(j):
        return k_ref[0, pl.ds(pl.multiple_of(j * t, t), t), :]

    def gather(j, ws):
        vblk = vt_ref[0, 0, j]
        return [_dot(vblk[h * hd:(h + 1) * hd, :], ws[h]) for h in heads]

    def weights(zt, r, carried):
        return jnp.exp2(jnp.minimum(zt - r, 0.0) + carried)

    diag = [first_tile + g for g in tiles]
    left = [jnp.maximum(diag[g] - 1, 0) for g in tiles]
    shift_left = [jnp.where(diag[g] > 0, 0.0, -SHIFT_OUT) for g in tiles]
    k_diag = [keys(diag[g]) for g in tiles]
    k_left = [keys(left[g]) for g in tiles]
    z_diag = [_dot_nt(k_diag[g], qms[g][h]) for g, h in chains]
    z_left = [_dot_nt(k_left[g], qms[g][h]) for g, h in chains]
    s_diag = [jnp.where(valid, _softplus2(z), 0.0).astype(BF16) for z in z_diag]
    s_left = [_softplus2(z).astype(BF16) for z in z_left]
    r_diag = [_dot(tri, s) for s in s_diag]
    r_left = [_dot(tri, s) for s in s_left]
    cum_diag = [-r[0:1, :] for r in r_diag]
    w_diag = [jnp.where(valid, weights(z_diag[c], r_diag[c], 0.0), 0.0).astype(BF16)
              for c in range(len(chains))]
    w_left = [weights(z_left[c], r_left[c], cum_diag[c] + shift_left[chains[c][0]]).astype(BF16)
              for c in range(len(chains))]
    cum_left = [cum_diag[c] - r_left[c][0:1, :] for c in range(len(chains))]
    for g in tiles:
        p_diag = gather(diag[g], w_diag[2 * g:2 * g + 2])
        p_left = gather(left[g], w_left[2 * g:2 * g + 2])
        for h in heads:
            acc_ref[g, h * hd:(h + 1) * hd, :] = p_diag[h] + p_left[h]

    def all_faded(cums):
        return jnp.max(jnp.maximum(cums[0], cums[1])) < -UNDERFLOW_LOG2

    def rest_of_sweep(g, cums):
        i = diag[g]
        qm = qms[g]

        def block_of(m):
            return jnp.clip(i - 2 - m, 0, nq - 1)

        def scores(j):
            kblk = keys(j)
            return [_dot_nt(kblk, qm[h]) for h in heads]

        rounds = jnp.where(i > 1, (i + SB_UNROLL) // SB_UNROLL, 0)
        faded = all_faded(cums)

        @pl.when(jnp.logical_and(rounds > 0, jnp.logical_not(faded)))
        def _():
            first = scores(block_of(0))
            for h in heads:
                zt_ref[0, h] = first[h]
                zt_ref[2, h] = jnp.zeros((t, t), F32)
                s_ref[2, h] = jnp.zeros((t, t), BF16)
                w_ref[1, h] = jnp.zeros((t, t), BF16)

        def trip(n, r, cums):
            nxt_slot, cur, old, done = (r + 1) % 3, r, (r + 2) % 3, (r + 1) % 3
            shift = jnp.where(jnp.logical_and(n >= 1, n < i), 0.0, -SHIFT_OUT)
            rs = [_dot(tri, s_ref[old, h]) for h in heads]
            new_ws = [weights(zt_ref[old, h], rs[h], cums[h] + shift) for h in heads]
            new_cums = [cums[h] - rs[h][0:1, :] for h in heads]
            nxt = scores(block_of(n + 1))
            parts = gather(block_of(n - 2), [w_ref[done, h] for h in heads])
            for h in heads:
                s_ref[cur, h] = _softplus2(zt_ref[cur, h]).astype(BF16)
            for h in heads:
                acc_ref[g, h * hd:(h + 1) * hd, :] += parts[h]
                w_ref[old, h] = new_ws[h].astype(BF16)
                zt_ref[nxt_slot, h] = nxt[h]
            return tuple(new_cums)

        def more(state):
            p, faded, _ = state
            return jnp.logical_and(p < rounds, jnp.logical_not(faded))

        def body(state):
            p, faded, cums = state
            for r in range(SB_UNROLL):
                cums = trip(SB_UNROLL * p + r, r % 3, cums)
                if r == SB_UNROLL - 2:
                    faded = all_faded(cums)
            return p + 1, faded, cums

        lax.while_loop(more, body, (jnp.int32(0), faded, cums))

    for g in tiles:
        rest_of_sweep(g, tuple(cum_left[2 * g:2 * g + 2]))
        o_ref[0, g * t:(g + 1) * t, :] = acc_ref[g].T


def _sb_prompt(q, kb16, vt, *, batch):
    n = q.shape[0]
    s = n // batch
    t = SB_TILE
    nq = s // t
    group = SB_GROUP if nq % SB_GROUP == 0 else 1
    hps = SB_WIDTH // LANES
    q3 = q.reshape(batch, s, SB_WIDTH)
    k3 = kb16.reshape(batch, s, SB_WIDTH)
    out = pl.pallas_call(
        functools.partial(_sb_prompt_kernel, t=t, nq=nq, group=group),
        grid=(batch, hps, nq // group),
        in_specs=[
            pl.BlockSpec((1, group * t, LANES), lambda b, hp, i: (b, i, hp)),
            pl.BlockSpec((1, s, LANES), lambda b, hp, i: (b, 0, hp)),
            pl.BlockSpec((1, 1, nq, LANES, t), lambda b, hp, i: (b, hp, 0, 0, 0)),
        ],
        out_specs=pl.BlockSpec((1, group * t, LANES), lambda b, hp, i: (b, i, hp)),
        out_shape=jax.ShapeDtypeStruct((batch, s, SB_WIDTH), F32),
        scratch_shapes=[pltpu.VMEM((group, LANES, t), F32), pltpu.VMEM((3, 2, t, t), F32),
                        pltpu.VMEM((3, 2, t, t), BF16), pltpu.VMEM((3, 2, t, t), BF16)],
        compiler_params=_params(("arbitrary", "arbitrary", "arbitrary")),
        name="sb_prompt",
    )(q3, k3, vt)
    return out.reshape(n, SB_WIDTH)


def _sb_sample_kernel(q_ref, kn_ref, vn_ref, ck_ref, cv_ref, o_ref, acc_ref, cum_ref, fade_ref,
                      *, tq, kb, span, nb):
    j = pl.program_id(1)
    hd = SB_HEAD_DIM
    heads = range(SB_HEADS)
    cols = [slice(h * hd, (h + 1) * hd) for h in heads]
    qs = [q_ref[:, c] for c in cols]

    def cumsum_right(s, m):
        s_hi, s_lo = _split_bf16(s)
        return _dot(s_hi, m) + _dot(s_lo, m)

    def sweep(zs, vts, m, cums, mask=None):
        ss = [_softplus2(z) for z in zs]
        if mask is not None:
            ss = [jnp.where(mask, s, 0.0) for s in ss]
        rs = [cumsum_right(s, m) for s in ss]
        ws = [jnp.exp2(jnp.minimum(zs[h] - rs[h], 0.0) + cums[h]) for h in heads]
        if mask is not None:
            ws = [jnp.where(mask, w, 0.0) for w in ws]
        outs = [_dot_nt(ws[h].astype(BF16), vts[h]) for h in heads]
        return outs, [cums[h] - rs[h][:, 0:1] for h in heads]

    @pl.when(j == 0)
    def _():
        qi = lax.broadcasted_iota(jnp.int32, (tq, tq), 0)
        ki = lax.broadcasted_iota(jnp.int32, (tq, tq), 1)
        valid = ki < qi
        tri_new = (qi >= ki).astype(BF16)
        zs = [_dot_nt(qs[h], kn_ref[:, cols[h]].astype(BF16)) for h in heads]
        vts = [vn_ref[:, cols[h]].astype(BF16).T for h in heads]
        outs, cums = sweep(zs, vts, tri_new, [jnp.zeros((tq, 1), F32)] * SB_HEADS, valid)
        for h in heads:
            acc_ref[h] = outs[h]
            cum_ref[h] = cums[h]
        fade_ref[0] = 0

    for blk in reversed(range(span // kb)):
        lanes = slice(blk * kb, (blk + 1) * kb)

        @pl.when(fade_ref[0] == 0)
        def _():
            a = lax.broadcasted_iota(jnp.int32, (kb, kb), 0)
            b = lax.broadcasted_iota(jnp.int32, (kb, kb), 1)
            tri = (a >= b).astype(BF16)
            zs = [_dot(qs[h], ck_ref[0, 0, h, :, lanes].astype(BF16)) for h in heads]
            vts = [cv_ref[0, 0, h, :, lanes].astype(BF16) for h in heads]
            outs, cums = sweep(zs, vts, tri, [cum_ref[h] for h in heads])
            for h in heads:
                acc_ref[h] += outs[h]
                cum_ref[h] = cums[h]
            faded = jnp.max(functools.reduce(jnp.maximum, cums)) < -UNDERFLOW_LOG2
            fade_ref[0] = faded.astype(jnp.int32)

    @pl.when(j == nb - 1)
    def _():
        for h in heads:
            o_ref[:, cols[h]] = acc_ref[h]


def _sb_sample(q, k_new, v_new, cache_kt, cache_vt, layer, *, batch):
    n = q.shape[0]
    tq = n // batch
    past = cache_kt.shape[-1]
    kb = min(MXU_DIM, past)
    span = min(SAMPLE_SPAN, past)
    nb = past // span
    assert past == nb * span and span % kb == 0
    row = pl.BlockSpec((tq, SB_WIDTH), lambda b, j: (b, 0))
    cache = pl.BlockSpec((1, 1, SB_HEADS, SB_HEAD_DIM, span), lambda b, j: (layer, b, 0, 0, nb - 1 - j))
    return pl.pallas_call(
        functools.partial(_sb_sample_kernel, tq=tq, kb=kb, span=span, nb=nb),
        grid=(batch, nb),
        in_specs=[row, row, row, cache, cache],
        out_specs=row,
        out_shape=jax.ShapeDtypeStruct((n, SB_WIDTH), F32),
        scratch_shapes=[pltpu.VMEM((SB_HEADS, tq, SB_HEAD_DIM), F32),
                        pltpu.VMEM((SB_HEADS, tq, 1), F32), pltpu.SMEM((1,), jnp.int32)],
        compiler_params=_params(("arbitrary", "arbitrary")),
        name="sb_sample",
    )(q, k_new, v_new, cache_kt, cache_vt)


def _ret_log_gamma(h):
    return math.log1p(-2.0 ** (-5 - h))


def _ret_kernel(q_ref, k_ref, v_ref, s0_ref, o_ref, sout_ref, state, dec, qdec, kdec,
                *, c, per, nsteps):
    seq = pl.program_id(0)
    ci = pl.program_id(1)

    @pl.when(jnp.logical_and(seq == 0, ci == 0))
    def _():
        li = lax.broadcasted_iota(jnp.int32, (c, c), 0)
        mi = lax.broadcasted_iota(jnp.int32, (c, c), 1)
        diff = (li - mi).astype(F32)
        pos = lax.broadcasted_iota(jnp.int32, (c, RET_HEAD_DIM), 0).astype(F32)
        for h in range(RET_HEADS):
            lg = _ret_log_gamma(h)
            dec[h] = jnp.where(diff >= 0.0, jnp.exp(jnp.maximum(diff, 0.0) * lg), 0.0)
            qdec[h] = jnp.exp((pos + 1.0) * lg)
            kdec[h] = jnp.exp((c - 1.0 - pos) * lg)

    @pl.when(ci == 0)
    def _():
        state[...] = s0_ref[0]

    heads = range(RET_HEADS)
    chunks = range(per)
    cols = [slice(h * RET_HEAD_DIM, (h + 1) * RET_HEAD_DIM) for h in heads]
    rows = [slice(a * c, (a + 1) * c) for a in chunks]
    qs = [[q_ref[rows[a], cols[h]] for h in heads] for a in chunks]
    ks = [[k_ref[rows[a], cols[h]] for h in heads] for a in chunks]
    vs = [[v_ref[rows[a], cols[h]] for h in heads] for a in chunks]
    scores = [[_dot_nt(qs[a][h], ks[a][h]) for h in heads] for a in chunks]
    kds = [[(ks[a][h].astype(F32) * kdec[h]).astype(BF16) for h in heads] for a in chunks]
    grown = [[_dot_tn(kds[a][h], vs[a][h]) for h in heads] for a in chunks]
    sts = [state[h] for h in heads]
    cross = []
    for a in chunks:
        cross.append([_dot(qs[a][h], sts[h].astype(BF16)) for h in heads])
        sts = [math.exp(c * _ret_log_gamma(h)) * sts[h] + grown[a][h] for h in heads]
    inner = [[_dot((scores[a][h] * dec[h]).astype(BF16), vs[a][h]) for h in heads] for a in chunks]
    for a in chunks:
        for h in heads:
            o_ref[rows[a], cols[h]] = inner[a][h] + cross[a][h] * qdec[h]
    for h in heads:
        state[h] = sts[h]

    @pl.when(ci == nsteps - 1)
    def _():
        sout_ref[0] = state[...]


def _retention(q, k, v, state0, *, batch):
    n = q.shape[0]
    t = n // batch
    c = min(RET_CHUNK, t)
    per = RET_GROUP if (t // c) % RET_GROUP == 0 else 1
    nsteps = t // (c * per)
    row = pl.BlockSpec((per * c, RET_WIDTH), lambda b, i: (b * nsteps + i, 0))
    st = pl.BlockSpec((1, RET_HEADS, RET_HEAD_DIM, RET_HEAD_DIM), lambda b, i: (b, 0, 0, 0))
    return pl.pallas_call(
        functools.partial(_ret_kernel, c=c, per=per, nsteps=nsteps),
        grid=(batch, nsteps),
        in_specs=[row, row, row, st],
        out_specs=[row, st],
        out_shape=[jax.ShapeDtypeStruct((n, RET_WIDTH), F32),
                   jax.ShapeDtypeStruct((batch, RET_HEADS, RET_HEAD_DIM, RET_HEAD_DIM), F32)],
        scratch_shapes=[pltpu.VMEM((RET_HEADS, RET_HEAD_DIM, RET_HEAD_DIM), F32),
                        pltpu.VMEM((RET_HEADS, c, c), F32),
                        pltpu.VMEM((RET_HEADS, c, RET_HEAD_DIM), F32),
                        pltpu.VMEM((RET_HEADS, c, RET_HEAD_DIM), F32)],
        compiler_params=_params(("arbitrary", "arbitrary")),
        name="retention",
    )(q, k, v, state0)


def _merge_ffn_kernel(x_ref, osb_ref, oret_ref, gate_ref, gt1_ref, sc2_ref, sh2_ref, gt2_ref,
                      gsb_ref, gret_ref, gffn_ref, gfin_ref, wout_ref, wfi_ref, wfo_ref,
                      *outs, final):
    xo_ref = outs[0]
    a = _rms(osb_ref[...], gsb_ref[...]).astype(BF16)
    mix = _dot(a, wout_ref[0:SB_WIDTH, :])
    gate = gate_ref[...]
    gate = gate * jax.nn.sigmoid(gate)
    for h in range(RET_HEADS):
        sl = slice(h * RET_HEAD_DIM, (h + 1) * RET_HEAD_DIM)
        r = (_rms(oret_ref[:, sl], gret_ref[:, sl]) * gate[:, sl]).astype(BF16)
        mix += _dot(r, wout_ref[SB_WIDTH + h * RET_HEAD_DIM:SB_WIDTH + (h + 1) * RET_HEAD_DIM, :])
    x1 = x_ref[...] + gt1_ref[0] * mix
    h2 = (_rms(x1, gffn_ref[...]) * (1.0 + sc2_ref[0]) + sh2_ref[0]).astype(BF16)
    ff = jnp.zeros_like(x1)
    for j in range(D_FF // FF_CHUNK):
        cols = slice(j * FF_CHUNK, (j + 1) * FF_CHUNK)
        ucols = slice(D_FF + j * FF_CHUNK, D_FF + (j + 1) * FF_CHUNK)
        g = _dot(h2, wfi_ref[:, cols])
        u = _dot(h2, wfi_ref[:, ucols])
        act = (g * jax.nn.sigmoid(g) * u).astype(BF16)
        ff += _dot(act, wfo_ref[cols, :])
    x2 = x1 + gt2_ref[0] * ff
    xo_ref[...] = x2
    if final:
        outs[1][...] = _rms(x2, gfin_ref[...])


def _merge_ffn(x, o_sb, o_ret, gate, gt1, sc2, sh2, gt2, g_sb, g_ret, g_ffn, g_fin,
               w_out_b, w_fi_b, w_fo_b, *, batch, final):
    n = x.shape[0]
    tm = min(TOKEN_TILE, n)
    tiles = n // tm
    per_seq = max(tiles // batch, 1)
    mrows = gt1.shape[1]

    def row(i):
        return (i, 0)

    def mod(i):
        return (i // per_seq if mrows == 1 else i, 0, 0)

    full = pl.BlockSpec((tm, D_MODEL), row)
    half = pl.BlockSpec((tm, SB_WIDTH), row)
    mspec = pl.BlockSpec((1, mrows, D_MODEL), mod)
    n_out = 2 if final else 1
    res = pl.pallas_call(
        functools.partial(_merge_ffn_kernel, final=final),
        grid=(tiles,),
        in_specs=[full, half, half, half, mspec, mspec, mspec, mspec,
                  _resident((1, SB_WIDTH)), _resident((1, RET_WIDTH)),
                  _resident((1, D_MODEL)), _resident((1, D_MODEL)),
                  _resident((D_MODEL, D_MODEL)), _resident((D_MODEL, 2 * D_FF)),
                  _resident((D_FF, D_MODEL))],
        out_specs=[full] * n_out,
        out_shape=[jax.ShapeDtypeStruct((n, D_MODEL), F32)] * n_out,
        compiler_params=_params(("arbitrary",)),
        name="merge_ffn",
    )(x, o_sb, o_ret, gate, gt1, sc2, sh2, gt2, g_sb, g_ret, g_ffn, g_fin, w_out_b, w_fi_b, w_fo_b)
    return res if final else (res[0], None)


def kernel(x_prompt, x_sample, cache_sb_k, cache_sb_v, state_ret, c_prompt, c_sample,
           g_norm_mix, g_norm_ffn, w_ada, b_ada, w_in, g_sb_out, g_ret_out, w_out,
           w_ff_in, w_ff_out, g_final):
    depth = w_in.shape[0]
    bp, sp, _ = x_prompt.shape
    bs, ts, _ = x_sample.shape
    past = cache_sb_k.shape[2]
    n_p = bp * sp
    n_s = bs * ts

    c_all = jnp.concatenate([c_prompt, c_sample], axis=0)
    pad = (-c_all.shape[0]) % BF16_SUBLANES
    mods = _modulation(jnp.pad(c_all, ((0, pad), (0, 0))), w_ada, b_ada)
    mods = mods.reshape(depth, -1, N_MOD, D_MODEL)
    mods_p = mods[:, :bp]
    mods_s = jnp.repeat(mods[:, bp:bp + bs], ts, axis=1)

    cos_p, sin_p = _rope_tables(sp, 0)
    cos_s, sin_s = _rope_tables(ts, past)
    cos_s = jnp.tile(cos_s, (bs, 1))
    sin_s = jnp.tile(sin_s, (bs, 1))

    cache_k = jnp.transpose(cache_sb_k, (0, 1, 3, 4, 2))
    cache_v = jnp.transpose(cache_sb_v, (0, 1, 3, 4, 2))
    zero_state = jnp.zeros((bp, RET_HEADS, RET_HEAD_DIM, RET_HEAD_DIM), F32)

    xp = x_prompt.reshape(n_p, D_MODEL)
    xs = x_sample.reshape(n_s, D_MODEL)
    kv_p = (jnp.zeros((depth, bp, SB_HEADS, SB_HEAD_DIM, sp), F32),) * 2
    rp, ks, vs, rs = [], [], [], []
    yp = ys = None
    for l in range(depth):
        final = l == depth - 1
        w_in_b = w_in[l].astype(BF16)
        wkvt_b = w_in[l][:, SB_WIDTH:3 * SB_WIDTH].T.astype(BF16)
        w_out_b = w_out[l].astype(BF16)
        w_fi_b = w_ff_in[l].astype(BF16)
        w_fo_b = w_ff_out[l].astype(BF16)
        g_mix = g_norm_mix[l].reshape(1, D_MODEL)
        g_ffn = g_norm_ffn[l].reshape(1, D_MODEL)
        g_sb = g_sb_out[l].reshape(1, SB_WIDTH)
        g_ret = g_ret_out[l].reshape(1, RET_WIDTH)
        g_fin = g_final.reshape(1, D_MODEL)

        m = [mods_p[l, :, j].reshape(bp, 1, D_MODEL) for j in range(N_MOD)]
        q, k, k16, v, vt, qr, kr, vr, gate = _proj(
            xp, m[1], m[0], g_mix, cos_p, sin_p, w_in_b, wkvt_b, kv_p, l, batch=bp, prompt=True)
        kv_p = (k, v)
        o_sb = _sb_prompt(q, k16, vt, batch=bp)
        o_ret, st = _retention(qr, kr, vr, zero_state, batch=bp)
        xp, yp = _merge_ffn(xp, o_sb, o_ret, gate, m[2], m[4], m[3], m[5], g_sb, g_ret, g_ffn,
                            g_fin, w_out_b, w_fi_b, w_fo_b, batch=bp, final=final)
        rp.append(st)

        m = [mods_s[l, :, j].reshape(1, n_s, D_MODEL) for j in range(N_MOD)]
        q, k, v, qr, kr, vr, gate = _proj(
            xs, m[1], m[0], g_mix, cos_s, sin_s, w_in_b, batch=bs, prompt=False)
        o_sb = _sb_sample(q, k, v, cache_k, cache_v, l, batch=bs)
        o_ret, st = _retention(qr, kr, vr, state_ret[l], batch=bs)
        xs, ys = _merge_ffn(xs, o_sb, o_ret, gate, m[2], m[4], m[3], m[5], g_sb, g_ret, g_ffn,
                            g_fin, w_out_b, w_fi_b, w_fo_b, batch=bs, final=final)
        ks.append(k)
        vs.append(v)
        rs.append(st)

    def heads(a, b, t):
        return a.reshape(depth, b, t, SB_HEADS, SB_HEAD_DIM)

    def rows(a):
        return jnp.transpose(a, (0, 1, 4, 2, 3))

    return (yp.reshape(bp, sp, D_MODEL), ys.reshape(bs, ts, D_MODEL),
            rows(kv_p[0]), rows(kv_p[1]), jnp.stack(rp),
            heads(jnp.stack(ks), bs, ts), heads(jnp.stack(vs), bs, ts), jnp.stack(rs))
```

```python
import functools
import math

import jax
import jax.numpy as jnp
from jax import lax
from jax.experimental import pallas as pl
from jax.experimental.pallas import tpu as pltpu

F32 = jnp.float32
BF16 = jnp.bfloat16

D_MODEL = 1024
SB_WIDTH = 512
RET_WIDTH = 512
SB_HEAD_DIM = 64
SB_HEADS = SB_WIDTH // SB_HEAD_DIM
RET_HEAD_DIM = 128
RET_HEADS = RET_WIDTH // RET_HEAD_DIM
IN_WIDTH = 3 * SB_WIDTH + 4 * RET_WIDTH
D_FF = 2816
N_MOD = 6
ROPE_BASE = 10000.0
EPS = 1e-6
LOG2E = 1.4426950408889634

LANES = 128
BF16_SUBLANES = 16
MXU_DIM = 256
VMEM_LIMIT_BYTES = 56 * 1024 * 1024

TOKEN_TILE = 512
ROPE_ROWS = 2048
SB_TILE = MXU_DIM
SB_GROUP = 4
SB_UNROLL = 3
SAMPLE_SPAN = 2048
RET_CHUNK = MXU_DIM
RET_GROUP = 4
FF_CHUNK = MXU_DIM
SOFTPLUS_CLAMP = 30.0
SHIFT_OUT = 1e30
UNDERFLOW_LOG2 = 160.0


def _dot(a, b):
    return jnp.dot(a, b, preferred_element_type=F32)


def _dot_nt(a, b):
    return lax.dot_general(a, b, (((1,), (1,)), ((), ())), preferred_element_type=F32)


def _dot_tn(a, b):
    return lax.dot_general(a, b, (((0,), (0,)), ((), ())), preferred_element_type=F32)


def _split_bf16(x):
    hi = x.astype(BF16)
    lo = (x - hi.astype(F32)).astype(BF16)
    return hi, lo


def _rms(x, g):
    return x * lax.rsqrt(jnp.mean(x * x, axis=-1, keepdims=True) + EPS) * g


def _softplus2(z2):
    e = jnp.exp2(jnp.minimum(z2, SOFTPLUS_CLAMP))
    return jnp.maximum(z2, jnp.log(1.0 + e) * LOG2E)


def _params(semantics):
    return pltpu.CompilerParams(dimension_semantics=semantics, vmem_limit_bytes=VMEM_LIMIT_BYTES)


def _resident(shape, layer=None):
    if layer is None:
        index = (0,) * len(shape)
    else:
        index = (layer,) + (0,) * len(shape)
        shape = (1,) + tuple(shape)
    return pl.BlockSpec(shape, lambda *_: index, pipeline_mode=pl.Buffered(1))


def _mod_kernel(c_ref, w_ref, b_ref, o_ref):
    c = c_ref[...]
    a = c * jax.nn.sigmoid(c)
    a_hi, a_lo = _split_bf16(a)
    w_hi, w_lo = _split_bf16(w_ref[0])
    o_ref[0] = _dot(a_hi, w_hi) + _dot(a_lo, w_hi) + _dot(a_hi, w_lo) + b_ref[0]


def _modulation(c, w_ada, b_ada):
    depth = w_ada.shape[0]
    rows = c.shape[0]
    return pl.pallas_call(
        _mod_kernel,
        grid=(depth, N_MOD),
        in_specs=[
            pl.BlockSpec((rows, D_MODEL), lambda l, j: (0, 0)),
            pl.BlockSpec((1, D_MODEL, D_MODEL), lambda l, j: (l, 0, j)),
            pl.BlockSpec((1, 1, D_MODEL), lambda l, j: (l, 0, j)),
        ],
        out_specs=pl.BlockSpec((1, rows, D_MODEL), lambda l, j: (l, 0, j)),
        out_shape=jax.ShapeDtypeStruct((depth, rows, N_MOD * D_MODEL), F32),
        compiler_params=_params(("arbitrary", "arbitrary")),
        name="modulation",
    )(c, w_ada, b_ada.reshape(depth, 1, N_MOD * D_MODEL))


def _rope_kernel(inv_ref, cos_ref, sin_ref, *, rows, pos0):
    row = lax.broadcasted_iota(jnp.int32, (rows, LANES), 0) + (pl.program_id(0) * rows + pos0)
    lane = lax.broadcasted_iota(jnp.int32, (rows, LANES), 1)
    ang = row.astype(F32) * inv_ref[...]
    sin = jnp.sin(ang)
    cos_ref[...] = jnp.cos(ang)
    sin_ref[...] = jnp.where(lane < RET_HEAD_DIM // 2, -sin, sin)


def _rope_tables(n, pos0):
    half = RET_HEAD_DIM // 2
    inv = ROPE_BASE ** (-jnp.arange(half, dtype=F32) / half)
    inv2 = jnp.concatenate([inv, inv]).reshape(1, LANES)
    rows = min(n, ROPE_ROWS)
    assert n % rows == 0
    return pl.pallas_call(
        functools.partial(_rope_kernel, rows=rows, pos0=pos0),
        grid=(n // rows,),
        in_specs=[pl.BlockSpec((1, LANES), lambda i: (0, 0))],
        out_specs=[pl.BlockSpec((rows, LANES), lambda i: (i, 0))] * 2,
        out_shape=[jax.ShapeDtypeStruct((n, LANES), F32)] * 2,
        compiler_params=_params(("arbitrary",)),
        name="rope_tables",
    )(inv2)


def _proj_kernel(x_ref, sc_ref, sh_ref, g_ref, cos_ref, sin_ref, w_ref, *rest, tm, kb, prompt):
    if prompt:
        wkvt_ref, _, _, q_ref, kt_ref, kb_ref, vtf_ref, vt_ref, qr_ref, kr_ref, vr_ref, gate_ref = rest
    else:
        q_ref, k_ref, v_ref, qr_ref, kr_ref, vr_ref, gate_ref = rest
    h = _rms(x_ref[...], g_ref[...]) * (1.0 + sc_ref[0]) + sh_ref[0]
    hb = h.astype(BF16)

    def seg(a, b):
        return _dot(hb, w_ref[0, :, a:b])

    q_ref[...] = (seg(0, SB_WIDTH) * (SB_HEAD_DIM ** -0.5 * LOG2E)).astype(BF16)
    k = seg(SB_WIDTH, 2 * SB_WIDTH)
    if prompt:
        kb_ref[...] = k.astype(BF16)
        kvt = _dot_nt(wkvt_ref[0], hb)
        kt_ref[0, 0] = kvt[:SB_WIDTH].reshape(SB_HEADS, SB_HEAD_DIM, tm)
        vtf_ref[0, 0] = kvt[SB_WIDTH:].reshape(SB_HEADS, SB_HEAD_DIM, tm)
        vt = kvt[SB_WIDTH:].astype(BF16)
        for hp in range(SB_WIDTH // LANES):
            for j in range(tm // kb):
                vt_ref[0, hp, j] = vt[hp * LANES:(hp + 1) * LANES, j * kb:(j + 1) * kb]
    else:
        k_ref[...] = k
        v_ref[...] = seg(2 * SB_WIDTH, 3 * SB_WIDTH)
    cos = cos_ref[...]
    sin = sin_ref[...]
    base = 3 * SB_WIDTH
    qr = seg(base, base + RET_WIDTH)
    kr = seg(base + RET_WIDTH, base + 2 * RET_WIDTH)
    for hh in range(RET_HEADS):
        sl = slice(hh * RET_HEAD_DIM, (hh + 1) * RET_HEAD_DIM)
        qh = qr[:, sl]
        kh = kr[:, sl]
        qr_ref[:, sl] = (qh * cos + pltpu.roll(qh, RET_HEAD_DIM // 2, 1) * sin).astype(BF16)
        kr_ref[:, sl] = ((kh * cos + pltpu.roll(kh, RET_HEAD_DIM // 2, 1) * sin)
                         * RET_HEAD_DIM ** -0.5).astype(BF16)
    vr_ref[...] = seg(base + 2 * RET_WIDTH, base + 3 * RET_WIDTH).astype(BF16)
    gate_ref[...] = seg(base + 3 * RET_WIDTH, base + 4 * RET_WIDTH)


def _proj(x, sc, sh, g, cos, sin, w_in_b, wkvt_b=None, kv_out=None, layer=0, *, batch, prompt):
    n = x.shape[0]
    tm = min(TOKEN_TILE, n)
    tiles = n // tm
    per_seq = max(tiles // batch, 1)
    pos_tiles = cos.shape[0] // tm
    kb = SB_TILE
    mrows = sc.shape[1]

    def row(i):
        return (i, 0)

    def mod(i):
        return (i // per_seq if mrows == 1 else i, 0, 0)

    def pos(i):
        return (i % pos_tiles, 0)

    in_specs = [
        pl.BlockSpec((tm, D_MODEL), row),
        pl.BlockSpec((1, mrows, D_MODEL), mod),
        pl.BlockSpec((1, mrows, D_MODEL), mod),
        _resident((1, D_MODEL)),
        pl.BlockSpec((tm, LANES), pos),
        pl.BlockSpec((tm, LANES), pos),
        _resident((D_MODEL, IN_WIDTH), layer),
    ]
    args = [x, sc, sh, g, cos, sin, w_in_b]
    wide_b = jax.ShapeDtypeStruct((n, SB_WIDTH), BF16)
    wide_f = jax.ShapeDtypeStruct((n, SB_WIDTH), F32)
    wide = pl.BlockSpec((tm, SB_WIDTH), row)
    if prompt:
        in_specs += [_resident((2 * SB_WIDTH, D_MODEL), layer)] + [pl.BlockSpec(memory_space=pl.ANY)] * 2
        aliases = {len(args) + 1: 1, len(args) + 2: 3}
        args += [wkvt_b, *kv_out]
        nkb = n // batch // kb
        hps = SB_WIDTH // LANES
        head_f = jax.ShapeDtypeStruct(kv_out[0].shape, F32)
        head = pl.BlockSpec((1, 1, SB_HEADS, SB_HEAD_DIM, tm),
                            lambda i: (layer, i // per_seq, 0, 0, i % per_seq))
        out_shape = [wide_b, head_f, wide_b, head_f,
                     jax.ShapeDtypeStruct((batch, hps, nkb, LANES, kb), BF16),
                     wide_b, wide_b, wide_b, wide_f]
        out_specs = [wide, head, wide, head,
                     pl.BlockSpec((1, hps, tm // kb, LANES, kb),
                                  lambda i: (i // per_seq, 0, i % per_seq, 0, 0)),
                     wide, wide, wide, wide]
    else:
        aliases = {}
        out_shape = [wide_b, wide_f, wide_f, wide_b, wide_b, wide_b, wide_f]
        out_specs = [wide] * 7
    return pl.pallas_call(
        functools.partial(_proj_kernel, tm=tm, kb=kb, prompt=prompt),
        grid=(tiles,),
        in_specs=in_specs,
        out_specs=out_specs,
        out_shape=out_shape,
        input_output_aliases=aliases,
        compiler_params=_params(("arbitrary",)),
        name="proj_prompt" if prompt else "proj_sample",
    )(*args)


def _sb_prompt_kernel(q_ref, k_ref, vt_ref, o_ref, acc_ref, zt_ref, s_ref, w_ref, *, t, nq, group):
    first_tile = pl.program_id(2) * group
    hd = SB_HEAD_DIM
    heads = range(2)
    tiles = range(group)
    chains = [(g, h) for g in tiles for h in heads]
    lane = lax.broadcasted_iota(jnp.int32, (t, LANES), 1)
    qms = []
    for g in tiles:
        q = q_ref[0, g * t:(g + 1) * t, :]
        qms.append((jnp.where(lane < hd, q, jnp.zeros_like(q)),
                    jnp.where(lane >= hd, q, jnp.zeros_like(q))))
    kk = lax.broadcasted_iota(jnp.int32, (t, t), 0)
    qq = lax.broadcasted_iota(jnp.int32, (t, t), 1)
    tri = (qq >= kk).astype(BF16)
    valid = kk < qq

    def keys(j):
        return k_ref[0, pl.ds(pl.multiple_of(j * t, t), t), :]

    def gather(j, ws):
        vblk = vt_ref[0, 0, j]
        return [_dot(vblk[h * hd:(h + 1) * hd, :], ws[h]) for h in heads]

    def weights(zt, r, carried):
        return jnp.exp2(jnp.minimum(zt - r, 0.0) + carried)

    diag = [first_tile + g for g in tiles]
    left = [jnp.maximum(diag[g] - 1, 0) for g in tiles]
    shift_left = [jnp.where(diag[g] > 0, 0.0, -SHIFT_OUT) for g in tiles]
    k_diag = [keys(diag[g]) for g in tiles]
    k_left = [keys(left[g]) for g in tiles]
    z_diag = [_dot_nt(k_diag[g], qms[g][h]) for g, h in chains]
    z_left = [_dot_nt(k_left[g], qms[g][h]) for g, h in chains]
    s_diag = [jnp.where(valid, _softplus2(z), 0.0).astype(BF16) for z in z_diag]
    s_left = [_softplus2(z).astype(BF16) for z in z_left]
    r_diag = [_dot(tri, s) for s in s_diag]
    r_left = [_dot(tri, s) for s in s_left]
    cum_diag = [-r[0:1, :] for r in r_diag]
    w_diag = [jnp.where(valid, weights(z_diag[c], r_diag[c], 0.0), 0.0).astype(BF16)
              for c in range(len(chains))]
    w_left = [weights(z_left[c], r_left[c], cum_diag[c] + shift_left[chains[c][0]]).astype(BF16)
              for c in range(len(chains))]
    cum_left = [cum_diag[c] - r_left[c][0:1, :] for c in range(len(chains))]
    for g in tiles:
        p_diag = gather(diag[g], w_diag[2 * g:2 * g + 2])
        p_left = gather(left[g], w_left[2 * g:2 * g + 2])
        for h in heads:
            acc_ref[g, h * hd:(h + 1) * hd, :] = p_diag[h] + p_left[h]

    def all_faded(cums):
        return jnp.max(jnp.maximum(cums[0], cums[1])) < -UNDERFLOW_LOG2

    def rest_of_sweep(g, cums):
        i = diag[g]
        qm = qms[g]

        def block_of(m):
            return jnp.clip(i - 2 - m, 0, nq - 1)

        def scores(j):
            kblk = keys(j)
            return [_dot_nt(kblk, qm[h]) for h in heads]

        rounds = jnp.where(i > 1, (i + SB_UNROLL) // SB_UNROLL, 0)
        faded = all_faded(cums)

        @pl.when(jnp.logical_and(rounds > 0, jnp.logical_not(faded)))
        def _():
            first = scores(block_of(0))
            for h in heads:
                zt_ref[0, h] = first[h]
                zt_ref[2, h] = jnp.zeros((t, t), F32)
                s_ref[2, h] = jnp.zeros((t, t), BF16)
                w_ref[1, h] = jnp.zeros((t, t), BF16)

        def trip(n, r, cums):
            nxt_slot, cur, old, done = (r + 1) % 3, r, (r + 2) % 3, (r + 1) % 3
            shift = jnp.where(jnp.logical_and(n >= 1, n < i), 0.0, -SHIFT_OUT)
            rs = [_dot(tri, s_ref[old, h]) for h in heads]
            new_ws = [weights(zt_ref[old, h], rs[h], cums[h] + shift) for h in heads]
            new_cums = [cums[h] - rs[h][0:1, :] for h in heads]
            nxt = scores(block_of(n + 1))
            parts = gather(block_of(n - 2), [w_ref[done, h] for h in heads])
            for h in heads:
                s_ref[cur, h] = _softplus2(zt_ref[cur, h]).astype(BF16)
            for h in heads:
                acc_ref[g, h * hd:(h + 1) * hd, :] += parts[h]
                w_ref[old, h] = new_ws[h].astype(BF16)
                zt_ref[nxt_slot, h] = nxt[h]
            return tuple(new_cums)

        def more(state):
            p, faded, _ = state
            return jnp.logical_and(p < rounds, jnp.logical_not(faded))

        def body(state):
            p, faded, cums = state
            for r in range(SB_UNROLL):
                cums = trip(SB_UNROLL * p + r, r % 3, cums)
                if r == SB_UNROLL - 2:
                    faded = all_faded(cums)
            return p + 1, faded, cums

        lax.while_loop(more, body, (jnp.int32(0), faded, cums))

    for g in tiles:
        rest_of_sweep(g, tuple(cum_left[2 * g:2 * g + 2]))
        o_ref[0, g * t:(g + 1) * t, :] = acc_ref[g].T


def _sb_prompt(q, kb16, vt, *, batch):
    n = q.shape[0]
    s = n // batch
    t = SB_TILE
    nq = s // t
    group = SB_GROUP if nq % SB_GROUP == 0 else 1
    hps = SB_WIDTH // LANES
    q3 = q.reshape(batch, s, SB_WIDTH)
    k3 = kb16.reshape(batch, s, SB_WIDTH)
    out = pl.pallas_call(
        functools.partial(_sb_prompt_kernel, t=t, nq=nq, group=group),
        grid=(batch, hps, nq // group),
        in_specs=[
            pl.BlockSpec((1, group * t, LANES), lambda b, hp, i: (b, i, hp)),
            pl.BlockSpec((1, s, LANES), lambda b, hp, i: (b, 0, hp)),
            pl.BlockSpec((1, 1, nq, LANES, t), lambda b, hp, i: (b, hp, 0, 0, 0)),
        ],
        out_specs=pl.BlockSpec((1, group * t, LANES), lambda b, hp, i: (b, i, hp)),
        out_shape=jax.ShapeDtypeStruct((batch, s, SB_WIDTH), F32),
        scratch_shapes=[pltpu.VMEM((group, LANES, t), F32), pltpu.VMEM((3, 2, t, t), F32),
                        pltpu.VMEM((3, 2, t, t), BF16), pltpu.VMEM((3, 2, t, t), BF16)],
        compiler_params=_params(("arbitrary", "arbitrary", "arbitrary")),
        name="sb_prompt",
    )(q3, k3, vt)
    return out.reshape(n, SB_WIDTH)


def _sb_sample_kernel(q_ref, kn_ref, vn_ref, ck_ref, cv_ref, o_ref, acc_ref, cum_ref, fade_ref,
                      *, tq, kb, span, nb):
    j = pl.program_id(1)
    hd = SB_HEAD_DIM
    heads = range(SB_HEADS)
    cols = [slice(h * hd, (h + 1) * hd) for h in heads]
    qs = [q_ref[:, c] for c in cols]

    def cumsum_right(s, m):
        s_hi, s_lo = _split_bf16(s)
        return _dot(s_hi, m) + _dot(s_lo, m)

    def sweep(zs, vts, m, cums, mask=None):
        ss = [_softplus2(z) for z in zs]
        if mask is not None:
            ss = [jnp.where(mask, s, 0.0) for s in ss]
        rs = [cumsum_right(s, m) for s in ss]
        ws = [jnp.exp2(jnp.minimum(zs[h] - rs[h], 0.0) + cums[h]) for h in heads]
        if mask is not None:
            ws = [jnp.where(mask, w, 0.0) for w in ws]
        outs = [_dot_nt(ws[h].astype(BF16), vts[h]) for h in heads]
        return outs, [cums[h] - rs[h][:, 0:1] for h in heads]

    @pl.when(j == 0)
    def _():
        qi = lax.broadcasted_iota(jnp.int32, (tq, tq), 0)
        ki = lax.broadcasted_iota(jnp.int32, (tq, tq), 1)
        valid = ki < qi
        tri_new = (qi >= ki).astype(BF16)
        zs = [_dot_nt(qs[h], kn_ref[:, cols[h]].astype(BF16)) for h in heads]
        vts = [vn_ref[:, cols[h]].astype(BF16).T for h in heads]
        outs, cums = sweep(zs, vts, tri_new, [jnp.zeros((tq, 1), F32)] * SB_HEADS, valid)
        for h in heads:
            acc_ref[h] = outs[h]
            cum_ref[h] = cums[h]
        fade_ref[0] = 0

    for blk in reversed(range(span // kb)):
        lanes = slice(blk * kb, (blk + 1) * kb)

        @pl.when(fade_ref[0] == 0)
        def _():
            a = lax.broadcasted_iota(jnp.int32, (kb, kb), 0)
            b = lax.broadcasted_iota(jnp.int32, (kb, kb), 1)
            tri = (a >= b).astype(BF16)
            zs = [_dot(qs[h], ck_ref[0, 0, h, :, lanes].astype(BF16)) for h in heads]
            vts = [cv_ref[0, 0, h, :, lanes].astype(BF16) for h in heads]
            outs, cums = sweep(zs, vts, tri, [cum_ref[h] for h in heads])
            for h in heads:
                acc_ref[h] += outs[h]
                cum_ref[h] = cums[h]
            faded = jnp.max(functools.reduce(jnp.maximum, cums)) < -UNDERFLOW_LOG2
            fade_ref[0] = faded.astype(jnp.int32)

    @pl.when(j == nb - 1)
    def _():
        for h in heads:
            o_ref[:, cols[h]] = acc_ref[h]


def _sb_sample(q, k_new, v_new, cache_kt, cache_vt, layer, *, batch):
    n = q.shape[0]
    tq = n // batch
    past = cache_kt.shape[-1]
    kb = min(MXU_DIM, past)
    span = min(SAMPLE_SPAN, past)
    nb = past // span
    assert past == nb * span and span % kb == 0
    row = pl.BlockSpec((tq, SB_WIDTH), lambda b, j: (b, 0))
    cache = pl.BlockSpec((1, 1, SB_HEADS, SB_HEAD_DIM, span), lambda b, j: (layer, b, 0, 0, nb - 1 - j))
    return pl.pallas_call(
        functools.partial(_sb_sample_kernel, tq=tq, kb=kb, span=span, nb=nb),
        grid=(batch, nb),
        in_specs=[row, row, row, cache, cache],
        out_specs=row,
        out_shape=jax.ShapeDtypeStruct((n, SB_WIDTH), F32),
        scratch_shapes=[pltpu.VMEM((SB_HEADS, tq, SB_HEAD_DIM), F32),
                        pltpu.VMEM((SB_HEADS, tq, 1), F32), pltpu.SMEM((1,), jnp.int32)],
        compiler_params=_params(("arbitrary", "arbitrary")),
        name="sb_sample",
    )(q, k_new, v_new, cache_kt, cache_vt)


def _ret_log_gamma(h):
    return math.log1p(-2.0 ** (-5 - h))


def _ret_kernel(q_ref, k_ref, v_ref, s0_ref, o_ref, sout_ref, state, dec, qdec, kdec,
                *, c, per, nsteps):
    seq = pl.program_id(0)
    ci = pl.program_id(1)

    @pl.when(jnp.logical_and(seq == 0, ci == 0))
    def _():
        li = lax.broadcasted_iota(jnp.int32, (c, c), 0)
        mi = lax.broadcasted_iota(jnp.int32, (c, c), 1)
        diff = (li - mi).astype(F32)
        pos = lax.broadcasted_iota(jnp.int32, (c, RET_HEAD_DIM), 0).astype(F32)
        for h in range(RET_HEADS):
            lg = _ret_log_gamma(h)
            dec[h] = jnp.where(diff >= 0.0, jnp.exp(jnp.maximum(diff, 0.0) * lg), 0.0)
            qdec[h] = jnp.exp((pos + 1.0) * lg)
            kdec[h] = jnp.exp((c - 1.0 - pos) * lg)

    @pl.when(ci == 0)
    def _():
        state[...] = s0_ref[0]

    heads = range(RET_HEADS)
    chunks = range(per)
    cols = [slice(h * RET_HEAD_DIM, (h + 1) * RET_HEAD_DIM) for h in heads]
    rows = [slice(a * c, (a + 1) * c) for a in chunks]
    qs = [[q_ref[rows[a], cols[h]] for h in heads] for a in chunks]
    ks = [[k_ref[rows[a], cols[h]] for h in heads] for a in chunks]
    vs = [[v_ref[rows[a], cols[h]] for h in heads] for a in chunks]
    scores = [[_dot_nt(qs[a][h], ks[a][h]) for h in heads] for a in chunks]
    kds = [[(ks[a][h].astype(F32) * kdec[h]).astype(BF16) for h in heads] for a in chunks]
    grown = [[_dot_tn(kds[a][h], vs[a][h]) for h in heads] for a in chunks]
    sts = [state[h] for h in heads]
    cross = []
    for a in chunks:
        cross.append([_dot(qs[a][h], sts[h].astype(BF16)) for h in heads])
        sts = [math.exp(c * _ret_log_gamma(h)) * sts[h] + grown[a][h] for h in heads]
    inner = [[_dot((scores[a][h] * dec[h]).astype(BF16), vs[a][h]) for h in heads] for a in chunks]
    for a in chunks:
        for h in heads:
            o_ref[rows[a], cols[h]] = inner[a][h] + cross[a][h] * qdec[h]
    for h in heads:
        state[h] = sts[h]

    @pl.when(ci == nsteps - 1)
    def _():
        sout_ref[0] = state[...]


def _retention(q, k, v, state0, *, batch):
    n = q.shape[0]
    t = n // batch
    c = min(RET_CHUNK, t)
    per = RET_GROUP if (t // c) % RET_GROUP == 0 else 1
    nsteps = t // (c * per)
    row = pl.BlockSpec((per * c, RET_WIDTH), lambda b, i: (b * nsteps + i, 0))
    st = pl.BlockSpec((1, RET_HEADS, RET_HEAD_DIM, RET_HEAD_DIM), lambda b, i: (b, 0, 0, 0))
    return pl.pallas_call(
        functools.partial(_ret_kernel, c=c, per=per, nsteps=nsteps),
        grid=(batch, nsteps),
        in_specs=[row, row, row, st],
        out_specs=[row, st],
        out_shape=[jax.ShapeDtypeStruct((n, RET_WIDTH), F32),
                   jax.ShapeDtypeStruct((batch, RET_HEADS, RET_HEAD_DIM, RET_HEAD_DIM), F32)],
        scratch_shapes=[pltpu.VMEM((RET_HEADS, RET_HEAD_DIM, RET_HEAD_DIM), F32),
                        pltpu.VMEM((RET_HEADS, c, c), F32),
                        pltpu.VMEM((RET_HEADS, c, RET_HEAD_DIM), F32),
                        pltpu.VMEM((RET_HEADS, c, RET_HEAD_DIM), F32)],
        compiler_params=_params(("arbitrary", "arbitrary")),
        name="retention",
    )(q, k, v, state0)


def _merge_ffn_kernel(x_ref, osb_ref, oret_ref, gate_ref, gt1_ref, sc2_ref, sh2_ref, gt2_ref,
                      gsb_ref, gret_ref, gffn_ref, gfin_ref, wout_ref, wfi_ref, wfo_ref,
                      *outs, final):
    xo_ref = outs[0]
    a = _rms(osb_ref[...], gsb_ref[...]).astype(BF16)
    mix = _dot(a, wout_ref[0, 0:SB_WIDTH, :])
    gate = gate_ref[...]
    gate = gate * jax.nn.sigmoid(gate)
    for h in range(RET_HEADS):
        sl = slice(h * RET_HEAD_DIM, (h + 1) * RET_HEAD_DIM)
        r = (_rms(oret_ref[:, sl], gret_ref[:, sl]) * gate[:, sl]).astype(BF16)
        mix += _dot(r, wout_ref[0, SB_WIDTH + h * RET_HEAD_DIM:SB_WIDTH + (h + 1) * RET_HEAD_DIM, :])
    x1 = x_ref[...] + gt1_ref[0] * mix
    h2 = (_rms(x1, gffn_ref[...]) * (1.0 + sc2_ref[0]) + sh2_ref[0]).astype(BF16)
    ff = jnp.zeros_like(x1)
    for j in range(D_FF // FF_CHUNK):
        cols = slice(j * FF_CHUNK, (j + 1) * FF_CHUNK)
        ucols = slice(D_FF + j * FF_CHUNK, D_FF + (j + 1) * FF_CHUNK)
        g = _dot(h2, wfi_ref[0, :, cols])
        u = _dot(h2, wfi_ref[0, :, ucols])
        act = (g * jax.nn.sigmoid(g) * u).astype(BF16)
        ff += _dot(act, wfo_ref[0, cols, :])
    x2 = x1 + gt2_ref[0] * ff
    xo_ref[...] = x2
    if final:
        outs[1][...] = _rms(x2, gfin_ref[...])


def _merge_ffn(x, o_sb, o_ret, gate, gt1, sc2, sh2, gt2, g_sb, g_ret, g_ffn, g_fin,
               w_out_b, w_fi_b, w_fo_b, layer, *, batch, final):
    n = x.shape[0]
    tm = min(TOKEN_TILE, n)
    tiles = n // tm
    per_seq = max(tiles // batch, 1)
    mrows = gt1.shape[1]

    def row(i):
        return (i, 0)

    def mod(i):
        return (i // per_seq if mrows == 1 else i, 0, 0)

    full = pl.BlockSpec((tm, D_MODEL), row)
    half = pl.BlockSpec((tm, SB_WIDTH), row)
    mspec = pl.BlockSpec((1, mrows, D_MODEL), mod)
    n_out = 2 if final else 1
    res = pl.pallas_call(
        functools.partial(_merge_ffn_kernel, final=final),
        grid=(tiles,),
        in_specs=[full, half, half, half, mspec, mspec, mspec, mspec,
                  _resident((1, SB_WIDTH)), _resident((1, RET_WIDTH)),
                  _resident((1, D_MODEL)), _resident((1, D_MODEL)),
                  _resident((D_MODEL, D_MODEL), layer), _resident((D_MODEL, 2 * D_FF), layer),
                  _resident((D_FF, D_MODEL), layer)],
        out_specs=[full] * n_out,
        out_shape=[jax.ShapeDtypeStruct((n, D_MODEL), F32)] * n_out,
        compiler_params=_params(("arbitrary",)),
        name="merge_ffn",
    )(x, o_sb, o_ret, gate, gt1, sc2, sh2, gt2, g_sb, g_ret, g_ffn, g_fin, w_out_b, w_fi_b, w_fo_b)
    return res if final else (res[0], None)


def kernel(x_prompt, x_sample, cache_sb_k, cache_sb_v, state_ret, c_prompt, c_sample,
           g_norm_mix, g_norm_ffn, w_ada, b_ada, w_in, g_sb_out, g_ret_out, w_out,
           w_ff_in, w_ff_out, g_final):
    depth = w_in.shape[0]
    bp, sp, _ = x_prompt.shape
    bs, ts, _ = x_sample.shape
    past = cache_sb_k.shape[2]
    n_p = bp * sp
    n_s = bs * ts

    c_all = jnp.concatenate([c_prompt, c_sample], axis=0)
    pad = (-c_all.shape[0]) % BF16_SUBLANES
    mods = _modulation(jnp.pad(c_all, ((0, pad), (0, 0))), w_ada, b_ada)
    mods = mods.reshape(depth, -1, N_MOD, D_MODEL)
    mods_p = mods[:, :bp]
    mods_s = jnp.repeat(mods[:, bp:bp + bs], ts, axis=1)

    cos_p, sin_p = _rope_tables(sp, 0)
    cos_s, sin_s = _rope_tables(ts, past)
    cos_s = jnp.tile(cos_s, (bs, 1))
    sin_s = jnp.tile(sin_s, (bs, 1))

    cache_k = jnp.transpose(cache_sb_k, (0, 1, 3, 4, 2))
    cache_v = jnp.transpose(cache_sb_v, (0, 1, 3, 4, 2))
    zero_state = jnp.zeros((bp, RET_HEADS, RET_HEAD_DIM, RET_HEAD_DIM), F32)

    w_in_b = w_in.astype(BF16)
    wkvt_b = jnp.swapaxes(w_in[:, :, SB_WIDTH:3 * SB_WIDTH], 1, 2).astype(BF16)
    w_out_b = w_out.astype(BF16)
    w_fi_b = w_ff_in.astype(BF16)
    w_fo_b = w_ff_out.astype(BF16)

    xp = x_prompt.reshape(n_p, D_MODEL)
    xs = x_sample.reshape(n_s, D_MODEL)
    kv_p = (jnp.zeros((depth, bp, SB_HEADS, SB_HEAD_DIM, sp), F32),) * 2
    rp, ks, vs, rs = [], [], [], []
    yp = ys = None
    for l in range(depth):
        final = l == depth - 1
        g_mix = g_norm_mix[l].reshape(1, D_MODEL)
        g_ffn = g_norm_ffn[l].reshape(1, D_MODEL)
        g_sb = g_sb_out[l].reshape(1, SB_WIDTH)
        g_ret = g_ret_out[l].reshape(1, RET_WIDTH)
        g_fin = g_final.reshape(1, D_MODEL)

        m = [mods_p[l, :, j].reshape(bp, 1, D_MODEL) for j in range(N_MOD)]
        q, k, k16, v, vt, qr, kr, vr, gate = _proj(
            xp, m[1], m[0], g_mix, cos_p, sin_p, w_in_b, wkvt_b, kv_p, l, batch=bp, prompt=True)
        kv_p = (k, v)
        o_sb = _sb_prompt(q, k16, vt, batch=bp)
        o_ret, st = _retention(qr, kr, vr, zero_state, batch=bp)
        xp, yp = _merge_ffn(xp, o_sb, o_ret, gate, m[2], m[4], m[3], m[5], g_sb, g_ret, g_ffn,
                            g_fin, w_out_b, w_fi_b, w_fo_b, l, batch=bp, final=final)
        rp.append(st)

        m = [mods_s[l, :, j].reshape(1, n_s, D_MODEL) for j in range(N_MOD)]
        q, k, v, qr, kr, vr, gate = _proj(
            xs, m[1], m[0], g_mix, cos_s, sin_s, w_in_b, layer=l, batch=bs, prompt=False)
        o_sb = _sb_sample(q, k, v, cache_k, cache_v, l, batch=bs)
        o_ret, st = _retention(qr, kr, vr, state_ret[l], batch=bs)
        xs, ys = _merge_ffn(xs, o_sb, o_ret, gate, m[2], m[4], m[3], m[5], g_sb, g_ret, g_ffn,
                            g_fin, w_out_b, w_fi_b, w_fo_b, l, batch=bs, final=final)
        ks.append(k)
        vs.append(v)
        rs.append(st)

    def heads(a, b, t):
        return a.reshape(depth, b, t, SB_HEADS, SB_HEAD_DIM)

    def rows(a):
        return jnp.transpose(a, (0, 1, 4, 2, 3))

    return (yp.reshape(bp, sp, D_MODEL), ys.reshape(bs, ts, D_MODEL),
            rows(kv_p[0]), rows(kv_p[1]), jnp.stack(rp),
            heads(jnp.stack(ks), bs, ts), heads(jnp.stack(vs), bs, ts), jnp.stack(rs))
```

```python
import functools
import math

import jax
import jax.numpy as jnp
from jax import lax
from jax.experimental import pallas as pl
from jax.experimental.pallas import tpu as pltpu

F32 = jnp.float32
BF16 = jnp.bfloat16

D_MODEL = 1024
SB_WIDTH = 512
RET_WIDTH = 512
SB_HEAD_DIM = 64
SB_HEADS = SB_WIDTH // SB_HEAD_DIM
RET_HEAD_DIM = 128
RET_HEADS = RET_WIDTH // RET_HEAD_DIM
IN_WIDTH = 3 * SB_WIDTH + 4 * RET_WIDTH
D_FF = 2816
N_MOD = 6
ROPE_BASE = 10000.0
EPS = 1e-6
LOG2E = 1.4426950408889634

LANES = 128
BF16_SUBLANES = 16
MXU_DIM = 256
VMEM_LIMIT_BYTES = 56 * 1024 * 1024

TOKEN_TILE = 512
ROPE_ROWS = 2048
SB_TILE = MXU_DIM
SB_GROUP = 8
SB_UNROLL = 3
SAMPLE_SPAN = 2048
RET_CHUNK = MXU_DIM
RET_GROUP = 8
FF_CHUNK = MXU_DIM
SOFTPLUS_CLAMP = 30.0
SHIFT_OUT = 1e30
UNDERFLOW_LOG2 = 160.0


def _dot(a, b):
    return jnp.dot(a, b, preferred_element_type=F32)


def _dot_nt(a, b):
    return lax.dot_general(a, b, (((1,), (1,)), ((), ())), preferred_element_type=F32)


def _dot_tn(a, b):
    return lax.dot_general(a, b, (((0,), (0,)), ((), ())), preferred_element_type=F32)


def _split_bf16(x):
    hi = x.astype(BF16)
    lo = (x - hi.astype(F32)).astype(BF16)
    return hi, lo


def _rms(x, g):
    return x * lax.rsqrt(jnp.mean(x * x, axis=-1, keepdims=True) + EPS) * g


def _softplus2(z2):
    e = jnp.exp2(jnp.minimum(z2, SOFTPLUS_CLAMP))
    return jnp.maximum(z2, jnp.log(1.0 + e) * LOG2E)


def _params(semantics):
    return pltpu.CompilerParams(dimension_semantics=semantics, vmem_limit_bytes=VMEM_LIMIT_BYTES)


def _resident(shape, layer=None):
    if layer is None:
        index = (0,) * len(shape)
    else:
        index = (layer,) + (0,) * len(shape)
        shape = (1,) + tuple(shape)
    return pl.BlockSpec(shape, lambda *_: index, pipeline_mode=pl.Buffered(1))


def _mod_kernel(c_ref, w_ref, b_ref, o_ref):
    c = c_ref[...]
    a = c * jax.nn.sigmoid(c)
    a_hi, a_lo = _split_bf16(a)
    w_hi, w_lo = _split_bf16(w_ref[0])
    o_ref[0] = _dot(a_hi, w_hi) + _dot(a_lo, w_hi) + _dot(a_hi, w_lo) + b_ref[0]


def _modulation(c, w_ada, b_ada):
    depth = w_ada.shape[0]
    rows = c.shape[0]
    return pl.pallas_call(
        _mod_kernel,
        grid=(depth, N_MOD),
        in_specs=[
            pl.BlockSpec((rows, D_MODEL), lambda l, j: (0, 0)),
            pl.BlockSpec((1, D_MODEL, D_MODEL), lambda l, j: (l, 0, j)),
            pl.BlockSpec((1, 1, D_MODEL), lambda l, j: (l, 0, j)),
        ],
        out_specs=pl.BlockSpec((1, rows, D_MODEL), lambda l, j: (l, 0, j)),
        out_shape=jax.ShapeDtypeStruct((depth, rows, N_MOD * D_MODEL), F32),
        compiler_params=_params(("arbitrary", "arbitrary")),
        name="modulation",
    )(c, w_ada, b_ada.reshape(depth, 1, N_MOD * D_MODEL))


def _rope_kernel(inv_ref, cos_ref, sin_ref, *, rows, pos0):
    row = lax.broadcasted_iota(jnp.int32, (rows, LANES), 0) + (pl.program_id(0) * rows + pos0)
    lane = lax.broadcasted_iota(jnp.int32, (rows, LANES), 1)
    ang = row.astype(F32) * inv_ref[...]
    sin = jnp.sin(ang)
    cos_ref[...] = jnp.cos(ang)
    sin_ref[...] = jnp.where(lane < RET_HEAD_DIM // 2, -sin, sin)


def _rope_tables(n, pos0):
    half = RET_HEAD_DIM // 2
    inv = ROPE_BASE ** (-jnp.arange(half, dtype=F32) / half)
    inv2 = jnp.concatenate([inv, inv]).reshape(1, LANES)
    rows = min(n, ROPE_ROWS)
    assert n % rows == 0
    return pl.pallas_call(
        functools.partial(_rope_kernel, rows=rows, pos0=pos0),
        grid=(n // rows,),
        in_specs=[pl.BlockSpec((1, LANES), lambda i: (0, 0))],
        out_specs=[pl.BlockSpec((rows, LANES), lambda i: (i, 0))] * 2,
        out_shape=[jax.ShapeDtypeStruct((n, LANES), F32)] * 2,
        compiler_params=_params(("arbitrary",)),
        name="rope_tables",
    )(inv2)


def _proj_kernel(x_ref, sc_ref, sh_ref, g_ref, cos_ref, sin_ref, w_ref, *rest, tm, kb, prompt):
    if prompt:
        wkvt_ref, _, _, q_ref, kt_ref, kb_ref, vtf_ref, vt_ref, qr_ref, kr_ref, vr_ref, gate_ref = rest
    else:
        q_ref, k_ref, v_ref, qr_ref, kr_ref, vr_ref, gate_ref = rest
    h = _rms(x_ref[...], g_ref[...]) * (1.0 + sc_ref[0]) + sh_ref[0]
    hb = h.astype(BF16)

    def seg(a, b):
        return _dot(hb, w_ref[0, :, a:b])

    q_ref[...] = (seg(0, SB_WIDTH) * (SB_HEAD_DIM ** -0.5 * LOG2E)).astype(BF16)
    k = seg(SB_WIDTH, 2 * SB_WIDTH)
    if prompt:
        kb_ref[...] = k.astype(BF16)
        kvt = _dot_nt(wkvt_ref[0], hb)
        kt_ref[0, 0] = kvt[:SB_WIDTH].reshape(SB_HEADS, SB_HEAD_DIM, tm)
        vtf_ref[0, 0] = kvt[SB_WIDTH:].reshape(SB_HEADS, SB_HEAD_DIM, tm)
        vt = kvt[SB_WIDTH:].astype(BF16)
        for hp in range(SB_WIDTH // LANES):
            for j in range(tm // kb):
                vt_ref[0, hp, j] = vt[hp * LANES:(hp + 1) * LANES, j * kb:(j + 1) * kb]
    else:
        k_ref[...] = k
        v_ref[...] = seg(2 * SB_WIDTH, 3 * SB_WIDTH)
    cos = cos_ref[...]
    sin = sin_ref[...]
    base = 3 * SB_WIDTH
    qr = seg(base, base + RET_WIDTH)
    kr = seg(base + RET_WIDTH, base + 2 * RET_WIDTH)
    for hh in range(RET_HEADS):
        sl = slice(hh * RET_HEAD_DIM, (hh + 1) * RET_HEAD_DIM)
        qh = qr[:, sl]
        kh = kr[:, sl]
        qr_ref[:, sl] = (qh * cos + pltpu.roll(qh, RET_HEAD_DIM // 2, 1) * sin).astype(BF16)
        kr_ref[:, sl] = ((kh * cos + pltpu.roll(kh, RET_HEAD_DIM // 2, 1) * sin)
                         * RET_HEAD_DIM ** -0.5).astype(BF16)
    vr_ref[...] = seg(base + 2 * RET_WIDTH, base + 3 * RET_WIDTH).astype(BF16)
    gate_ref[...] = seg(base + 3 * RET_WIDTH, base + 4 * RET_WIDTH)


def _proj(x, sc, sh, g, cos, sin, w_in_b, wkvt_b=None, kv_out=None, layer=0, *, batch, prompt):
    n = x.shape[0]
    tm = min(TOKEN_TILE, n)
    tiles = n // tm
    per_seq = max(tiles // batch, 1)
    pos_tiles = cos.shape[0] // tm
    kb = SB_TILE
    mrows = sc.shape[1]

    def row(i):
        return (i, 0)

    def mod(i):
        return (i // per_seq if mrows == 1 else i, 0, 0)

    def pos(i):
        return (i % pos_tiles, 0)

    in_specs = [
        pl.BlockSpec((tm, D_MODEL), row),
        pl.BlockSpec((1, mrows, D_MODEL), mod),
        pl.BlockSpec((1, mrows, D_MODEL), mod),
        _resident((1, D_MODEL)),
        pl.BlockSpec((tm, LANES), pos),
        pl.BlockSpec((tm, LANES), pos),
        _resident((D_MODEL, IN_WIDTH), layer),
    ]
    args = [x, sc, sh, g, cos, sin, w_in_b]
    wide_b = jax.ShapeDtypeStruct((n, SB_WIDTH), BF16)
    wide_f = jax.ShapeDtypeStruct((n, SB_WIDTH), F32)
    wide = pl.BlockSpec((tm, SB_WIDTH), row)
    if prompt:
        in_specs += [_resident((2 * SB_WIDTH, D_MODEL), layer)] + [pl.BlockSpec(memory_space=pl.ANY)] * 2
        aliases = {len(args) + 1: 1, len(args) + 2: 3}
        args += [wkvt_b, *kv_out]
        nkb = n // batch // kb
        hps = SB_WIDTH // LANES
        head_f = jax.ShapeDtypeStruct(kv_out[0].shape, F32)
        head = pl.BlockSpec((1, 1, SB_HEADS, SB_HEAD_DIM, tm),
                            lambda i: (layer, i // per_seq, 0, 0, i % per_seq))
        out_shape = [wide_b, head_f, wide_b, head_f,
                     jax.ShapeDtypeStruct((batch, hps, nkb, LANES, kb), BF16),
                     wide_b, wide_b, wide_b, wide_f]
        out_specs = [wide, head, wide, head,
                     pl.BlockSpec((1, hps, tm // kb, LANES, kb),
                                  lambda i: (i // per_seq, 0, i % per_seq, 0, 0)),
                     wide, wide, wide, wide]
    else:
        aliases = {}
        out_shape = [wide_b, wide_f, wide_f, wide_b, wide_b, wide_b, wide_f]
        out_specs = [wide] * 7
    return pl.pallas_call(
        functools.partial(_proj_kernel, tm=tm, kb=kb, prompt=prompt),
        grid=(tiles,),
        in_specs=in_specs,
        out_specs=out_specs,
        out_shape=out_shape,
        input_output_aliases=aliases,
        compiler_params=_params(("arbitrary",)),
        name="proj_prompt" if prompt else "proj_sample",
    )(*args)


def _sb_prompt_kernel(q_ref, k_ref, vt_ref, o_ref, acc_ref, zt_ref, s_ref, w_ref, *, t, nq, group):
    first_tile = pl.program_id(2) * group
    hd = SB_HEAD_DIM
    heads = range(2)
    tiles = range(group)
    chains = [(g, h) for g in tiles for h in heads]
    lane = lax.broadcasted_iota(jnp.int32, (t, LANES), 1)
    qms = []
    for g in tiles:
        q = q_ref[0, g * t:(g + 1) * t, :]
        qms.append((jnp.where(lane < hd, q, jnp.zeros_like(q)),
                    jnp.where(lane >= hd, q, jnp.zeros_like(q))))
    kk = lax.broadcasted_iota(jnp.int32, (t, t), 0)
    qq = lax.broadcasted_iota(jnp.int32, (t, t), 1)
    tri = (qq >= kk).astype(BF16)
    valid = kk < qq

    def keys(j):
        return k_ref[0, pl.ds(pl.multiple_of(j * t, t), t), :]

    def gather(j, ws):
        vblk = vt_ref[0, 0, j]
        return [_dot(vblk[h * hd:(h + 1) * hd, :], ws[h]) for h in heads]

    def weights(zt, r, carried):
        return jnp.exp2(jnp.minimum(zt - r, 0.0) + carried)

    diag = [first_tile + g for g in tiles]
    left = [jnp.maximum(diag[g] - 1, 0) for g in tiles]
    shift_left = [jnp.where(diag[g] > 0, 0.0, -SHIFT_OUT) for g in tiles]
    k_diag = [keys(diag[g]) for g in tiles]
    k_left = [keys(left[g]) for g in tiles]
    z_diag = [_dot_nt(k_diag[g], qms[g][h]) for g, h in chains]
    z_left = [_dot_nt(k_left[g], qms[g][h]) for g, h in chains]
    s_diag = [jnp.where(valid, _softplus2(z), 0.0).astype(BF16) for z in z_diag]
    s_left = [_softplus2(z).astype(BF16) for z in z_left]
    r_diag = [_dot(tri, s) for s in s_diag]
    r_left = [_dot(tri, s) for s in s_left]
    cum_diag = [-r[0:1, :] for r in r_diag]
    w_diag = [jnp.where(valid, weights(z_diag[c], r_diag[c], 0.0), 0.0).astype(BF16)
              for c in range(len(chains))]
    w_left = [weights(z_left[c], r_left[c], cum_diag[c] + shift_left[chains[c][0]]).astype(BF16)
              for c in range(len(chains))]
    cum_left = [cum_diag[c] - r_left[c][0:1, :] for c in range(len(chains))]
    for g in tiles:
        p_diag = gather(diag[g], w_diag[2 * g:2 * g + 2])
        p_left = gather(left[g], w_left[2 * g:2 * g + 2])
        for h in heads:
            acc_ref[g, h * hd:(h + 1) * hd, :] = p_diag[h] + p_left[h]

    def all_faded(cums):
        return jnp.max(jnp.maximum(cums[0], cums[1])) < -UNDERFLOW_LOG2

    def rest_of_sweep(g, cums):
        i = diag[g]
        qm = qms[g]

        def block_of(m):
            return jnp.clip(i - 2 - m, 0, nq - 1)

        def scores(j):
            kblk = keys(j)
            return [_dot_nt(kblk, qm[h]) for h in heads]

        rounds = jnp.where(i > 1, (i + SB_UNROLL) // SB_UNROLL, 0)
        faded = all_faded(cums)

        @pl.when(jnp.logical_and(rounds > 0, jnp.logical_not(faded)))
        def _():
            first = scores(block_of(0))
            for h in heads:
                zt_ref[0, h] = first[h]
                zt_ref[2, h] = jnp.zeros((t, t), F32)
                s_ref[2, h] = jnp.zeros((t, t), BF16)
                w_ref[1, h] = jnp.zeros((t, t), BF16)

        def trip(n, r, cums):
            nxt_slot, cur, old, done = (r + 1) % 3, r, (r + 2) % 3, (r + 1) % 3
            shift = jnp.where(jnp.logical_and(n >= 1, n < i), 0.0, -SHIFT_OUT)
            rs = [_dot(tri, s_ref[old, h]) for h in heads]
            new_ws = [weights(zt_ref[old, h], rs[h], cums[h] + shift) for h in heads]
            new_cums = [cums[h] - rs[h][0:1, :] for h in heads]
            nxt = scores(block_of(n + 1))
            parts = gather(block_of(n - 2), [w_ref[done, h] for h in heads])
            for h in heads:
                s_ref[cur, h] = _softplus2(zt_ref[cur, h]).astype(BF16)
            for h in heads:
                acc_ref[g, h * hd:(h + 1) * hd, :] += parts[h]
                w_ref[old, h] = new_ws[h].astype(BF16)
                zt_ref[nxt_slot, h] = nxt[h]
            return tuple(new_cums)

        def more(state):
            p, faded, _ = state
            return jnp.logical_and(p < rounds, jnp.logical_not(faded))

        def body(state):
            p, faded, cums = state
            for r in range(SB_UNROLL):
                cums = trip(SB_UNROLL * p + r, r % 3, cums)
                if r == SB_UNROLL - 2:
                    faded = all_faded(cums)
            return p + 1, faded, cums

        lax.while_loop(more, body, (jnp.int32(0), faded, cums))

    for g in tiles:
        rest_of_sweep(g, tuple(cum_left[2 * g:2 * g + 2]))
        o_ref[0, g * t:(g + 1) * t, :] = acc_ref[g].T


def _sb_prompt(q, kb16, vt, *, batch):
    n = q.shape[0]
    s = n // batch
    t = SB_TILE
    nq = s // t
    group = SB_GROUP if nq % SB_GROUP == 0 else 1
    hps = SB_WIDTH // LANES
    q3 = q.reshape(batch, s, SB_WIDTH)
    k3 = kb16.reshape(batch, s, SB_WIDTH)
    out = pl.pallas_call(
        functools.partial(_sb_prompt_kernel, t=t, nq=nq, group=group),
        grid=(batch, hps, nq // group),
        in_specs=[
            pl.BlockSpec((1, group * t, LANES), lambda b, hp, i: (b, i, hp)),
            pl.BlockSpec((1, s, LANES), lambda b, hp, i: (b, 0, hp)),
            pl.BlockSpec((1, 1, nq, LANES, t), lambda b, hp, i: (b, hp, 0, 0, 0)),
        ],
        out_specs=pl.BlockSpec((1, group * t, LANES), lambda b, hp, i: (b, i, hp)),
        out_shape=jax.ShapeDtypeStruct((batch, s, SB_WIDTH), F32),
        scratch_shapes=[pltpu.VMEM((group, LANES, t), F32), pltpu.VMEM((3, 2, t, t), F32),
                        pltpu.VMEM((3, 2, t, t), BF16), pltpu.VMEM((3, 2, t, t), BF16)],
        compiler_params=_params(("arbitrary", "arbitrary", "arbitrary")),
        name="sb_prompt",
    )(q3, k3, vt)
    return out.reshape(n, SB_WIDTH)


def _sb_sample_kernel(q_ref, kn_ref, vn_ref, ck_ref, cv_ref, o_ref, acc_ref, cum_ref, fade_ref,
                      *, tq, kb, span, nb):
    j = pl.program_id(1)
    hd = SB_HEAD_DIM
    heads = range(SB_HEADS)
    cols = [slice(h * hd, (h + 1) * hd) for h in heads]
    qs = [q_ref[:, c] for c in cols]

    def cumsum_right(s, m):
        s_hi, s_lo = _split_bf16(s)
        return _dot(s_hi, m) + _dot(s_lo, m)

    def sweep(zs, vts, m, cums, mask=None):
        ss = [_softplus2(z) for z in zs]
        if mask is not None:
            ss = [jnp.where(mask, s, 0.0) for s in ss]
        rs = [cumsum_right(s, m) for s in ss]
        ws = [jnp.exp2(jnp.minimum(zs[h] - rs[h], 0.0) + cums[h]) for h in heads]
        if mask is not None:
            ws = [jnp.where(mask, w, 0.0) for w in ws]
        outs = [_dot_nt(ws[h].astype(BF16), vts[h]) for h in heads]
        return outs, [cums[h] - rs[h][:, 0:1] for h in heads]

    @pl.when(j == 0)
    def _():
        qi = lax.broadcasted_iota(jnp.int32, (tq, tq), 0)
        ki = lax.broadcasted_iota(jnp.int32, (tq, tq), 1)
        valid = ki < qi
        tri_new = (qi >= ki).astype(BF16)
        zs = [_dot_nt(qs[h], kn_ref[:, cols[h]].astype(BF16)) for h in heads]
        vts = [vn_ref[:, cols[h]].astype(BF16).T for h in heads]
        outs, cums = sweep(zs, vts, tri_new, [jnp.zeros((tq, 1), F32)] * SB_HEADS, valid)
        for h in heads:
            acc_ref[h] = outs[h]
            cum_ref[h] = cums[h]
        fade_ref[0] = 0

    for blk in reversed(range(span // kb)):
        lanes = slice(blk * kb, (blk + 1) * kb)

        @pl.when(fade_ref[0] == 0)
        def _():
            a = lax.broadcasted_iota(jnp.int32, (kb, kb), 0)
            b = lax.broadcasted_iota(jnp.int32, (kb, kb), 1)
            tri = (a >= b).astype(BF16)
            zs = [_dot(qs[h], ck_ref[0, 0, h, :, lanes].astype(BF16)) for h in heads]
            vts = [cv_ref[0, 0, h, :, lanes].astype(BF16) for h in heads]
            outs, cums = sweep(zs, vts, tri, [cum_ref[h] for h in heads])
            for h in heads:
                acc_ref[h] += outs[h]
                cum_ref[h] = cums[h]
            faded = jnp.max(functools.reduce(jnp.maximum, cums)) < -UNDERFLOW_LOG2
            fade_ref[0] = faded.astype(jnp.int32)

    @pl.when(j == nb - 1)
    def _():
        for h in heads:
            o_ref[:, cols[h]] = acc_ref[h]


def _sb_sample(q, k_new, v_new, cache_kt, cache_vt, layer, *, batch):
    n = q.shape[0]
    tq = n // batch
    past = cache_kt.shape[-1]
    kb = min(MXU_DIM, past)
    span = min(SAMPLE_SPAN, past)
    nb = past // span
    assert past == nb * span and span % kb == 0
    row = pl.BlockSpec((tq, SB_WIDTH), lambda b, j: (b, 0))
    cache = pl.BlockSpec((1, 1, SB_HEADS, SB_HEAD_DIM, span), lambda b, j: (layer, b, 0, 0, nb - 1 - j))
    return pl.pallas_call(
        functools.partial(_sb_sample_kernel, tq=tq, kb=kb, span=span, nb=nb),
        grid=(batch, nb),
        in_specs=[row, row, row, cache, cache],
        out_specs=row,
        out_shape=jax.ShapeDtypeStruct((n, SB_WIDTH), F32),
        scratch_shapes=[pltpu.VMEM((SB_HEADS, tq, SB_HEAD_DIM), F32),
                        pltpu.VMEM((SB_HEADS, tq, 1), F32), pltpu.SMEM((1,), jnp.int32)],
        compiler_params=_params(("arbitrary", "arbitrary")),
        name="sb_sample",
    )(q, k_new, v_new, cache_kt, cache_vt)


def _ret_log_gamma(h):
    return math.log1p(-2.0 ** (-5 - h))


def _ret_kernel(q_ref, k_ref, v_ref, s0_ref, o_ref, sout_ref, state, dec, qdec, kdec,
                *, c, per, nsteps):
    seq = pl.program_id(0)
    ci = pl.program_id(1)

    @pl.when(jnp.logical_and(seq == 0, ci == 0))
    def _():
        li = lax.broadcasted_iota(jnp.int32, (c, c), 0)
        mi = lax.broadcasted_iota(jnp.int32, (c, c), 1)
        diff = (li - mi).astype(F32)
        pos = lax.broadcasted_iota(jnp.int32, (c, RET_HEAD_DIM), 0).astype(F32)
        for h in range(RET_HEADS):
            lg = _ret_log_gamma(h)
            dec[h] = jnp.where(diff >= 0.0, jnp.exp(jnp.maximum(diff, 0.0) * lg), 0.0)
            qdec[h] = jnp.exp((pos + 1.0) * lg)
            kdec[h] = jnp.exp((c - 1.0 - pos) * lg)

    @pl.when(ci == 0)
    def _():
        state[...] = s0_ref[0]

    heads = range(RET_HEADS)
    chunks = range(per)
    cols = [slice(h * RET_HEAD_DIM, (h + 1) * RET_HEAD_DIM) for h in heads]
    rows = [slice(a * c, (a + 1) * c) for a in chunks]
    qs = [[q_ref[rows[a], cols[h]] for h in heads] for a in chunks]
    ks = [[k_ref[rows[a], cols[h]] for h in heads] for a in chunks]
    vs = [[v_ref[rows[a], cols[h]] for h in heads] for a in chunks]
    scores = [[_dot_nt(qs[a][h], ks[a][h]) for h in heads] for a in chunks]
    kds = [[(ks[a][h].astype(F32) * kdec[h]).astype(BF16) for h in heads] for a in chunks]
    grown = [[_dot_tn(kds[a][h], vs[a][h]) for h in heads] for a in chunks]
    sts = [state[h] for h in heads]
    cross = []
    for a in chunks:
        cross.append([_dot(qs[a][h], sts[h].astype(BF16)) for h in heads])
        sts = [math.exp(c * _ret_log_gamma(h)) * sts[h] + grown[a][h] for h in heads]
    inner = [[_dot((scores[a][h] * dec[h]).astype(BF16), vs[a][h]) for h in heads] for a in chunks]
    for a in chunks:
        for h in heads:
            o_ref[rows[a], cols[h]] = inner[a][h] + cross[a][h] * qdec[h]
    for h in heads:
        state[h] = sts[h]

    @pl.when(ci == nsteps - 1)
    def _():
        sout_ref[0] = state[...]


def _retention(q, k, v, state0, *, batch):
    n = q.shape[0]
    t = n // batch
    c = min(RET_CHUNK, t)
    per = RET_GROUP if (t // c) % RET_GROUP == 0 else 1
    nsteps = t // (c * per)
    row = pl.BlockSpec((per * c, RET_WIDTH), lambda b, i: (b * nsteps + i, 0))
    st = pl.BlockSpec((1, RET_HEADS, RET_HEAD_DIM, RET_HEAD_DIM), lambda b, i: (b, 0, 0, 0))
    return pl.pallas_call(
        functools.partial(_ret_kernel, c=c, per=per, nsteps=nsteps),
        grid=(batch, nsteps),
        in_specs=[row, row, row, st],
        out_specs=[row, st],
        out_shape=[jax.ShapeDtypeStruct((n, RET_WIDTH), F32),
                   jax.ShapeDtypeStruct((batch, RET_HEADS, RET_HEAD_DIM, RET_HEAD_DIM), F32)],
        scratch_shapes=[pltpu.VMEM((RET_HEADS, RET_HEAD_DIM, RET_HEAD_DIM), F32),
                        pltpu.VMEM((RET_HEADS, c, c), F32),
                        pltpu.VMEM((RET_HEADS, c, RET_HEAD_DIM), F32),
                        pltpu.VMEM((RET_HEADS, c, RET_HEAD_DIM), F32)],
        compiler_params=_params(("arbitrary", "arbitrary")),
        name="retention",
    )(q, k, v, state0)


def _merge_ffn_kernel(x_ref, osb_ref, oret_ref, gate_ref, gt1_ref, sc2_ref, sh2_ref, gt2_ref,
                      gsb_ref, gret_ref, gffn_ref, gfin_ref, wout_ref, wfi_ref, wfo_ref,
                      *outs, final):
    xo_ref = outs[0]
    a = _rms(osb_ref[...], gsb_ref[...]).astype(BF16)
    mix = _dot(a, wout_ref[0, 0:SB_WIDTH, :])
    gate = gate_ref[...]
    gate = gate * jax.nn.sigmoid(gate)
    for h in range(RET_HEADS):
        sl = slice(h * RET_HEAD_DIM, (h + 1) * RET_HEAD_DIM)
        r = (_rms(oret_ref[:, sl], gret_ref[:, sl]) * gate[:, sl]).astype(BF16)
        mix += _dot(r, wout_ref[0, SB_WIDTH + h * RET_HEAD_DIM:SB_WIDTH + (h + 1) * RET_HEAD_DIM, :])
    x1 = x_ref[...] + gt1_ref[0] * mix
    h2 = (_rms(x1, gffn_ref[...]) * (1.0 + sc2_ref[0]) + sh2_ref[0]).astype(BF16)
    ff = jnp.zeros_like(x1)
    for j in range(D_FF // FF_CHUNK):
        cols = slice(j * FF_CHUNK, (j + 1) * FF_CHUNK)
        ucols = slice(D_FF + j * FF_CHUNK, D_FF + (j + 1) * FF_CHUNK)
        g = _dot(h2, wfi_ref[0, :, cols])
        u = _dot(h2, wfi_ref[0, :, ucols])
        act = (g * jax.nn.sigmoid(g) * u).astype(BF16)
        ff += _dot(act, wfo_ref[0, cols, :])
    x2 = x1 + gt2_ref[0] * ff
    xo_ref[...] = x2
    if final:
        outs[1][...] = _rms(x2, gfin_ref[...])


def _merge_ffn(x, o_sb, o_ret, gate, gt1, sc2, sh2, gt2, g_sb, g_ret, g_ffn, g_fin,
               w_out_b, w_fi_b, w_fo_b, layer, *, batch, final):
    n = x.shape[0]
    tm = min(TOKEN_TILE, n)
    tiles = n // tm
    per_seq = max(tiles // batch, 1)
    mrows = gt1.shape[1]

    def row(i):
        return (i, 0)

    def mod(i):
        return (i // per_seq if mrows == 1 else i, 0, 0)

    full = pl.BlockSpec((tm, D_MODEL), row)
    half = pl.BlockSpec((tm, SB_WIDTH), row)
    mspec = pl.BlockSpec((1, mrows, D_MODEL), mod)
    n_out = 2 if final else 1
    res = pl.pallas_call(
        functools.partial(_merge_ffn_kernel, final=final),
        grid=(tiles,),
        in_specs=[full, half, half, half, mspec, mspec, mspec, mspec,
                  _resident((1, SB_WIDTH)), _resident((1, RET_WIDTH)),
                  _resident((1, D_MODEL)), _resident((1, D_MODEL)),
                  _resident((D_MODEL, D_MODEL), layer), _resident((D_MODEL, 2 * D_FF), layer),
                  _resident((D_FF, D_MODEL), layer)],
        out_specs=[full] * n_out,
        out_shape=[jax.ShapeDtypeStruct((n, D_MODEL), F32)] * n_out,
        compiler_params=_params(("arbitrary",)),
        name="merge_ffn",
    )(x, o_sb, o_ret, gate, gt1, sc2, sh2, gt2, g_sb, g_ret, g_ffn, g_fin, w_out_b, w_fi_b, w_fo_b)
    return res if final else (res[0], None)


def kernel(x_prompt, x_sample, cache_sb_k, cache_sb_v, state_ret, c_prompt, c_sample,
           g_norm_mix, g_norm_ffn, w_ada, b_ada, w_in, g_sb_out, g_ret_out, w_out,
           w_ff_in, w_ff_out, g_final):
    depth = w_in.shape[0]
    bp, sp, _ = x_prompt.shape
    bs, ts, _ = x_sample.shape
    past = cache_sb_k.shape[2]
    n_p = bp * sp
    n_s = bs * ts

    c_all = jnp.concatenate([c_prompt, c_sample], axis=0)
    pad = (-c_all.shape[0]) % BF16_SUBLANES
    mods = _modulation(jnp.pad(c_all, ((0, pad), (0, 0))), w_ada, b_ada)
    mods = mods.reshape(depth, -1, N_MOD, D_MODEL)
    mods_p = mods[:, :bp]
    mods_s = jnp.repeat(mods[:, bp:bp + bs], ts, axis=1)

    cos_p, sin_p = _rope_tables(sp, 0)
    cos_s, sin_s = _rope_tables(ts, past)
    cos_s = jnp.tile(cos_s, (bs, 1))
    sin_s = jnp.tile(sin_s, (bs, 1))

    cache_k = jnp.transpose(cache_sb_k, (0, 1, 3, 4, 2))
    cache_v = jnp.transpose(cache_sb_v, (0, 1, 3, 4, 2))
    zero_state = jnp.zeros((bp, RET_HEADS, RET_HEAD_DIM, RET_HEAD_DIM), F32)

    w_in_b = w_in.astype(BF16)
    wkvt_b = jnp.swapaxes(w_in[:, :, SB_WIDTH:3 * SB_WIDTH], 1, 2).astype(BF16)
    w_out_b = w_out.astype(BF16)
    w_fi_b = w_ff_in.astype(BF16)
    w_fo_b = w_ff_out.astype(BF16)

    xp = x_prompt.reshape(n_p, D_MODEL)
    xs = x_sample.reshape(n_s, D_MODEL)
    kv_p = (jnp.zeros((depth, bp, SB_HEADS, SB_HEAD_DIM, sp), F32),) * 2
    rp, ks, vs, rs = [], [], [], []
    yp = ys = None
    for l in range(depth):
        final = l == depth - 1
        g_mix = g_norm_mix[l].reshape(1, D_MODEL)
        g_ffn = g_norm_ffn[l].reshape(1, D_MODEL)
        g_sb = g_sb_out[l].reshape(1, SB_WIDTH)
        g_ret = g_ret_out[l].reshape(1, RET_WIDTH)
        g_fin = g_final.reshape(1, D_MODEL)

        m = [mods_p[l, :, j].reshape(bp, 1, D_MODEL) for j in range(N_MOD)]
        q, k, k16, v, vt, qr, kr, vr, gate = _proj(
            xp, m[1], m[0], g_mix, cos_p, sin_p, w_in_b, wkvt_b, kv_p, l, batch=bp, prompt=True)
        kv_p = (k, v)
        o_sb = _sb_prompt(q, k16, vt, batch=bp)
        o_ret, st = _retention(qr, kr, vr, zero_state, batch=bp)
        xp, yp = _merge_ffn(xp, o_sb, o_ret, gate, m[2], m[4], m[3], m[5], g_sb, g_ret, g_ffn,
                            g_fin, w_out_b, w_fi_b, w_fo_b, l, batch=bp, final=final)
        rp.append(st)

        m = [mods_s[l, :, j].reshape(1, n_s, D_MODEL) for j in range(N_MOD)]
        q, k, v, qr, kr, vr, gate = _proj(
            xs, m[1], m[0], g_mix, cos_s, sin_s, w_in_b, layer=l, batch=bs, prompt=False)
        o_sb = _sb_sample(q, k, v, cache_k, cache_v, l, batch=bs)
        o_ret, st = _retention(qr, kr, vr, state_ret[l], batch=bs)
        xs, ys = _merge_ffn(xs, o_sb, o_ret, gate, m[2], m[4], m[3], m[5], g_sb, g_ret, g_ffn,
                            g_fin, w_out_b, w_fi_b, w_fo_b, l, batch=bs, final=final)
        ks.append(k)
        vs.append(v)
        rs.append(st)

    def heads(a, b, t):
        return a.reshape(depth, b, t, SB_HEADS, SB_HEAD_DIM)

    def rows(a):
        return jnp.transpose(a, (0, 1, 4, 2, 3))

    return (yp.reshape(bp, sp, D_MODEL), ys.reshape(bs, ts, D_MODEL),
            rows(kv_p[0]), rows(kv_p[1]), jnp.stack(rp),
            heads(jnp.stack(ks), bs, ts), heads(jnp.stack(vs), bs, ts), jnp.stack(rs))
```
